```python
import math
import numpy as np
import jax
import jax.numpy as jnp
from jax import lax

D_MODEL = 2048
BATCH = 4
SEQ = 2048
DEPTH = 2
DEC_BATCH = 128
DEC_SEQ = 8
PAST_LEN = 2048
PAGE_SIZE = 128

HEAD_DIM = 64
FOX_HEADS = 8
NSA_HEADS = 8
NSA_KV_HEADS = 2
NSA_GROUP = NSA_HEADS // NSA_KV_HEADS
SB_HEADS = 8
FOX_W = FOX_HEADS * HEAD_DIM
NSA_W = NSA_HEADS * HEAD_DIM
NSA_KV_W = NSA_KV_HEADS * HEAD_DIM
SB_W = SB_HEADS * HEAD_DIM
CMP_BLOCK = 32
CMP_STRIDE = 16
CMP_HIDDEN = 2 * HEAD_DIM
SEL_BLOCK = 64
SEL_TOPK = 8
WINDOW = 512
N_BUCKETS = 32
MAX_DISTANCE = 128
N_GROUPS = 4
EXPERTS_PER_GROUP = 4
N_EXPERTS = N_GROUPS * EXPERTS_PER_GROUP
EXPERT_TOPK = 2
EXPERT_FF = 512
Q_BLOCK = 128
RMS_EPS = 1e-6
FORGET_BIAS_INIT = 3.0
NEG_INF = -1e30
FORCE_SCORE = 1e9
IN_SIZES = (FOX_W, 2 * FOX_W, FOX_HEADS, NSA_W, 6 * NSA_KV_W, 3 * NSA_HEADS, SB_W, 2 * SB_W, 3 * D_MODEL)
IN_SPLITS = tuple(sum(IN_SIZES[:i + 1]) for i in range(len(IN_SIZES) - 1))
IN_W = sum(IN_SIZES)

kernel_name = 'hybrid_fox_nsa_stickbreak_hmoe_step'


def rmsnorm(x, g):
    x32 = x.astype(jnp.float32)
    y = x32 * lax.rsqrt(jnp.mean(x32 * x32, axis=-1, keepdims=True) + RMS_EPS)
    return (y * g.astype(jnp.float32)).astype(x.dtype)


def masked_softmax(logits, mask):
    l = jnp.where(mask, logits, NEG_INF)
    e = jnp.where(mask, jnp.exp(l - jnp.max(l, axis=-1, keepdims=True)), 0.0)
    return e / jnp.maximum(jnp.sum(e, axis=-1, keepdims=True), jnp.finfo(jnp.float32).tiny)


def t5_bucket(rel):
    n = jnp.maximum(rel, 0)
    exact = N_BUCKETS // 2
    far = jnp.log(jnp.maximum(n, exact).astype(jnp.float32) / exact) / math.log(MAX_DISTANCE / exact)
    far = exact + (far * (N_BUCKETS - exact)).astype(jnp.int32)
    return jnp.where(n < exact, n, jnp.minimum(far, N_BUCKETS - 1))


def to_blocks(a):
    return jnp.moveaxis(a.reshape(a.shape[0], a.shape[1] // Q_BLOCK, Q_BLOCK, *a.shape[2:]), 1, 0)


def from_blocks(a):
    a = jnp.moveaxis(a, 0, 1)
    return a.reshape(a.shape[0], a.shape[1] * a.shape[2], *a.shape[3:])


def split_projection(hn, w_in):
    b, t = hn.shape[:2]
    q_a, kv_a, f_a, q_b, kv_b, g_b, q_c, kv_c, g_m = jnp.split(hn @ w_in, IN_SPLITS, axis=-1)
    return (q_a.reshape(b, t, FOX_HEADS, HEAD_DIM),
            kv_a.reshape(b, t, 2, FOX_HEADS, HEAD_DIM),
            f_a,
            q_b.reshape(b, t, NSA_HEADS, HEAD_DIM),
            kv_b.reshape(b, t, 3, 2, NSA_KV_HEADS, HEAD_DIM),
            g_b.reshape(b, t, 3, NSA_HEADS),
            q_c.reshape(b, t, SB_HEADS, HEAD_DIM),
            kv_c.reshape(b, t, 2, SB_HEADS, HEAD_DIM),
            g_m.reshape(b, t, 3, D_MODEL))


def fox_attend(q, qpos, cq, k, v, kpos, ck):
    s = jnp.einsum('bthd,bshd->bhts', q, k, preferred_element_type=jnp.float32) * HEAD_DIM ** -0.5
    s = s + jnp.swapaxes(cq, 1, 2)[..., :, None] - jnp.swapaxes(ck, 1, 2)[..., None, :]
    p = masked_softmax(s, kpos[None, :] <= qpos[:, None])
    return jnp.einsum('bhts,bshd->bthd', p.astype(v.dtype), v)


def sb_attend(q, qpos, k, v, kpos):
    z = jnp.einsum('bthd,bshd->bhts', q, k, preferred_element_type=jnp.float32) * HEAD_DIM ** -0.5
    mask = kpos[None, :] < qpos[:, None]
    log_keep = jnp.where(mask, jax.nn.log_sigmoid(-z), 0.0)
    between = lax.cumsum(log_keep, axis=3, reverse=True) - log_keep
    a = jnp.where(mask, jnp.exp(jax.nn.log_sigmoid(z) + between), 0.0)
    return jnp.einsum('bhts,bshd->bthd', a.astype(v.dtype), v)


def nsa_compress(rows, pe, w1, w2):
    b, length = rows.shape[:2]
    n_c = (length - CMP_BLOCK) // CMP_STRIDE + 1
    ratio = CMP_BLOCK // CMP_STRIDE
    chunks = rows[:, :(n_c + ratio - 1) * CMP_STRIDE].reshape(
        b, n_c + ratio - 1, CMP_STRIDE, 2, NSA_KV_HEADS, HEAD_DIM)
    pre = None
    for r in range(ratio):
        part = slice(r * CMP_STRIDE, (r + 1) * CMP_STRIDE)
        blk = chunks[:, r:r + n_c] + jnp.swapaxes(pe[:, part], 0, 1)[:, :, None, :]
        term = jnp.einsum('bnlcgd,cldh->bncgh', blk, w1[:, part])
        pre = term if pre is None else pre + term
    comp = jnp.einsum('bncgh,chd->bncgd', jax.nn.gelu(pre), w2)
    cend = (np.arange(n_c) * CMP_STRIDE + CMP_BLOCK - 1).astype(np.int32)
    return comp, cend


def sel_to_blocks(rows):
    b, length = rows.shape[:2]
    n_s = -(-length // SEL_BLOCK)
    rows = jnp.pad(rows, ((0, 0), (0, n_s * SEL_BLOCK - length), (0, 0), (0, 0), (0, 0)))
    rows = rows.reshape(b, n_s, SEL_BLOCK, 2, NSA_KV_HEADS, HEAD_DIM)
    return jnp.transpose(rows, (0, 4, 1, 2, 3, 5))


def cmp_sel_overlap(n_c, n_s):
    cs = np.arange(n_c)[:, None] * CMP_STRIDE
    ss = np.arange(n_s)[None, :] * SEL_BLOCK
    ov = np.clip(np.minimum(cs + CMP_BLOCK, ss + SEL_BLOCK) - np.maximum(cs, ss), 0, None)
    return jnp.asarray(ov / CMP_BLOCK, dtype=jnp.float32)


def nsa_attend(q, qpos, gate, wkv, wpos, comp, cend, sel_kv, t5_table):
    b, tb = q.shape[:2]
    g_n, hg = NSA_KV_HEADS, NSA_GROUP
    scale = HEAD_DIM ** -0.5
    qg = q.reshape(b, tb, g_n, hg, HEAD_DIM)
    t5g = t5_table.astype(jnp.float32).reshape(N_BUCKETS, g_n, hg)
    rel_c = qpos[:, None] - cend[None, :]
    lc = jnp.einsum('btghd,bngd->bghtn', qg, comp[:, :, 0], preferred_element_type=jnp.float32) * scale
    lc = lc + jnp.transpose(t5g[t5_bucket(rel_c)], (2, 3, 0, 1))
    pc = masked_softmax(lc, rel_c >= 0)
    o_c = jnp.einsum('bghtn,bngd->btghd', pc.astype(q.dtype), comp[:, :, 1])
    n_c, n_s = comp.shape[1], sel_kv.shape[2]
    score = jnp.einsum('bghtn,ns->bgts', pc, cmp_sel_overlap(n_c, n_s))
    blk = jnp.arange(n_s)
    valid = blk[None, :] * SEL_BLOCK <= qpos[:, None]
    forced = (blk[None, :] == qpos[:, None] // SEL_BLOCK) | (blk[None, :] == 0)
    score = jnp.where(forced, FORCE_SCORE, jnp.where(valid, score, NEG_INF))
    k_sel = min(SEL_TOPK, n_s)
    _, sel = lax.top_k(score, k_sel)
    b_i = jnp.arange(b)[:, None, None, None]
    g_i = jnp.arange(g_n)[None, :, None, None]
    kv_sel = sel_kv[b_i, g_i, sel]
    spos = sel[..., None] * SEL_BLOCK + jnp.arange(SEL_BLOCK)
    rel_s = qpos[None, None, :, None, None] - spos
    bias_s = jnp.moveaxis(t5g[t5_bucket(rel_s), jnp.arange(g_n)[None, :, None, None, None]], -1, 2)
    ls = jnp.einsum('btghd,bgtksd->bghtks', qg, kv_sel[..., 0, :], preferred_element_type=jnp.float32) * scale + bias_s
    m_tot = k_sel * SEL_BLOCK
    ps = masked_softmax(ls.reshape(b, g_n, hg, tb, m_tot), (rel_s >= 0).reshape(b, g_n, 1, tb, m_tot))
    o_s = jnp.einsum('bghtm,bgtmd->btghd', ps.astype(q.dtype), kv_sel[..., 1, :].reshape(b, g_n, tb, m_tot, HEAD_DIM))
    rel_w = qpos[:, None] - wpos[None, :]
    lw = jnp.einsum('btghd,bwgd->bghtw', qg, wkv[:, :, 0], preferred_element_type=jnp.float32) * scale
    lw = lw + jnp.transpose(t5g[t5_bucket(rel_w)], (2, 3, 0, 1))
    pw = masked_softmax(lw, (rel_w >= 0) & (rel_w < WINDOW))
    o_w = jnp.einsum('bghtw,bwgd->btghd', pw.astype(q.dtype), wkv[:, :, 1])
    g = jax.nn.sigmoid(gate.astype(jnp.float32)).reshape(b, tb, 3, g_n, hg, 1)
    o = g[:, :, 0] * o_c + g[:, :, 1] * o_s + g[:, :, 2] * o_w
    return o.astype(q.dtype).reshape(b, tb, NSA_HEADS, HEAD_DIM)


def prompt_mixers(hn, w_in, b_f, cmp_pe, cmp_w1, cmp_w2, t5_table):
    b, t = hn.shape[:2]
    q_a, kv_a, f_a, q_b, kv_b, gate_b, q_c, kv_c, g_m = split_projection(hn, w_in)
    pos = jnp.arange(t, dtype=jnp.int32)
    pos_blk = pos.reshape(-1, Q_BLOCK)
    logf = jax.nn.log_sigmoid(f_a.astype(jnp.float32) + b_f.astype(jnp.float32))
    cum = jnp.cumsum(logf, axis=1)
    k_a, v_a = kv_a[:, :, 0], kv_a[:, :, 1]
    o_a = from_blocks(lax.map(lambda a: fox_attend(a[0], a[1], a[2], k_a, v_a, pos, cum),
                              (to_blocks(q_a), pos_blk, to_blocks(cum))))
    comp, cend = nsa_compress(kv_b[:, :, 0], cmp_pe, cmp_w1, cmp_w2)
    sel_kv = sel_to_blocks(kv_b[:, :, 1])
    win = kv_b[:, :, 2]
    band = (np.arange(t // Q_BLOCK) * Q_BLOCK)[:, None] + np.arange(WINDOW + Q_BLOCK)[None, :]
    win_pad = jnp.pad(win, ((0, 0), (WINDOW, 0), (0, 0), (0, 0), (0, 0)))
    win_blk = jnp.moveaxis(win_pad[:, band], 1, 0)
    wpos_blk = jnp.asarray(band - WINDOW, dtype=jnp.int32)
    o_b = from_blocks(lax.map(lambda a: nsa_attend(a[0], a[1], a[2], a[3], a[4], comp, cend, sel_kv, t5_table),
                              (to_blocks(q_b), pos_blk, to_blocks(gate_b), win_blk, wpos_blk)))
    k_c, v_c = kv_c[:, :, 0], kv_c[:, :, 1]
    o_c = from_blocks(lax.map(lambda a: sb_attend(a[0], a[1], k_c, v_c, pos), (to_blocks(q_c), pos_blk)))
    states = (kv_a, logf, kv_b[:, :, :2], kv_c, win[:, t - min(WINDOW, t):])
    return o_a, o_b, o_c, g_m, states


def sample_mixers(hn, l, page_table, cache_fox_kv, cache_fox_logf, cache_nsa_kv, cache_sb_kv,
                  state_nsa_win_kv, w_in, b_f, cmp_pe, cmp_w1, cmp_w2, t5_table):
    b, t = hn.shape[:2]
    past = page_table.shape[1] * PAGE_SIZE

    def gather_past(cache):
        rows = cache[page_table, l]
        return rows.reshape(b, past, *rows.shape[3:])

    q_a, kv_a, f_a, q_b, kv_b, gate_b, q_c, kv_c, g_m = split_projection(hn, w_in)
    qpos = past + jnp.arange(t, dtype=jnp.int32)
    kpos = jnp.arange(past + t, dtype=jnp.int32)
    logf = jax.nn.log_sigmoid(f_a.astype(jnp.float32) + b_f.astype(jnp.float32))
    cum = jnp.cumsum(jnp.concatenate([gather_past(cache_fox_logf).astype(jnp.float32), logf], axis=1), axis=1)
    kv_a_all = jnp.concatenate([gather_past(cache_fox_kv).astype(kv_a.dtype), kv_a], axis=1)
    o_a = fox_attend(q_a, qpos, cum[:, past:], kv_a_all[:, :, 0], kv_a_all[:, :, 1], kpos, cum)
    nsa_all = jnp.concatenate([gather_past(cache_nsa_kv).astype(kv_b.dtype), kv_b[:, :, :2]], axis=1)
    comp, cend = nsa_compress(nsa_all[:, :, 0], cmp_pe, cmp_w1, cmp_w2)
    sel_kv = sel_to_blocks(nsa_all[:, :, 1])
    win_buf = state_nsa_win_kv[:, l].astype(kv_b.dtype)
    wb = win_buf.shape[1]
    wkv = jnp.concatenate([win_buf, kv_b[:, :, 2]], axis=1)
    wpos = past - wb + jnp.arange(wb + t, dtype=jnp.int32)
    o_b = nsa_attend(q_b, qpos, gate_b, wkv, wpos, comp, cend, sel_kv, t5_table)
    kv_c_all = jnp.concatenate([gather_past(cache_sb_kv).astype(kv_c.dtype), kv_c], axis=1)
    o_c = sb_attend(q_c, qpos, kv_c_all[:, :, 0], kv_c_all[:, :, 1], kpos)
    keep = min(WINDOW, wb + t)
    states = (kv_a, logf, kv_b[:, :, :2], kv_c, wkv[:, wb + t - keep:])
    return o_a, o_b, o_c, g_m, states


def merge_branches(o_a, o_b, o_c, g_m, w_out_a, w_out_b, w_out_c, w_out):
    b, t = g_m.shape[:2]
    g = jax.nn.sigmoid(g_m.astype(jnp.float32)).astype(o_a.dtype)
    mixed = (g[:, :, 0] * (o_a.reshape(b, t, -1) @ w_out_a)
             + g[:, :, 1] * (o_b.reshape(b, t, -1) @ w_out_b)
             + g[:, :, 2] * (o_c.reshape(b, t, -1) @ w_out_c))
    return mixed @ w_out


def hier_moe(x, wg_grp, bg_grp, wg_exp, bg_exp, w_gate, w_up, w_down):
    lead = x.shape[:-1]
    xf = x.reshape(-1, D_MODEL)
    n = xf.shape[0]
    grp_logits = (xf @ wg_grp).astype(jnp.float32) + bg_grp.astype(jnp.float32)
    p_grp = jax.nn.softmax(grp_logits, axis=-1)
    g_star = jnp.argmax(grp_logits, axis=-1)
    w_grp = jnp.take_along_axis(p_grp, g_star[:, None], axis=1)
    exp_logits = ((xf @ wg_exp).astype(jnp.float32) + bg_exp.astype(jnp.float32)).reshape(n, N_GROUPS, EXPERTS_PER_GROUP)
    in_grp = jnp.take_along_axis(exp_logits, g_star[:, None, None], axis=1)[:, 0]
    top_v, top_i = lax.top_k(in_grp, EXPERT_TOPK)
    w_sel = jax.nn.softmax(top_v, axis=-1) * w_grp
    eid = g_star[:, None] * EXPERTS_PER_GROUP + top_i
    combine = jnp.sum(jax.nn.one_hot(eid, N_EXPERTS, dtype=jnp.float32) * w_sel[..., None], axis=1)
    h = jax.nn.silu(jnp.einsum('nd,edf->nef', xf, w_gate)) * jnp.einsum('nd,edf->nef', xf, w_up)
    h = h * combine[..., None].astype(h.dtype)
    return jnp.einsum('nef,efd->nd', h, w_down).reshape(*lead, D_MODEL)


def setup_inputs(seed: int = 0) -> dict:
    key = jax.random.key(seed)
    ks = jax.random.split(key, 32)
    f32 = jnp.float32
    n_pages = PAST_LEN // PAGE_SIZE
    n_used = DEC_BATCH * n_pages
    n_pool = n_used + n_used // 4
    win_buf = min(WINDOW, PAST_LEN)

    def nrm(k, shape, scale=1.0):
        return scale * jax.random.normal(k, shape, f32)

    return {
        'x_prompt': nrm(ks[0], (BATCH, SEQ, D_MODEL)),
        'x_sample': nrm(ks[1], (DEC_BATCH, DEC_SEQ, D_MODEL)),
        'cache_fox_kv': nrm(ks[2], (n_pool, DEPTH, PAGE_SIZE, 2, FOX_HEADS, HEAD_DIM)),
        'cache_fox_logf': jax.nn.log_sigmoid(FORGET_BIAS_INIT + nrm(ks[3], (n_pool, DEPTH, PAGE_SIZE, FOX_HEADS))),
        'cache_nsa_kv': nrm(ks[4], (n_pool, DEPTH, PAGE_SIZE, 2, 2, NSA_KV_HEADS, HEAD_DIM)),
        'cache_sb_kv': nrm(ks[5], (n_pool, DEPTH, PAGE_SIZE, 2, SB_HEADS, HEAD_DIM)),
        'state_nsa_win_kv': nrm(ks[6], (DEC_BATCH, DEPTH, win_buf, 2, NSA_KV_HEADS, HEAD_DIM)),
        'page_table': jax.random.permutation(ks[7], n_pool)[:n_used].reshape(DEC_BATCH, n_pages).astype(jnp.int32),
        'norm_mix_g': 1.0 + nrm(ks[8], (DEPTH, D_MODEL), 0.01),
        'norm_ffn_g': 1.0 + nrm(ks[9], (DEPTH, D_MODEL), 0.01),
        'norm_final_g': 1.0 + nrm(ks[10], (D_MODEL,), 0.01),
        'w_in': nrm(ks[11], (DEPTH, D_MODEL, IN_W), D_MODEL ** -0.5),
        'b_forget': FORGET_BIAS_INIT + nrm(ks[12], (DEPTH, FOX_HEADS), 0.1),
        't5_table': nrm(ks[13], (N_BUCKETS, NSA_HEADS), 0.2),
        'cmp_pe': nrm(ks[14], (DEPTH, 2, CMP_BLOCK, HEAD_DIM), 0.1),
        'cmp_w1': nrm(ks[15], (DEPTH, 2, CMP_BLOCK, HEAD_DIM, CMP_HIDDEN), (CMP_BLOCK * HEAD_DIM) ** -0.5),
        'cmp_w2': nrm(ks[16], (DEPTH, 2, CMP_HIDDEN, HEAD_DIM), CMP_HIDDEN ** -0.5),
        'w_out_a': nrm(ks[17], (DEPTH, FOX_W, D_MODEL), FOX_W ** -0.5),
        'w_out_b': nrm(ks[18], (DEPTH, NSA_W, D_MODEL), NSA_W ** -0.5),
        'w_out_c': nrm(ks[19], (DEPTH, SB_W, D_MODEL), SB_W ** -0.5),
        'w_out': nrm(ks[20], (DEPTH, D_MODEL, D_MODEL), D_MODEL ** -0.5),
        'router_group_w': nrm(ks[21], (DEPTH, D_MODEL, N_GROUPS), D_MODEL ** -0.5),
        'router_group_b': nrm(ks[22], (DEPTH, N_GROUPS), 0.01),
        'router_expert_w': nrm(ks[23], (DEPTH, D_MODEL, N_EXPERTS), D_MODEL ** -0.5),
        'router_expert_b': nrm(ks[24], (DEPTH, N_EXPERTS), 0.01),
        'expert_w_gate': nrm(ks[25], (DEPTH, N_EXPERTS, D_MODEL, EXPERT_FF), D_MODEL ** -0.5),
        'expert_w_up': nrm(ks[26], (DEPTH, N_EXPERTS, D_MODEL, EXPERT_FF), D_MODEL ** -0.5),
        'expert_w_down': nrm(ks[27], (DEPTH, N_EXPERTS, EXPERT_FF, D_MODEL), EXPERT_FF ** -0.5),
    }


def reference(x_prompt, x_sample, cache_fox_kv, cache_fox_logf, cache_nsa_kv, cache_sb_kv, state_nsa_win_kv,
              page_table, norm_mix_g, norm_ffn_g, norm_final_g, w_in, b_forget, t5_table, cmp_pe, cmp_w1, cmp_w2,
              w_out_a, w_out_b, w_out_c, w_out, router_group_w, router_group_b, router_expert_w, router_expert_b,
              expert_w_gate, expert_w_up, expert_w_down):
    h_p, h_s = x_prompt, x_sample
    st_p = ([], [], [], [], [])
    st_s = ([], [], [], [], [])
    for l in range(DEPTH):
        moe_w = (router_group_w[l], router_group_b[l], router_expert_w[l], router_expert_b[l],
                 expert_w_gate[l], expert_w_up[l], expert_w_down[l])
        o_a, o_b, o_c, g_m, st = prompt_mixers(rmsnorm(h_p, norm_mix_g[l]), w_in[l], b_forget[l],
                                               cmp_pe[l], cmp_w1[l], cmp_w2[l], t5_table)
        h_p = h_p + merge_branches(o_a, o_b, o_c, g_m, w_out_a[l], w_out_b[l], w_out_c[l], w_out[l])
        h_p = h_p + hier_moe(rmsnorm(h_p, norm_ffn_g[l]), *moe_w)
        for acc, s in zip(st_p, st):
            acc.append(s)
        o_a, o_b, o_c, g_m, st = sample_mixers(rmsnorm(h_s, norm_mix_g[l]), l, page_table, cache_fox_kv,
                                               cache_fox_logf, cache_nsa_kv, cache_sb_kv, state_nsa_win_kv,
                                               w_in[l], b_forget[l], cmp_pe[l], cmp_w1[l], cmp_w2[l], t5_table)
        h_s = h_s + merge_branches(o_a, o_b, o_c, g_m, w_out_a[l], w_out_b[l], w_out_c[l], w_out[l])
        h_s = h_s + hier_moe(rmsnorm(h_s, norm_ffn_g[l]), *moe_w)
        for acc, s in zip(st_s, st):
            acc.append(s)
    y_prompt = rmsnorm(h_p, norm_final_g)
    y_sample = rmsnorm(h_s, norm_final_g)
    fox_kv_p = jnp.stack(st_p[0], axis=1)
    fox_logf_p = jnp.stack(st_p[1], axis=1)
    nsa_kv_p = jnp.stack(st_p[2], axis=1)
    sb_kv_p = jnp.stack(st_p[3], axis=1)
    win_kv_p = jnp.stack(st_p[4], axis=1)
    fox_kv_s = jnp.stack(st_s[0], axis=1)
    fox_logf_s = jnp.stack(st_s[1], axis=1)
    nsa_kv_s = jnp.stack(st_s[2], axis=1)
    sb_kv_s = jnp.stack(st_s[3], axis=1)
    win_kv_s = jnp.stack(st_s[4], axis=1)
    return (y_prompt, y_sample, fox_kv_p, fox_kv_s, fox_logf_p, fox_logf_s, nsa_kv_p, nsa_kv_s,
            sb_kv_p, sb_kv_s, win_kv_p, win_kv_s)
```

```python
import functools
import math

import numpy as np
import jax
import jax.numpy as jnp
from jax import lax
from jax.experimental import pallas as pl
from jax.experimental.pallas import tpu as pltpu

F32 = jnp.float32
BF16 = jnp.bfloat16

HEAD_DIM = 64
N_HEADS = 8
NSA_G = 2
NSA_HG = N_HEADS // NSA_G
HW = N_HEADS * HEAD_DIM
PAGE = 128
DEC_T = 8
CMP_BLOCK = 32
CMP_STRIDE = 16
CMP_HIDDEN = 128
SEL_BLOCK = 64
SEL_TOPK = 8
WINDOW = 512
N_BUCKETS = 32
MAX_DISTANCE = 128
N_GROUPS = 4
EXPERTS_PER_GROUP = 4
N_EXPERTS = N_GROUPS * EXPERTS_PER_GROUP
RMS_EPS = 1e-6
NEG_INF = -1e30
FORCE_SCORE = 1e9
BELOW_ALL = -3e38
SCALE = HEAD_DIM ** -0.5
TINY = float(np.finfo(np.float32).tiny)
QB = 128
LANES = 128
VMEM_LIMIT = 56 * 1024 * 1024

C_QA, C_KA, C_VA, C_QB, C_QC, C_KC, C_VC = 0, 512, 1024, 1536, 2048, 2560, 3072
C_CMP, C_SEL, C_WIN, C_MISC, C_GM = 3584, 3840, 4096, 4352, 4608
MISC_W = 256


def _params(*sem):
    return pltpu.CompilerParams(dimension_semantics=sem, vmem_limit_bytes=VMEM_LIMIT)


def _pick(n, cap, mult):
    t = (min(cap, n) // mult) * mult
    while t > 0 and n % t:
        t -= mult
    assert t > 0, (n, cap, mult)
    return t


def _iota(shape, axis):
    return lax.broadcasted_iota(jnp.int32, shape, axis)


def _dot(a, b):
    return jnp.dot(a, b, preferred_element_type=F32)


def _dot_nt(a, b):
    return lax.dot_general(a, b, (((1,), (1,)), ((), ())), preferred_element_type=F32)


def _softplus(x):
    return jnp.maximum(x, 0.0) + jnp.log1p(jnp.exp(-jnp.abs(x)))


def _log_sigmoid(x):
    return jnp.minimum(x, 0.0) - jnp.log1p(jnp.exp(-jnp.abs(x)))


def _sigmoid(x):
    return 1.0 / (1.0 + jnp.exp(-x))


def _split_dot(x, w):
    hi = x.astype(BF16)
    lo = (x - hi.astype(F32)).astype(BF16)
    return _dot(hi, w) + _dot(lo, w)


def _tri(w):
    return jnp.where(_iota((w, w), 0) > _iota((w, w), 1), 1.0, 0.0).astype(BF16)


def _bucket(rel):
    n = jnp.maximum(rel, 0)
    exact = N_BUCKETS // 2
    far = jnp.log(jnp.maximum(n, exact).astype(F32) / exact) / math.log(MAX_DISTANCE / exact)
    far = exact + (far * (N_BUCKETS - exact)).astype(jnp.int32)
    return jnp.where(n < exact, n, jnp.minimum(far, N_BUCKETS - 1))


def _masked_softmax(logits, mask):
    l = jnp.where(mask, logits, NEG_INF)
    e = jnp.where(mask, jnp.exp(l - jnp.max(l, axis=-1, keepdims=True)), 0.0)
    return e / jnp.maximum(jnp.sum(e, axis=-1, keepdims=True), TINY)


def _overlap(nch):
    cs = _iota((nch, LANES), 0) * CMP_STRIDE
    ss = _iota((nch, LANES), 1) * SEL_BLOCK
    ov = jnp.maximum(jnp.minimum(cs + CMP_BLOCK, ss + SEL_BLOCK) - jnp.maximum(cs, ss), 0)
    return (ov.astype(F32) / CMP_BLOCK).astype(BF16)


def _select_blocks(score, qpos, n_s):
    lane = _iota(score.shape, 1)
    lane_f = lane.astype(F32)
    valid = lane * SEL_BLOCK <= qpos
    forced = (lane == jnp.right_shift(qpos, 6)) | (lane == 0)
    sc = jnp.where(forced, FORCE_SCORE, jnp.where(valid, score, NEG_INF))
    sc = jnp.where(lane < n_s, sc, BELOW_ALL)
    sel = jnp.zeros(score.shape, F32)
    for _ in range(min(SEL_TOPK, n_s)):
        mx = jnp.max(sc, axis=-1, keepdims=True)
        idx = jnp.min(jnp.where(sc == mx, lane_f, 1e9), axis=-1, keepdims=True)
        pick = lane_f == idx
        sel = jnp.where(pick, 1.0, sel)
        sc = jnp.where(pick, -3.3e38, sc)
    return sel


def _proj_kernel(x_ref, g_ref, w_ref, of_ref, ob_ref, xn_ref):
    @pl.when(pl.program_id(1) == 0)
    def _():
        x = x_ref[...]
        ms = jnp.mean(x * x, axis=-1, keepdims=True)
        xn_ref[...] = (x * lax.rsqrt(ms + RMS_EPS) * g_ref[...]).astype(BF16)

    y = _dot(xn_ref[...], w_ref[...])
    of_ref[...] = y
    ob_ref[...] = y.astype(BF16)


def _proj(h, g, w):
    n, d = h.shape
    wp = w.shape[1]
    tm = _pick(n, 512, 16)
    tn = _pick(wp, 1536, 128)
    return pl.pallas_call(
        _proj_kernel,
        grid=(n // tm, wp // tn),
        in_specs=[pl.BlockSpec((tm, d), lambda i, j: (i, 0)),
                  pl.BlockSpec((1, d), lambda i, j: (0, 0)),
                  pl.BlockSpec((d, tn), lambda i, j: (0, j))],
        out_specs=[pl.BlockSpec((tm, tn), lambda i, j: (i, j)),
                   pl.BlockSpec((tm, tn), lambda i, j: (i, j))],
        out_shape=[jax.ShapeDtypeStruct((n, wp), F32), jax.ShapeDtypeStruct((n, wp), BF16)],
        scratch_shapes=[pltpu.VMEM((tm, d), BF16)],
        compiler_params=_params("arbitrary", "arbitrary"),
        name="proj",
    )(h, g.reshape(1, d), w)


def _merge_kernel(oa_ref, ob_ref, oc_ref, wa_ref, wb_ref, wc_ref, g0_ref, g1_ref, g2_ref, o_ref):
    m = _sigmoid(g0_ref[...]) * _dot(oa_ref[...], wa_ref[...])
    m = m + _sigmoid(g1_ref[...]) * _dot(ob_ref[...], wb_ref[...])
    m = m + _sigmoid(g2_ref[...]) * _dot(oc_ref[...], wc_ref[...])
    o_ref[...] = m.astype(BF16)


def _merge(o_a, o_b, o_c, pf, wa, wb, wc, d):
    n = o_a.shape[0]
    tm = _pick(n, 512, 16)
    tn = _pick(d, 512, 128)
    gm0 = C_GM // tn
    o_spec = pl.BlockSpec((tm, HW), lambda i, j: (i, 0))
    w_spec = pl.BlockSpec((HW, tn), lambda i, j: (0, j))

    def g_spec(k):
        return pl.BlockSpec((tm, tn), lambda i, j: (i, gm0 + k * (d // tn) + j))

    return pl.pallas_call(
        _merge_kernel,
        grid=(n // tm, d // tn),
        in_specs=[o_spec, o_spec, o_spec, w_spec, w_spec, w_spec, g_spec(0), g_spec(1), g_spec(2)],
        out_specs=pl.BlockSpec((tm, tn), lambda i, j: (i, j)),
        out_shape=jax.ShapeDtypeStruct((n, d), BF16),
        compiler_params=_params("arbitrary", "arbitrary"),
        name="merge",
    )(o_a, o_b, o_c, wa, wb, wc, pf, pf, pf)


def _mm_res_kernel(x_ref, w_ref, r_ref, o_ref):
    o_ref[...] = r_ref[...] + _dot(x_ref[...], w_ref[...])


def _mm_res(x, w, res):
    n, k = x.shape
    d = w.shape[1]
    tm = _pick(n, 512, 16)
    tn = _pick(d, 512, 128)
    return pl.pallas_call(
        _mm_res_kernel,
        grid=(n // tm, d // tn),
        in_specs=[pl.BlockSpec((tm, k), lambda i, j: (i, 0)),
                  pl.BlockSpec((k, tn), lambda i, j: (0, j)),
                  pl.BlockSpec((tm, tn), lambda i, j: (i, j))],
        out_specs=pl.BlockSpec((tm, tn), lambda i, j: (i, j)),
        out_shape=jax.ShapeDtypeStruct((n, d), F32),
        compiler_params=_params("arbitrary", "arbitrary"),
        name="out_proj",
    )(x, w, res)


def _router_kernel(h_ref, g_ref, w_ref, b_ref, xn_ref, comb_ref):
    x = h_ref[...]
    ms = jnp.mean(x * x, axis=-1, keepdims=True)
    xn = x * lax.rsqrt(ms + RMS_EPS) * g_ref[...]
    xn_ref[...] = xn.astype(BF16)
    logits = jnp.dot(xn, w_ref[...], precision=lax.Precision.HIGHEST, preferred_element_type=F32) + b_ref[...]
    lane = _iota(logits.shape, 1)
    lane_f = lane.astype(F32)
    is_grp = lane < N_GROUPS
    gl = jnp.where(is_grp, logits, BELOW_ALL)
    gmax = jnp.max(gl, axis=-1, keepdims=True)
    gsum = jnp.sum(jnp.where(is_grp, jnp.exp(gl - gmax), 0.0), axis=-1, keepdims=True)
    w_grp = 1.0 / gsum
    g_star = jnp.min(jnp.where(is_grp & (gl == gmax), lane_f, 1e9), axis=-1, keepdims=True)
    lo = N_GROUPS + g_star * EXPERTS_PER_GROUP
    in_grp = (lane_f >= lo) & (lane_f < lo + EXPERTS_PER_GROUP)
    el = jnp.where(in_grp, logits, BELOW_ALL)
    v1 = jnp.max(el, axis=-1, keepdims=True)
    i1 = jnp.min(jnp.where(el == v1, lane_f, 1e9), axis=-1, keepdims=True)
    el2 = jnp.where(lane_f == i1, BELOW_ALL, el)
    v2 = jnp.max(el2, axis=-1, keepdims=True)
    i2 = jnp.min(jnp.where(el2 == v2, lane_f, 1e9), axis=-1, keepdims=True)
    e2 = jnp.exp(v2 - v1)
    den = 1.0 + e2
    comb = jnp.where(lane_f == i1, w_grp / den, 0.0) + jnp.where(lane_f == i2, w_grp * e2 / den, 0.0)
    comb_ref[...] = comb


def _router(h, g, w, b):
    n, d = h.shape
    tm = _pick(n, 512, 16)
    return pl.pallas_call(
        _router_kernel,
        grid=(n // tm,),
        in_specs=[pl.BlockSpec((tm, d), lambda i: (i, 0)),
                  pl.BlockSpec((1, d), lambda i: (0, 0)),
                  pl.BlockSpec((d, LANES), lambda i: (0, 0)),
                  pl.BlockSpec((1, LANES), lambda i: (0, 0))],
        out_specs=[pl.BlockSpec((tm, d), lambda i: (i, 0)),
                   pl.BlockSpec((tm, LANES), lambda i: (i, 0))],
        out_shape=[jax.ShapeDtypeStruct((n, d), BF16), jax.ShapeDtypeStruct((n, LANES), F32)],
        compiler_params=_params("arbitrary"),
        name="router",
    )(h, g.reshape(1, d), w, b)


def _moe_kernel(x_ref, c_ref, h_ref, wg_ref, wu_ref, wd_ref, o_ref, acc_ref):
    e = pl.program_id(1)

    @pl.when(e == 0)
    def _():
        acc_ref[...] = h_ref[...]

    x = x_ref[...]
    comb = c_ref[...]
    ce = jnp.sum(jnp.where(_iota(comb.shape, 1) == e + N_GROUPS, comb, 0.0), axis=-1, keepdims=True)
    gate = _dot(x, wg_ref[0])
    up = _dot(x, wu_ref[0])
    hh = gate * _sigmoid(gate) * up * ce
    acc_ref[...] += _dot(hh.astype(BF16), wd_ref[0])

    @pl.when(e == pl.num_programs(1) - 1)
    def _():
        o_ref[...] = acc_ref[...]


def _moe(xn, comb, h, wg, wu, wd):
    n, d = h.shape
    ne, _, ff = wg.shape
    tm = _pick(n, 512, 16)
    return pl.pallas_call(
        _moe_kernel,
        grid=(n // tm, ne),
        in_specs=[pl.BlockSpec((tm, d), lambda i, e: (i, 0)),
                  pl.BlockSpec((tm, LANES), lambda i, e: (i, 0)),
                  pl.BlockSpec((tm, d), lambda i, e: (i, 0)),
                  pl.BlockSpec((1, d, ff), lambda i, e: (e, 0, 0)),
                  pl.BlockSpec((1, d, ff), lambda i, e: (e, 0, 0)),
                  pl.BlockSpec((1, ff, d), lambda i, e: (e, 0, 0))],
        out_specs=pl.BlockSpec((tm, d), lambda i, e: (i, 0)),
        out_shape=jax.ShapeDtypeStruct((n, d), F32),
        scratch_shapes=[pltpu.VMEM((tm, d), F32)],
        compiler_params=_params("arbitrary", "arbitrary"),
        name="moe",
    )(xn, comb, h, wg, wu, wd)


def _final_norm_kernel(h_ref, g_ref, o_ref):
    x = h_ref[...]
    ms = jnp.mean(x * x, axis=-1, keepdims=True)
    o_ref[...] = x * lax.rsqrt(ms + RMS_EPS) * g_ref[...]


def _final_norm(h, g):
    n, d = h.shape
    tm = _pick(n, 512, 8)
    return pl.pallas_call(
        _final_norm_kernel,
        grid=(n // tm,),
        in_specs=[pl.BlockSpec((tm, d), lambda i: (i, 0)), pl.BlockSpec((1, d), lambda i: (0, 0))],
        out_specs=pl.BlockSpec((tm, d), lambda i: (i, 0)),
        out_shape=jax.ShapeDtypeStruct((n, d), F32),
        compiler_params=_params("arbitrary"),
        name="final_norm",
    )(h, g.reshape(1, d))


def _cum_prompt_kernel(m_ref, bf_ref, logf_ref, cum_ref):
    logf = _log_sigmoid(m_ref[:, 0:LANES] + bf_ref[...])
    logf_ref[...] = logf
    t = logf.shape[0]
    row = _iota(logf.shape, 0)
    c = logf
    s = 1
    while s < t:
        c = c + jnp.where(row >= s, pltpu.roll(c, s, axis=0), 0.0)
        s *= 2
    cum_ref[...] = c


def _cum_prompt(pf, bf, nb, t):
    return pl.pallas_call(
        _cum_prompt_kernel,
        grid=(nb,),
        in_specs=[pl.BlockSpec((t, MISC_W), lambda b: (b, C_MISC // MISC_W)),
                  pl.BlockSpec((1, LANES), lambda b: (0, 0))],
        out_specs=[pl.BlockSpec((t, LANES), lambda b: (b, 0)), pl.BlockSpec((t, LANES), lambda b: (b, 0))],
        out_shape=[jax.ShapeDtypeStruct((nb * t, LANES), F32), jax.ShapeDtypeStruct((nb * t, LANES), F32)],
        compiler_params=_params("arbitrary"),
        name="cum_prompt",
    )(pf, bf)


def _fox_prompt_kernel(q_ref, k_ref, v_ref, cq_ref, ck_ref, o_ref):
    i = pl.program_id(1)
    q0 = i * QB
    qpos = q0 + _iota((QB, 1), 0)
    lane = _iota((QB, LANES), 1)
    kl = _iota((1, QB), 1)
    nkb = i + 1
    for p in range(N_HEADS // 2):
        cols = slice(p * LANES, (p + 1) * LANES)
        qp = q_ref[:, cols] * jnp.asarray(SCALE, BF16)
        accs = []
        for pos in range(2):
            h = 2 * p + pos
            qm = jnp.where((lane >= HEAD_DIM) == (pos == 1), qp, jnp.zeros_like(qp))
            cqh = cq_ref[:, h:h + 1]

            def body(kb, carry, qm=qm, cqh=cqh, h=h, cols=cols):
                m, l, acc = carry
                k0 = pl.multiple_of(kb * QB, QB)
                s = _dot_nt(qm, k_ref[pl.ds(k0, QB), cols])
                s = s + cqh - ck_ref[0, h, pl.ds(kb, 1), :]
                mask = (k0 + kl) <= qpos
                s = jnp.where(mask, s, NEG_INF)
                m_new = jnp.maximum(m, jnp.max(s, axis=-1, keepdims=True))
                pr = jnp.where(mask, jnp.exp(s - m_new), 0.0)
                alpha = jnp.exp(m - m_new)
                l = alpha * l + jnp.sum(pr, axis=-1, keepdims=True)
                acc = alpha * acc + _dot(pr.astype(BF16), v_ref[pl.ds(k0, QB), cols])
                return m_new, l, acc

            init = (jnp.full((QB, 1), NEG_INF, F32), jnp.zeros((QB, 1), F32), jnp.zeros((QB, LANES), F32))
            m, l, acc = lax.fori_loop(0, nkb, body, init)
            accs.append(acc / jnp.maximum(l, TINY))
        o_ref[:, cols] = jnp.where(lane < HEAD_DIM, accs[0], accs[1]).astype(BF16)


def _fox_prompt(pb, cum, cum_t, nb, t):
    nq = t // QB
    return pl.pallas_call(
        _fox_prompt_kernel,
        grid=(nb, nq),
        in_specs=[pl.BlockSpec((QB, HW), lambda b, i: (b * nq + i, C_QA // HW)),
                  pl.BlockSpec((t, HW), lambda b, i: (b, C_KA // HW)),
                  pl.BlockSpec((t, HW), lambda b, i: (b, C_VA // HW)),
                  pl.BlockSpec((QB, LANES), lambda b, i: (b * nq + i, 0)),
                  pl.BlockSpec((1, N_HEADS, nq, QB), lambda b, i: (b, 0, 0, 0))],
        out_specs=pl.BlockSpec((QB, HW), lambda b, i: (b * nq + i, 0)),
        out_shape=jax.ShapeDtypeStruct((nb * t, HW), BF16),
        compiler_params=_params("arbitrary", "arbitrary"),
        name="fox_prompt",
    )(pb, pb, pb, cum, cum_t)


def _sb_prompt_kernel(q_ref, k_ref, v_ref, o_ref):
    i = pl.program_id(1)
    q0 = i * QB
    qpos = q0 + _iota((QB, 1), 0)
    lane = _iota((QB, LANES), 1)
    kl = _iota((1, QB), 1)
    tri = _tri(QB)
    nkb = i + 1
    for p in range(N_HEADS // 2):
        cols = slice(p * LANES, (p + 1) * LANES)
        qp = q_ref[:, cols] * jnp.asarray(SCALE, BF16)
        accs = []
        for pos in range(2):
            qm = jnp.where((lane >= HEAD_DIM) == (pos == 1), qp, jnp.zeros_like(qp))

            def body(n, carry, qm=qm, cols=cols):
                r, acc = carry
                kb = nkb - 1 - n
                k0 = pl.multiple_of(kb * QB, QB)
                z = _dot_nt(qm, k_ref[pl.ds(k0, QB), cols])
                mask = (k0 + kl) < qpos
                sp = _softplus(z)
                lk = jnp.where(mask, -sp, 0.0)
                between = _split_dot(lk, tri) + r
                a = jnp.where(mask, jnp.exp(z - sp + between), 0.0)
                acc = acc + _dot(a.astype(BF16), v_ref[pl.ds(k0, QB), cols])
                r = r + jnp.sum(lk, axis=-1, keepdims=True)
                return r, acc

            init = (jnp.zeros((QB, 1), F32), jnp.zeros((QB, LANES), F32))
            _, acc = lax.fori_loop(0, nkb, body, init)
            accs.append(acc)
        o_ref[:, cols] = jnp.where(lane < HEAD_DIM, accs[0], accs[1]).astype(BF16)


def _sb_prompt(pb, nb, t):
    nq = t // QB
    return pl.pallas_call(
        _sb_prompt_kernel,
        grid=(nb, nq),
        in_specs=[pl.BlockSpec((QB, HW), lambda b, i: (b * nq + i, C_QC // HW)),
                  pl.BlockSpec((t, HW), lambda b, i: (b, C_KC // HW)),
                  pl.BlockSpec((t, HW), lambda b, i: (b, C_VC // HW))],
        out_specs=pl.BlockSpec((QB, HW), lambda b, i: (b * nq + i, 0)),
        out_shape=jax.ShapeDtypeStruct((nb * t, HW), BF16),
        compiler_params=_params("arbitrary", "arbitrary"),
        name="sb_prompt",
    )(pb, pb, pb)


def _compress_kernel(x_ref, w1_ref, pe_ref, w2_ref, o_ref):
    w1 = w1_ref[0]
    pe = _dot(pe_ref[0].astype(BF16), w1)
    pe_term = pe[0:1, 0:CMP_HIDDEN] + pe[1:2, CMP_HIDDEN:2 * CMP_HIDDEN]
    nch = x_ref.shape[3]
    out = None
    for g in range(NSA_G):
        y = _dot(x_ref[0, 0, g], w1)
        pre = y[:, 0:CMP_HIDDEN] + pltpu.roll(y[:, CMP_HIDDEN:2 * CMP_HIDDEN], nch - 1, axis=0) + pe_term
        hid = 0.5 * pre * (1.0 + jnp.tanh(math.sqrt(2.0 / math.pi) * (pre + 0.044715 * pre * pre * pre)))
        cg = _dot(hid.astype(BF16), w2_ref[0, g])
        out = cg if out is None else out + cg
    o_ref[0, 0] = out.astype(BF16)


def _compress(chunks, w1ab, pe2, w2p):
    _, nb, _, nch, cw = chunks.shape
    return pl.pallas_call(
        _compress_kernel,
        grid=(2, nb),
        in_specs=[pl.BlockSpec((1, 1, NSA_G, nch, cw), lambda c, b: (c, b, 0, 0, 0)),
                  pl.BlockSpec((1, cw, 2 * CMP_HIDDEN), lambda c, b: (c, 0, 0)),
                  pl.BlockSpec((1, 8, cw), lambda c, b: (c, 0, 0)),
                  pl.BlockSpec((1, NSA_G, CMP_HIDDEN, LANES), lambda c, b: (c, 0, 0, 0))],
        out_specs=pl.BlockSpec((1, 1, nch, LANES), lambda c, b: (c, b, 0, 0)),
        out_shape=jax.ShapeDtypeStruct((2, nb, nch, LANES), BF16),
        compiler_params=_params("arbitrary", "arbitrary"),
        name="nsa_compress",
    )(chunks, w1ab, pe2, w2p)


def _nsa_prompt_kernel(t5_ref, q_ref, ck_ref, cv_ref, sel_ref, win_ref, misc_ref, o_ref, *, n_s):
    i = pl.program_id(1)
    q0 = i * QB
    qpos = q0 + _iota((QB, 1), 0)
    lane = _iota((QB, LANES), 1)
    kl = _iota((1, QB), 1)
    nch = ck_ref.shape[2]
    rows4 = NSA_HG * QB

    def lut(bk, head):
        out = jnp.zeros(bk.shape, F32)
        for k in range(N_BUCKETS):
            out = jnp.where(bk == k, t5_ref[k, head], out)
        return out

    def stack4(f):
        return jnp.concatenate([f(hg) for hg in range(NSA_HG)], axis=0)

    def tile4(x):
        return jnp.concatenate([x] * NSA_HG, axis=0)

    rel_d = _iota((QB, QB), 0) - _iota((QB, QB), 1)
    bk_d = _bucket(rel_d)
    bk_p = _bucket(rel_d + QB)
    cend = _iota((QB, nch), 1) * CMP_STRIDE + (CMP_BLOCK - 1)
    rel_c = qpos - cend
    bk_c = _bucket(rel_c)
    mask_c = tile4(jnp.where((rel_c >= 0) & (_iota((QB, nch), 1) < nch - 1), 1.0, 0.0)) > 0.5
    ov = _overlap(nch)
    sig = _sigmoid(misc_ref[:, 0:LANES])
    pieces = [None] * N_HEADS

    for g in range(NSA_G):
        def q_head(hg, g=g):
            head = g * NSA_HG + hg
            x = q_ref[:, (head // 2) * LANES:(head // 2 + 1) * LANES].astype(F32) * SCALE
            if head % 2 != g:
                x = pltpu.roll(x, HEAD_DIM, axis=1)
            return jnp.where((lane >= HEAD_DIM) == (g == 1), x, 0.0).astype(BF16)

        qg = stack4(q_head)
        bias_d = stack4(lambda hg, g=g: lut(bk_d, g * NSA_HG + hg))
        bias_p = stack4(lambda hg, g=g: lut(bk_p, g * NSA_HG + hg))
        bias_far = stack4(lambda hg, g=g: jnp.full((QB, 1), t5_ref[N_BUCKETS - 1, g * NSA_HG + hg], F32))

        lc = _dot_nt(qg, ck_ref[0, 0]) + stack4(lambda hg, g=g: lut(bk_c, g * NSA_HG + hg))
        pc = _masked_softmax(lc, mask_c)
        o_cmp = _dot(pc.astype(BF16), cv_ref[0, 0])
        psum = pc[0:QB]
        for hg in range(1, NSA_HG):
            psum = psum + pc[hg * QB:(hg + 1) * QB]
        selm = _select_blocks(_split_dot(psum, ov), qpos, n_s).astype(BF16)

        def attend(kv_ref, kb, mask, bias, carry, real=None):
            m, l, acc = carry
            k0 = pl.multiple_of(kb * QB, QB)
            kk = kv_ref[pl.ds(k0, QB), 0:LANES]
            vv = kv_ref[pl.ds(k0, QB), LANES:2 * LANES]
            if real is not None:
                kk = jnp.where(real, kk, jnp.zeros_like(kk))
                vv = jnp.where(real, vv, jnp.zeros_like(vv))
            s = _dot_nt(qg, kk) + bias
            mask4 = tile4(jnp.where(mask, 1.0, 0.0)) > 0.5
            s = jnp.where(mask4, s, NEG_INF)
            m_new = jnp.maximum(m, jnp.max(s, axis=-1, keepdims=True))
            pr = jnp.where(mask4, jnp.exp(s - m_new), 0.0)
            alpha = jnp.exp(m - m_new)
            l = alpha * l + jnp.sum(pr, axis=-1, keepdims=True)
            acc = alpha * acc + _dot(pr.astype(BF16), vv)
            return m_new, l, acc

        def sel_mask(kb, valid, selm=selm):
            expand = jnp.where(_iota((LANES, QB), 0) == 2 * kb + jnp.right_shift(_iota((LANES, QB), 1), 6),
                               1.0, 0.0).astype(BF16)
            picked = _dot(selm, expand) > 0.5
            return picked & ((kb * QB + kl) <= qpos) & valid

        def init():
            return (jnp.full((rows4, 1), NEG_INF, F32), jnp.zeros((rows4, 1), F32), jnp.zeros((rows4, LANES), F32))

        carry = attend(sel_ref, i, sel_mask(i, True), bias_d, init())
        kbp = jnp.maximum(i - 1, 0)
        carry = attend(sel_ref, kbp, sel_mask(kbp, i >= 1), bias_p, carry)
        carry = lax.fori_loop(0, jnp.maximum(i - 1, 0),
                              lambda kb, c: attend(sel_ref, kb, sel_mask(kb, True), bias_far, c), carry)
        o_sel = carry[2] / jnp.maximum(carry[1], TINY)

        carry = init()
        for dlt in range(WINDOW // QB + 1):
            rel = qpos - ((i - dlt) * QB + kl)
            mask = (rel >= 0) & (rel < WINDOW)
            bias = bias_d if dlt == 0 else (bias_p if dlt == 1 else bias_far)
            carry = attend(win_ref, jnp.maximum(i - dlt, 0), mask, bias, carry, real=None if dlt == 0 else i - dlt >= 0)
        o_win = carry[2] / jnp.maximum(carry[1], TINY)

        def gate(br, g=g):
            return stack4(lambda hg: sig[:, 8 + br * N_HEADS + g * NSA_HG + hg:9 + br * N_HEADS + g * NSA_HG + hg])

        o_g = gate(0) * o_cmp + gate(1) * o_sel + gate(2) * o_win
        for hg in range(NSA_HG):
            head = g * NSA_HG + hg
            x = o_g[hg * QB:(hg + 1) * QB]
            if head % 2 != g:
                x = pltpu.roll(x, HEAD_DIM, axis=1)
            pieces[head] = x

    for p in range(N_HEADS // 2):
        o_ref[:, p * LANES:(p + 1) * LANES] = jnp.where(lane < HEAD_DIM, pieces[2 * p], pieces[2 * p + 1]).astype(BF16)


def _nsa_prompt(pb, pf, comp, t5, nb, t):
    nq = t // QB
    nch = comp.shape[2]
    kernel = functools.partial(_nsa_prompt_kernel, n_s=-(-t // SEL_BLOCK))
    return pl.pallas_call(
        kernel,
        grid=(nb, nq),
        in_specs=[pl.BlockSpec(memory_space=pltpu.SMEM),
                  pl.BlockSpec((QB, HW), lambda b, i: (b * nq + i, C_QB // HW)),
                  pl.BlockSpec((1, 1, nch, LANES), lambda b, i: (0, b, 0, 0)),
                  pl.BlockSpec((1, 1, nch, LANES), lambda b, i: (1, b, 0, 0)),
                  pl.BlockSpec((t, 2 * LANES), lambda b, i: (b, C_SEL // (2 * LANES))),
                  pl.BlockSpec((t, 2 * LANES), lambda b, i: (b, C_WIN // (2 * LANES))),
                  pl.BlockSpec((QB, MISC_W), lambda b, i: (b * nq + i, C_MISC // MISC_W))],
        out_specs=pl.BlockSpec((QB, HW), lambda b, i: (b * nq + i, 0)),
        out_shape=jax.ShapeDtypeStruct((nb * t, HW), BF16),
        compiler_params=_params("arbitrary", "arbitrary"),
        name="nsa_prompt",
    )(t5, pb, comp, comp, pb, pb, pf)


def _page_specs(npg, block, layer, tail):
    def spec(j):
        return pl.BlockSpec(block, lambda b, pt: (pt[b * npg + j], layer) + tail)
    return [spec(j) for j in range(npg)]


def _cum_sample_kernel(pt_ref, *refs, npg):
    pages = refs[:npg]
    m_ref, bf_ref, cp_ref, logf_ref, cn_ref = refs[npg:]
    x = jnp.concatenate([pg[0, 0] for pg in pages], axis=0)
    lane = _iota(x.shape, 1)
    s = 1
    while s < PAGE:
        x = x + jnp.where(lane >= s, pltpu.roll(x, s, axis=1), 0.0)
        s *= 2
    off = jnp.zeros((N_HEADS, 1), F32)
    for j in range(npg):
        blk = x[j * N_HEADS:(j + 1) * N_HEADS]
        cp_ref[0, j] = blk + off
        off = off + blk[:, PAGE - 1:PAGE]
    eye = _iota((N_HEADS, LANES), 0) == _iota((N_HEADS, LANES), 1)
    tot = jnp.sum(jnp.where(eye, off, 0.0), axis=0, keepdims=True)
    logf = _log_sigmoid(m_ref[:, 0:LANES] + bf_ref[...])
    logf_ref[...] = logf
    row = _iota(logf.shape, 0)
    c = logf
    s = 1
    while s < DEC_T:
        c = c + jnp.where(row >= s, pltpu.roll(c, s, axis=0), 0.0)
        s *= 2
    cn_ref[...] = c + tot


def _cum_sample(pt, logf_view, pf, bf, layer, nbs, npg, row0):
    kernel = functools.partial(_cum_sample_kernel, npg=npg)
    grid_spec = pltpu.PrefetchScalarGridSpec(
        num_scalar_prefetch=1,
        grid=(nbs,),
        in_specs=_page_specs(npg, (1, 1, N_HEADS, PAGE), layer, (0, 0)) + [
            pl.BlockSpec((DEC_T, MISC_W), lambda b, pt: (row0 // DEC_T + b, C_MISC // MISC_W)),
            pl.BlockSpec((1, LANES), lambda b, pt: (0, 0))],
        out_specs=[pl.BlockSpec((1, npg, N_HEADS, PAGE), lambda b, pt: (b, 0, 0, 0)),
                   pl.BlockSpec((DEC_T, LANES), lambda b, pt: (b, 0)),
                   pl.BlockSpec((DEC_T, LANES), lambda b, pt: (b, 0))],
    )
    return pl.pallas_call(
        kernel,
        grid_spec=grid_spec,
        out_shape=[jax.ShapeDtypeStruct((nbs, npg, N_HEADS, PAGE), F32),
                   jax.ShapeDtypeStruct((nbs * DEC_T, LANES), F32),
                   jax.ShapeDtypeStruct((nbs * DEC_T, LANES), F32)],
        compiler_params=_params("arbitrary"),
        name="cum_sample",
    )(pt, *([logf_view] * npg), pf, bf)


def _block_diag_q(q):
    qt = jnp.concatenate([q] * N_HEADS, axis=0)
    same = jnp.right_shift(_iota(qt.shape, 1), 6) == jnp.right_shift(_iota(qt.shape, 0), 3)
    return jnp.where(same, qt * SCALE, 0.0).astype(BF16)


def _rows_per_head(x):
    return jnp.concatenate([jnp.broadcast_to(x[h:h + 1], (DEC_T, x.shape[1])) for h in range(N_HEADS)], axis=0)


def _col_per_head(x):
    return jnp.concatenate([x[:, h:h + 1] for h in range(N_HEADS)], axis=0)


def _own_head_lanes(acc):
    lane_h = jnp.right_shift(_iota((DEC_T, HW), 1), 6)
    out = jnp.zeros((DEC_T, HW), F32)
    for h in range(N_HEADS):
        out = jnp.where(lane_h == h, acc[h * DEC_T:(h + 1) * DEC_T], out)
    return out


def _pad_rows(x, n):
    return jnp.concatenate([x, jnp.zeros((n - x.shape[0], x.shape[1]), x.dtype)], axis=0)


def _new_key_mask(strict):
    shape = (N_HEADS * DEC_T, PAGE)
    t_row = jnp.bitwise_and(_iota(shape, 0), DEC_T - 1)
    return (_iota(shape, 1) < t_row) if strict else (_iota(shape, 1) <= t_row)


def _fox_sample_kernel(pt_ref, *refs, npg):
    pages = refs[:npg]
    q_ref, k_ref, v_ref, cp_ref, cn_ref, cnt_ref, o_ref = refs[npg:]
    qbd = _block_diag_q(q_ref[...])
    cq = _col_per_head(cn_ref[...])
    s_pages = []
    for j in range(npg):
        s = _dot(qbd, pages[j][0, 0, 0].astype(BF16))
        s_pages.append(s + cq - _rows_per_head(cp_ref[0, j]))
    k_new = _pad_rows(k_ref[...], PAGE).astype(BF16)
    s_new = _dot_nt(qbd, k_new) + cq - _rows_per_head(cnt_ref[0])
    mask_new = _new_key_mask(strict=False)
    s_new = jnp.where(mask_new, s_new, NEG_INF)
    m = jnp.max(s_new, axis=-1, keepdims=True)
    for s in s_pages:
        m = jnp.maximum(m, jnp.max(s, axis=-1, keepdims=True))
    p_new = jnp.where(mask_new, jnp.exp(s_new - m), 0.0)
    l = jnp.sum(p_new, axis=-1, keepdims=True)
    acc = _dot(p_new.astype(BF16), _pad_rows(v_ref[...], PAGE).astype(BF16))
    for j in range(npg):
        pr = jnp.exp(s_pages[j] - m)
        l = l + jnp.sum(pr, axis=-1, keepdims=True)
        acc = acc + _dot_nt(pr.astype(BF16), pages[j][0, 0, 1].astype(BF16))
    o_ref[...] = _own_head_lanes(acc / jnp.maximum(l, TINY))


def _fox_sample(pt, kv_view, pf, cum_past, cum_new, cum_new_t, layer, nbs, npg, row0):
    kernel = functools.partial(_fox_sample_kernel, npg=npg)
    rb = row0 // DEC_T
    grid_spec = pltpu.PrefetchScalarGridSpec(
        num_scalar_prefetch=1,
        grid=(nbs,),
        in_specs=_page_specs(npg, (1, 1, 2, HW, PAGE), layer, (0, 0, 0)) + [
            pl.BlockSpec((DEC_T, HW), lambda b, pt: (rb + b, C_QA // HW)),
            pl.BlockSpec((DEC_T, HW), lambda b, pt: (rb + b, C_KA // HW)),
            pl.BlockSpec((DEC_T, HW), lambda b, pt: (rb + b, C_VA // HW)),
            pl.BlockSpec((1, npg, N_HEADS, PAGE), lambda b, pt: (b, 0, 0, 0)),
            pl.BlockSpec((DEC_T, LANES), lambda b, pt: (b, 0)),
            pl.BlockSpec((1, N_HEADS, LANES), lambda b, pt: (b, 0, 0))],
        out_specs=pl.BlockSpec((DEC_T, HW), lambda b, pt: (b, 0)),
    )
    return pl.pallas_call(
        kernel,
        grid_spec=grid_spec,
        out_shape=jax.ShapeDtypeStruct((nbs * DEC_T, HW), F32),
        compiler_params=_params("arbitrary"),
        name="fox_sample",
    )(pt, *([kv_view] * npg), pf, pf, pf, cum_past, cum_new, cum_new_t)


def _sb_sample_kernel(pt_ref, *refs, npg):
    pages = refs[:npg]
    q_ref, k_ref, v_ref, o_ref = refs[npg:]
    qbd = _block_diag_q(q_ref[...])
    tri = _tri(PAGE)
    z = _dot_nt(qbd, _pad_rows(k_ref[...], PAGE).astype(BF16))
    mask = _new_key_mask(strict=True)
    sp = _softplus(z)
    lk = jnp.where(mask, -sp, 0.0)
    between = _split_dot(lk, tri)
    a = jnp.where(mask, jnp.exp(z - sp + between), 0.0)
    acc = _dot(a.astype(BF16), _pad_rows(v_ref[...], PAGE).astype(BF16))
    r = jnp.sum(lk, axis=-1, keepdims=True)
    for j in reversed(range(npg)):
        z = _dot(qbd, pages[j][0, 0, 0].astype(BF16))
        sp = _softplus(z)
        lk = -sp
        between = _split_dot(lk, tri) + r
        a = jnp.exp(z - sp + between)
        acc = acc + _dot_nt(a.astype(BF16), pages[j][0, 0, 1].astype(BF16))
        r = r + jnp.sum(lk, axis=-1, keepdims=True)
    o_ref[...] = _own_head_lanes(acc)


def _sb_sample(pt, kv_view, pf, layer, nbs, npg, row0):
    kernel = functools.partial(_sb_sample_kernel, npg=npg)
    rb = row0 // DEC_T
    grid_spec = pltpu.PrefetchScalarGridSpec(
        num_scalar_prefetch=1,
        grid=(nbs,),
        in_specs=_page_specs(npg, (1, 1, 2, HW, PAGE), layer, (0, 0, 0)) + [
            pl.BlockSpec((DEC_T, HW), lambda b, pt: (rb + b, C_QC // HW)),
            pl.BlockSpec((DEC_T, HW), lambda b, pt: (rb + b, C_KC // HW)),
            pl.BlockSpec((DEC_T, HW), lambda b, pt: (rb + b, C_VC // HW))],
        out_specs=pl.BlockSpec((DEC_T, HW), lambda b, pt: (b, 0)),
    )
    return pl.pallas_call(
        kernel,
        grid_spec=grid_spec,
        out_shape=jax.ShapeDtypeStruct((nbs * DEC_T, HW), F32),
        compiler_params=_params("arbitrary"),
        name="sb_sample",
    )(pt, *([kv_view] * npg), pf, pf, pf)


def _nsa_gather_kernel(pt_ref, *refs, npg):
    pages = refs[:npg]
    o_ref = refs[npg]
    for j in range(npg):
        o_ref[0, j] = pages[j][0, 0, 0].astype(BF16)


def _nsa_gather(pt, nsa_view, layer, nbs, npg):
    kernel = functools.partial(_nsa_gather_kernel, npg=npg)
    grid_spec = pltpu.PrefetchScalarGridSpec(
        num_scalar_prefetch=1,
        grid=(nbs,),
        in_specs=_page_specs(npg, (1, 1, 1, 2, LANES, PAGE), layer, (0, 0, 0, 0)),
        out_specs=pl.BlockSpec((1, npg, 2, LANES, PAGE), lambda b, pt: (b, 0, 0, 0, 0)),
    )
    return pl.pallas_call(
        kernel,
        grid_spec=grid_spec,
        out_shape=jax.ShapeDtypeStruct((nbs, npg, 2, LANES, PAGE), BF16),
        compiler_params=_params("arbitrary"),
        name="nsa_gather",
    )(pt, *([nsa_view] * npg))


def _nsa_sample_kernel(pt_ref, *refs, npg, n_s):
    pages = refs[:npg]
    q_ref, seln_ref, winn_ref, misc_ref, ck_ref, cv_ref, wst_ref, t5_ref, exp_ref, o_ref = refs[npg:]
    past = npg * PAGE
    rows = N_HEADS * DEC_T
    nch = ck_ref.shape[2]
    wb = wst_ref.shape[4]
    lane8 = _iota((DEC_T, LANES), 1)
    t5c = t5_ref[...]

    def lut(rel):
        bk = _bucket(rel)
        out = jnp.zeros(rel.shape, F32)
        for k in range(N_BUCKETS):
            out = jnp.where(bk == k, t5c[:, k:k + 1], out)
        return out

    def t_of(shape):
        return jnp.bitwise_and(_iota(shape, 0), DEC_T - 1)

    def g_rows(x):
        return jnp.concatenate([x] * NSA_HG, axis=0)

    q = q_ref[...]
    qrows = []
    for head in range(N_HEADS):
        g = head // NSA_HG
        x = q[:, (head // 2) * LANES:(head // 2 + 1) * LANES] * SCALE
        if head % 2 != g:
            x = pltpu.roll(x, HEAD_DIM, axis=1)
        qrows.append(jnp.where((lane8 >= HEAD_DIM) == (g == 1), x, 0.0))
    qbd = jnp.concatenate(qrows, axis=0).astype(BF16)
    bias_far = t5c[:, N_BUCKETS - 1:N_BUCKETS]

    qpos_c = past + t_of((rows, nch))
    rel_c = qpos_c - (_iota((rows, nch), 1) * CMP_STRIDE + (CMP_BLOCK - 1))
    mask_c = (rel_c >= 0) & (_iota((rows, nch), 1) < nch - 1)
    pc = _masked_softmax(_dot_nt(qbd, ck_ref[0, 0]) + lut(rel_c), mask_c)
    o_cmp = _dot(pc.astype(BF16), cv_ref[0, 0])
    ov = _overlap(nch)
    psums = []
    for g in range(NSA_G):
        base = g * NSA_HG * DEC_T
        psum = pc[base:base + DEC_T]
        for hg in range(1, NSA_HG):
            psum = psum + pc[base + hg * DEC_T:base + (hg + 1) * DEC_T]
        psums.append(psum)
    qpos_g = past + jnp.bitwise_and(_iota((NSA_G * DEC_T, 1), 0), DEC_T - 1)
    picked = _select_blocks(_split_dot(jnp.concatenate(psums, axis=0), ov), qpos_g, n_s)
    picked = jnp.concatenate([g_rows(picked[g * DEC_T:(g + 1) * DEC_T]) for g in range(NSA_G)], axis=0)

    sel_past = _dot(picked.astype(BF16), exp_ref[...])
    last_blk = past // SEL_BLOCK
    sel_new = jnp.sum(jnp.where(_iota((rows, LANES), 1) == last_blk, picked, 0.0), axis=-1, keepdims=True) > 0.5
    rel_new = t_of((rows, PAGE)) - _iota((rows, PAGE), 1)
    in_new = _iota((rows, PAGE), 1) < DEC_T
    bias_new = lut(rel_new)
    s_pages, m_pages = [], []
    for j in range(npg):
        s = _dot(qbd, pages[j][0, 0, 0, 0].astype(BF16))
        if past - (j + 1) * PAGE + 1 >= MAX_DISTANCE:
            s = s + bias_far
        else:
            s = s + lut(past + t_of((rows, PAGE)) - (j * PAGE + _iota((rows, PAGE), 1)))
        mk = sel_past[:, j * PAGE:(j + 1) * PAGE] > 0.5
        s_pages.append(jnp.where(mk, s, NEG_INF))
        m_pages.append(mk)
    mask_n = sel_new & (rel_new >= 0) & in_new
    seln = _pad_rows(seln_ref[...], PAGE).astype(BF16)
    s_new = jnp.where(mask_n, _dot_nt(qbd, seln[:, 0:LANES]) + bias_new, NEG_INF)
    m = jnp.max(s_new, axis=-1, keepdims=True)
    for s in s_pages:
        m = jnp.maximum(m, jnp.max(s, axis=-1, keepdims=True))
    p_new = jnp.where(mask_n, jnp.exp(s_new - m), 0.0)
    l = jnp.sum(p_new, axis=-1, keepdims=True)
    acc = _dot(p_new.astype(BF16), seln[:, LANES:2 * LANES])
    for j in range(npg):
        pr = jnp.where(m_pages[j], jnp.exp(s_pages[j] - m), 0.0)
        l = l + jnp.sum(pr, axis=-1, keepdims=True)
        acc = acc + _dot_nt(pr.astype(BF16), pages[j][0, 0, 0, 1].astype(BF16))
    o_sel = acc / jnp.maximum(l, TINY)

    rel_w = (wb + t_of((rows, wb))) - _iota((rows, wb), 1)
    mask_w = (rel_w >= 0) & (rel_w < WINDOW)
    s_w = jnp.where(mask_w, _dot(qbd, wst_ref[0, 0, 0].astype(BF16)) + lut(rel_w), NEG_INF)
    mask_wn = (rel_new >= 0) & (rel_new < WINDOW) & in_new
    winn = _pad_rows(winn_ref[...], PAGE).astype(BF16)
    s_wn = jnp.where(mask_wn, _dot_nt(qbd, winn[:, 0:LANES]) + bias_new, NEG_INF)
    m = jnp.maximum(jnp.max(s_w, axis=-1, keepdims=True), jnp.max(s_wn, axis=-1, keepdims=True))
    p_w = jnp.where(mask_w, jnp.exp(s_w - m), 0.0)
    p_wn = jnp.where(mask_wn, jnp.exp(s_wn - m), 0.0)
    l = jnp.sum(p_w, axis=-1, keepdims=True) + jnp.sum(p_wn, axis=-1, keepdims=True)
    acc = _dot_nt(p_w.astype(BF16), wst_ref[0, 0, 1].astype(BF16)) + _dot(p_wn.astype(BF16), winn[:, LANES:2 * LANES])
    o_win = acc / jnp.maximum(l, TINY)

    sig = _sigmoid(misc_ref[:, 0:LANES])

    def gate(br):
        return jnp.concatenate(
            [sig[:, 8 + br * N_HEADS + h:9 + br * N_HEADS + h] for h in range(N_HEADS)], axis=0)

    o_all = gate(0) * o_cmp + gate(1) * o_sel + gate(2) * o_win
    pieces = []
    for head in range(N_HEADS):
        x = o_all[head * DEC_T:(head + 1) * DEC_T]
        if head % 2 != head // NSA_HG:
            x = pltpu.roll(x, HEAD_DIM, axis=1)
        pieces.append(x)
    for p in range(N_HEADS // 2):
        o_ref[:, p * LANES:(p + 1) * LANES] = jnp.where(lane8 < HEAD_DIM, pieces[2 * p], pieces[2 * p + 1])


def _nsa_sample(pt, nsa_view, pf, comp, win_view, t5col, expand, layer, nbs, npg, row0):
    nch = comp.shape[2]
    wb = win_view.shape[4]
    past = npg * PAGE
    kernel = functools.partial(_nsa_sample_kernel, npg=npg, n_s=-(-(past + DEC_T) // SEL_BLOCK))
    rb = row0 // DEC_T
    grid_spec = pltpu.PrefetchScalarGridSpec(
        num_scalar_prefetch=1,
        grid=(nbs,),
        in_specs=_page_specs(npg, (1, 1, 1, 2, LANES, PAGE), layer, (1, 0, 0, 0)) + [
            pl.BlockSpec((DEC_T, HW), lambda b, pt: (rb + b, C_QB // HW)),
            pl.BlockSpec((DEC_T, 2 * LANES), lambda b, pt: (rb + b, C_SEL // (2 * LANES))),
            pl.BlockSpec((DEC_T, 2 * LANES), lambda b, pt: (rb + b, C_WIN // (2 * LANES))),
            pl.BlockSpec((DEC_T, MISC_W), lambda b, pt: (rb + b, C_MISC // MISC_W)),
            pl.BlockSpec((1, 1, nch, LANES), lambda b, pt: (0, b, 0, 0)),
            pl.BlockSpec((1, 1, nch, LANES), lambda b, pt: (1, b, 0, 0)),
            pl.BlockSpec((1, 1, 2, LANES, wb), lambda b, pt: (b, layer, 0, 0, 0)),
            pl.BlockSpec((N_HEADS * DEC_T, LANES), lambda b, pt: (0, 0)),
            pl.BlockSpec((LANES, past), lambda b, pt: (0, 0))],
        out_specs=pl.BlockSpec((DEC_T, HW), lambda b, pt: (b, 0)),
    )
    return pl.pallas_call(
        kernel,
        grid_spec=grid_spec,
        out_shape=jax.ShapeDtypeStruct((nbs * DEC_T, HW), F32),
        compiler_params=_params("arbitrary"),
        name="nsa_sample",
    )(pt, *([nsa_view] * npg), pf, pf, pf, pf, comp, comp, win_view, t5col, expand)


def _reorder_w_in(w_in, d):
    sizes = (HW, 2 * HW, N_HEADS, HW, 6 * NSA_G * HEAD_DIM, 3 * N_HEADS, HW, 2 * HW, 3 * d)
    offs = np.concatenate([[0], np.cumsum(sizes)])
    seg = [w_in[:, :, offs[i]:offs[i + 1]] for i in range(len(sizes))]
    q_a, kv_a, f_a, q_b, kv_b, g_b, q_c, kv_c, g_m = seg
    pad = jnp.zeros(w_in.shape[:2] + (MISC_W - N_HEADS - 3 * N_HEADS,), w_in.dtype)
    return jnp.concatenate([q_a, kv_a, q_b, q_c, kv_c, kv_b, f_a, g_b, pad, g_m], axis=-1).astype(BF16)


def kernel(x_prompt, x_sample, cache_fox_kv, cache_fox_logf, cache_nsa_kv, cache_sb_kv, state_nsa_win_kv, page_table, norm_mix_g, norm_ffn_g, norm_final_g, w_in, b_forget, t5_table, cmp_pe, cmp_w1, cmp_w2, w_out_a, w_out_b, w_out_c, w_out, router_group_w, router_group_b, router_expert_w, router_expert_b, expert_w_gate, expert_w_up, expert_w_down):
    nb, t, d = x_prompt.shape
    nbs, dec_t, _ = x_sample.shape
    depth = w_in.shape[0]
    npool = cache_fox_kv.shape[0]
    npg = page_table.shape[1]
    past = npg * PAGE
    wb = state_nsa_win_kv.shape[2]
    assert dec_t == DEC_T and t % QB == 0 and d % 128 == 0 and wb == WINDOW and past >= WINDOW
    n_p, n_s_rows = nb * t, nbs * DEC_T
    nq = t // QB

    fox_view = jnp.transpose(cache_fox_kv, (0, 1, 3, 4, 5, 2)).reshape(npool, depth, 2, HW, PAGE)
    sb_view = jnp.transpose(cache_sb_kv, (0, 1, 3, 4, 5, 2)).reshape(npool, depth, 2, HW, PAGE)
    nsa_view = jnp.transpose(cache_nsa_kv, (0, 1, 3, 4, 5, 6, 2)).reshape(npool, depth, 2, 2, LANES, PAGE)
    logf_view = jnp.transpose(cache_fox_logf, (0, 1, 3, 2))
    win_view = jnp.transpose(state_nsa_win_kv, (0, 1, 3, 4, 5, 2)).reshape(nbs, depth, 2, LANES, wb)
    pt = page_table.reshape(-1).astype(jnp.int32)

    w_in_r = _reorder_w_in(w_in, d)
    bf_pad = jnp.pad(b_forget.astype(F32), ((0, 0), (0, LANES - N_HEADS))).reshape(depth, 1, LANES)
    t5 = t5_table.astype(F32)
    t5col = jnp.pad(jnp.repeat(t5.T, DEC_T, axis=0), ((0, 0), (0, LANES - N_BUCKETS)))
    expand = jnp.asarray(np.arange(LANES)[:, None] == (np.arange(past)[None, :] // SEL_BLOCK), BF16)
    half = CMP_BLOCK // 2 * HEAD_DIM
    w1ab = jnp.concatenate([cmp_w1[:, :, :CMP_BLOCK // 2].reshape(depth, 2, half, CMP_HIDDEN),
                            cmp_w1[:, :, CMP_BLOCK // 2:].reshape(depth, 2, half, CMP_HIDDEN)], axis=-1).astype(BF16)
    pe2 = jnp.pad(cmp_pe.reshape(depth, 2, 2, half), ((0, 0), (0, 0), (0, 6), (0, 0))).astype(F32)
    w2p = jnp.stack([jnp.pad(cmp_w2, ((0, 0), (0, 0), (0, 0), (g * HEAD_DIM, LANES - (g + 1) * HEAD_DIM)))
                     for g in range(NSA_G)], axis=2).astype(BF16)
    w_router = jnp.pad(jnp.concatenate([router_group_w, router_expert_w], axis=-1),
                       ((0, 0), (0, 0), (0, LANES - N_GROUPS - N_EXPERTS))).astype(F32)
    b_router = jnp.pad(jnp.concatenate([router_group_b, router_expert_b], axis=-1),
                       ((0, 0), (0, LANES - N_GROUPS - N_EXPERTS))).astype(F32).reshape(depth, 1, LANES)
    wa, wb_, wc, wo = (w.astype(BF16) for w in (w_out_a, w_out_b, w_out_c, w_out))
    wg, wu, wd = (w.astype(BF16) for w in (expert_w_gate, expert_w_up, expert_w_down))

    h = jnp.concatenate([x_prompt.reshape(n_p, d), x_sample.reshape(n_s_rows, d)], axis=0)
    st_p = [[] for _ in range(5)]
    st_s = [[] for _ in range(5)]
    for l in range(depth):
        pf, pb = _proj(h, norm_mix_g[l], w_in_r[l])

        logf_p, cum_p = _cum_prompt(pf, bf_pad[l], nb, t)
        cum_t = jnp.transpose(cum_p[:, :N_HEADS].reshape(nb, t, N_HEADS), (0, 2, 1)).reshape(nb, N_HEADS, nq, QB)
        o_a_p = _fox_prompt(pb, cum_p, cum_t, nb, t)
        o_c_p = _sb_prompt(pb, nb, t)
        cmp_rows = pb[:n_p, C_CMP:C_CMP + 2 * LANES].reshape(nb, t // CMP_STRIDE, CMP_STRIDE, 2, NSA_G, HEAD_DIM)
        chunks_p = jnp.transpose(cmp_rows, (3, 0, 4, 1, 2, 5)).reshape(2, nb, NSA_G, t // CMP_STRIDE, half)
        comp_p = _compress(chunks_p, w1ab[l], pe2[l], w2p[l])
        o_b_p = _nsa_prompt(pb, pf, comp_p, t5, nb, t)

        cum_past, logf_s, cum_new = _cum_sample(pt, logf_view, pf, bf_pad[l], l, nbs, npg, n_p)
        cum_new_t = jnp.pad(jnp.transpose(cum_new[:, :N_HEADS].reshape(nbs, DEC_T, N_HEADS), (0, 2, 1)),
                            ((0, 0), (0, 0), (0, LANES - DEC_T)))
        o_a_s = _fox_sample(pt, fox_view, pf, cum_past, cum_new, cum_new_t, l, nbs, npg, n_p)
        o_c_s = _sb_sample(pt, sb_view, pf, l, nbs, npg, n_p)
        dense = _nsa_gather(pt, nsa_view, l, nbs, npg)
        dense = dense.reshape(nbs, npg, 2, NSA_G, HEAD_DIM, PAGE // CMP_STRIDE, CMP_STRIDE)
        chunks_s = jnp.transpose(dense, (2, 0, 3, 1, 5, 6, 4)).reshape(2, nbs, NSA_G, past // CMP_STRIDE, half)
        comp_s = _compress(chunks_s, w1ab[l], pe2[l], w2p[l])
        o_b_s = _nsa_sample(pt, nsa_view, pf, comp_s, win_view, t5col, expand, l, nbs, npg, n_p)

        o_a = jnp.concatenate([o_a_p, o_a_s.astype(BF16)], axis=0)
        o_b = jnp.concatenate([o_b_p, o_b_s.astype(BF16)], axis=0)
        o_c = jnp.concatenate([o_c_p, o_c_s.astype(BF16)], axis=0)
        mixed = _merge(o_a, o_b, o_c, pf, wa[l], wb_[l], wc[l], d)
        h = _mm_res(mixed, wo[l], h)
        xn, comb = _router(h, norm_ffn_g[l], w_router[l], b_router[l])
        h = _moe(xn, comb, h, wg[l], wu[l], wd[l])

        def rows(c0, width, shape, lo, hi):
            return pf[lo:hi, c0:c0 + width].reshape(shape)

        win_new_p = rows(C_WIN, 2 * LANES, (nb, t, 2, NSA_G, HEAD_DIM), 0, n_p)
        win_new_s = rows(C_WIN, 2 * LANES, (nbs, DEC_T, 2, NSA_G, HEAD_DIM), n_p, n_p + n_s_rows)
        st_p[0].append(rows(C_KA, 2 * HW, (nb, t, 2, N_HEADS, HEAD_DIM), 0, n_p))
        st_p[1].append(logf_p[:, :N_HEADS].reshape(nb, t, N_HEADS))
        st_p[2].append(rows(C_CMP, 4 * LANES, (nb, t, 2, 2, NSA_G, HEAD_DIM), 0, n_p))
        st_p[3].append(rows(C_KC, 2 * HW, (nb, t, 2, N_HEADS, HEAD_DIM), 0, n_p))
        st_p[4].append(win_new_p[:, t - min(WINDOW, t):])
        st_s[0].append(rows(C_KA, 2 * HW, (nbs, DEC_T, 2, N_HEADS, HEAD_DIM), n_p, n_p + n_s_rows))
        st_s[1].append(logf_s[:, :N_HEADS].reshape(nbs, DEC_T, N_HEADS))
        st_s[2].append(rows(C_CMP, 4 * LANES, (nbs, DEC_T, 2, 2, NSA_G, HEAD_DIM), n_p, n_p + n_s_rows))
        st_s[3].append(rows(C_KC, 2 * HW, (nbs, DEC_T, 2, N_HEADS, HEAD_DIM), n_p, n_p + n_s_rows))
        wkv = jnp.concatenate([state_nsa_win_kv[:, l], win_new_s], axis=1)
        st_s[4].append(wkv[:, wb + DEC_T - min(WINDOW, wb + DEC_T):])

    y = _final_norm(h, norm_final_g)
    y_prompt = y[:n_p].reshape(nb, t, d)
    y_sample = y[n_p:].reshape(nbs, DEC_T, d)
    sp = [jnp.stack(s, axis=1) for s in st_p]
    ss = [jnp.stack(s, axis=1) for s in st_s]
    return (y_prompt, y_sample, sp[0], ss[0], sp[1], ss[1], sp[2], ss[2], sp[3], ss[3], sp[4], ss[4])
```

```python
import functools
import math

import numpy as np
import jax
import jax.numpy as jnp
from jax import lax
from jax.experimental import pallas as pl
from jax.experimental.pallas import tpu as pltpu

F32 = jnp.float32
BF16 = jnp.bfloat16

HEAD_DIM = 64
N_HEADS = 8
NSA_G = 2
NSA_HG = N_HEADS // NSA_G
HW = N_HEADS * HEAD_DIM
PAGE = 128
DEC_T = 8
CMP_BLOCK = 32
CMP_STRIDE = 16
CMP_HIDDEN = 128
SEL_BLOCK = 64
SEL_TOPK = 8
WINDOW = 512
N_BUCKETS = 32
MAX_DISTANCE = 128
N_GROUPS = 4
EXPERTS_PER_GROUP = 4
N_EXPERTS = N_GROUPS * EXPERTS_PER_GROUP
RMS_EPS = 1e-6
NEG_INF = -1e30
FORCE_SCORE = 1e9
BELOW_ALL = -3e38
SCALE = HEAD_DIM ** -0.5
TINY = float(np.finfo(np.float32).tiny)
QB = 128
KB = 256
LANES = 128
VMEM_LIMIT = 56 * 1024 * 1024

C_QA, C_KA, C_VA, C_QB, C_QC, C_KC, C_VC = 0, 512, 1024, 1536, 2048, 2560, 3072
C_CMP, C_SEL, C_WIN, C_MISC, C_GM = 3584, 3840, 4096, 4352, 4608
MISC_W = 256
CHUNK_W = CMP_STRIDE * LANES


def _params(*sem):
    return pltpu.CompilerParams(dimension_semantics=sem, vmem_limit_bytes=VMEM_LIMIT)


def _pick(n, cap, mult):
    t = (min(cap, n) // mult) * mult
    while t > 0 and n % t:
        t -= mult
    assert t > 0, (n, cap, mult)
    return t


def _iota(shape, axis):
    return lax.broadcasted_iota(jnp.int32, shape, axis)


def _dot(a, b):
    return jnp.dot(a, b, preferred_element_type=F32)


def _dot_nt(a, b):
    return lax.dot_general(a, b, (((1,), (1,)), ((), ())), preferred_element_type=F32)


def _softplus(x):
    return jnp.maximum(x, 0.0) + jnp.log1p(jnp.exp(-jnp.abs(x)))


def _log_sigmoid(x):
    return jnp.minimum(x, 0.0) - jnp.log1p(jnp.exp(-jnp.abs(x)))


def _sigmoid(x):
    return 1.0 / (1.0 + jnp.exp(-x))


def _split_dot(x, w):
    hi = x.astype(BF16)
    lo = (x - hi.astype(F32)).astype(BF16)
    return _dot(hi, w) + _dot(lo, w)


def _tri(w):
    return jnp.where(_iota((w, w), 0) > _iota((w, w), 1), 1.0, 0.0).astype(BF16)


def _bucket(rel):
    n = jnp.maximum(rel, 0)
    exact = N_BUCKETS // 2
    far = jnp.log(jnp.maximum(n, exact).astype(F32) / exact) / math.log(MAX_DISTANCE / exact)
    far = exact + (far * (N_BUCKETS - exact)).astype(jnp.int32)
    return jnp.where(n < exact, n, jnp.minimum(far, N_BUCKETS - 1))


def _masked_softmax(logits, mask):
    l = jnp.where(mask, logits, NEG_INF)
    e = jnp.where(mask, jnp.exp(l - jnp.max(l, axis=-1, keepdims=True)), 0.0)
    return e / jnp.maximum(jnp.sum(e, axis=-1, keepdims=True), TINY)


def _overlap(nch):
    cs = _iota((nch, LANES), 0) * CMP_STRIDE
    ss = _iota((nch, LANES), 1) * SEL_BLOCK
    ov = jnp.maximum(jnp.minimum(cs + CMP_BLOCK, ss + SEL_BLOCK) - jnp.maximum(cs, ss), 0)
    return (ov.astype(F32) / CMP_BLOCK).astype(BF16)


def _select_blocks(score, qpos, n_s):
    lane = _iota(score.shape, 1)
    lane_f = lane.astype(F32)
    valid = lane * SEL_BLOCK <= qpos
    forced = (lane == jnp.right_shift(qpos, 6)) | (lane == 0)
    sc = jnp.where(forced, FORCE_SCORE, jnp.where(valid, score, NEG_INF))
    sc = jnp.where(lane < n_s, sc, BELOW_ALL)
    sel = jnp.zeros(score.shape, F32)
    for _ in range(min(SEL_TOPK, n_s)):
        mx = jnp.max(sc, axis=-1, keepdims=True)
        idx = jnp.min(jnp.where(sc == mx, lane_f, 1e9), axis=-1, keepdims=True)
        pick = lane_f == idx
        sel = jnp.where(pick, 1.0, sel)
        sc = jnp.where(pick, -3.3e38, sc)
    return sel


def _proj_kernel(x_ref, g_ref, w_ref, of_ref, ob_ref, xn_ref):
    @pl.when(pl.program_id(1) == 0)
    def _():
        x = x_ref[...]
        ms = jnp.mean(x * x, axis=-1, keepdims=True)
        xn_ref[...] = (x * lax.rsqrt(ms + RMS_EPS) * g_ref[...]).astype(BF16)

    y = _dot(xn_ref[...], w_ref[...])
    of_ref[...] = y
    ob_ref[...] = y.astype(BF16)


def _proj(h, g, w, layer):
    n, d = h.shape
    wp = w.shape[2]
    tm = _pick(n, 512, 16)
    tn = _pick(wp, 1536, 128)
    return pl.pallas_call(
        _proj_kernel,
        grid=(n // tm, wp // tn),
        in_specs=[pl.BlockSpec((tm, d), lambda i, j: (i, 0)),
                  pl.BlockSpec((1, d), lambda i, j: (0, 0)),
                  pl.BlockSpec((None, d, tn), lambda i, j: (layer, 0, j))],
        out_specs=[pl.BlockSpec((tm, tn), lambda i, j: (i, j)),
                   pl.BlockSpec((tm, tn), lambda i, j: (i, j))],
        out_shape=[jax.ShapeDtypeStruct((n, wp), F32), jax.ShapeDtypeStruct((n, wp), BF16)],
        scratch_shapes=[pltpu.VMEM((tm, d), BF16)],
        compiler_params=_params("arbitrary", "arbitrary"),
        name="proj",
    )(h, g.reshape(1, d), w)


def _merge_kernel(oa_ref, ob_ref, oc_ref, wa_ref, wb_ref, wc_ref, g0_ref, g1_ref, g2_ref, o_ref):
    m = _sigmoid(g0_ref[...]) * _dot(oa_ref[...], wa_ref[...])
    m = m + _sigmoid(g1_ref[...]) * _dot(ob_ref[...], wb_ref[...])
    m = m + _sigmoid(g2_ref[...]) * _dot(oc_ref[...], wc_ref[...])
    o_ref[...] = m.astype(BF16)


def _merge(o_a, o_b, o_c, pf, wa, wb, wc, d, layer):
    n = o_a.shape[0]
    tm = _pick(n, 512, 16)
    tn = _pick(d, 512, 128)
    gm0 = C_GM // tn
    o_spec = pl.BlockSpec((tm, HW), lambda i, j: (i, 0))
    w_spec = pl.BlockSpec((None, HW, tn), lambda i, j: (layer, 0, j))

    def g_spec(k):
        return pl.BlockSpec((tm, tn), lambda i, j: (i, gm0 + k * (d // tn) + j))

    return pl.pallas_call(
        _merge_kernel,
        grid=(n // tm, d // tn),
        in_specs=[o_spec, o_spec, o_spec, w_spec, w_spec, w_spec, g_spec(0), g_spec(1), g_spec(2)],
        out_specs=pl.BlockSpec((tm, tn), lambda i, j: (i, j)),
        out_shape=jax.ShapeDtypeStruct((n, d), BF16),
        compiler_params=_params("arbitrary", "arbitrary"),
        name="merge",
    )(o_a, o_b, o_c, wa, wb, wc, pf, pf, pf)


def _mm_res_kernel(x_ref, w_ref, r_ref, o_ref):
    o_ref[...] = r_ref[...] + _dot(x_ref[...], w_ref[...])


def _mm_res(x, w, res, layer):
    n, k = x.shape
    d = w.shape[2]
    tm = _pick(n, 512, 16)
    tn = _pick(d, 512, 128)
    return pl.pallas_call(
        _mm_res_kernel,
        grid=(n // tm, d // tn),
        in_specs=[pl.BlockSpec((tm, k), lambda i, j: (i, 0)),
                  pl.BlockSpec((None, k, tn), lambda i, j: (layer, 0, j)),
                  pl.BlockSpec((tm, tn), lambda i, j: (i, j))],
        out_specs=pl.BlockSpec((tm, tn), lambda i, j: (i, j)),
        out_shape=jax.ShapeDtypeStruct((n, d), F32),
        compiler_params=_params("arbitrary", "arbitrary"),
        name="out_proj",
    )(x, w, res)


def _router_kernel(h_ref, g_ref, w_ref, b_ref, xn_ref, comb_ref):
    x = h_ref[...]
    ms = jnp.mean(x * x, axis=-1, keepdims=True)
    xn = x * lax.rsqrt(ms + RMS_EPS) * g_ref[...]
    xn_ref[...] = xn.astype(BF16)
    logits = jnp.dot(xn, w_ref[...], precision=lax.Precision.HIGHEST, preferred_element_type=F32) + b_ref[...]
    lane = _iota(logits.shape, 1)
    lane_f = lane.astype(F32)
    is_grp = lane < N_GROUPS
    gl = jnp.where(is_grp, logits, BELOW_ALL)
    gmax = jnp.max(gl, axis=-1, keepdims=True)
    gsum = jnp.sum(jnp.where(is_grp, jnp.exp(gl - gmax), 0.0), axis=-1, keepdims=True)
    w_grp = 1.0 / gsum
    g_star = jnp.min(jnp.where(is_grp & (gl == gmax), lane_f, 1e9), axis=-1, keepdims=True)
    lo = N_GROUPS + g_star * EXPERTS_PER_GROUP
    in_grp = (lane_f >= lo) & (lane_f < lo + EXPERTS_PER_GROUP)
    el = jnp.where(in_grp, logits, BELOW_ALL)
    v1 = jnp.max(el, axis=-1, keepdims=True)
    i1 = jnp.min(jnp.where(el == v1, lane_f, 1e9), axis=-1, keepdims=True)
    el2 = jnp.where(lane_f == i1, BELOW_ALL, el)
    v2 = jnp.max(el2, axis=-1, keepdims=True)
    i2 = jnp.min(jnp.where(el2 == v2, lane_f, 1e9), axis=-1, keepdims=True)
    e2 = jnp.exp(v2 - v1)
    den = 1.0 + e2
    comb = jnp.where(lane_f == i1, w_grp / den, 0.0) + jnp.where(lane_f == i2, w_grp * e2 / den, 0.0)
    comb_ref[...] = comb


def _router(h, g, w, b):
    n, d = h.shape
    tm = _pick(n, 512, 16)
    return pl.pallas_call(
        _router_kernel,
        grid=(n // tm,),
        in_specs=[pl.BlockSpec((tm, d), lambda i: (i, 0)),
                  pl.BlockSpec((1, d), lambda i: (0, 0)),
                  pl.BlockSpec((d, LANES), lambda i: (0, 0)),
                  pl.BlockSpec((1, LANES), lambda i: (0, 0))],
        out_specs=[pl.BlockSpec((tm, d), lambda i: (i, 0)),
                   pl.BlockSpec((tm, LANES), lambda i: (i, 0))],
        out_shape=[jax.ShapeDtypeStruct((n, d), BF16), jax.ShapeDtypeStruct((n, LANES), F32)],
        compiler_params=_params("arbitrary"),
        name="router",
    )(h, g.reshape(1, d), w, b)


def _moe_kernel(x_ref, c_ref, h_ref, wg_ref, wu_ref, wd_ref, o_ref, acc_ref):
    e = pl.program_id(1)

    @pl.when(e == 0)
    def _():
        acc_ref[...] = h_ref[...]

    x = x_ref[...]
    comb = c_ref[...]
    ce = jnp.sum(jnp.where(_iota(comb.shape, 1) == e + N_GROUPS, comb, 0.0), axis=-1, keepdims=True)
    gate = _dot(x, wg_ref[0])
    up = _dot(x, wu_ref[0])
    hh = gate * _sigmoid(gate) * up * ce
    acc_ref[...] += _dot(hh.astype(BF16), wd_ref[0])

    @pl.when(e == pl.num_programs(1) - 1)
    def _():
        o_ref[...] = acc_ref[...]


def _moe(xn, comb, h, wg, wu, wd, layer):
    n, d = h.shape
    _, ne, _, ff = wg.shape
    tm = _pick(n, 512, 16)
    return pl.pallas_call(
        _moe_kernel,
        grid=(n // tm, ne),
        in_specs=[pl.BlockSpec((tm, d), lambda i, e: (i, 0)),
                  pl.BlockSpec((tm, LANES), lambda i, e: (i, 0)),
                  pl.BlockSpec((tm, d), lambda i, e: (i, 0)),
                  pl.BlockSpec((None, 1, d, ff), lambda i, e: (layer, e, 0, 0)),
                  pl.BlockSpec((None, 1, d, ff), lambda i, e: (layer, e, 0, 0)),
                  pl.BlockSpec((None, 1, ff, d), lambda i, e: (layer, e, 0, 0))],
        out_specs=pl.BlockSpec((tm, d), lambda i, e: (i, 0)),
        out_shape=jax.ShapeDtypeStruct((n, d), F32),
        scratch_shapes=[pltpu.VMEM((tm, d), F32)],
        compiler_params=_params("arbitrary", "arbitrary"),
        name="moe",
    )(xn, comb, h, wg, wu, wd)


def _final_norm_kernel(h_ref, g_ref, o_ref):
    x = h_ref[...]
    ms = jnp.mean(x * x, axis=-1, keepdims=True)
    o_ref[...] = x * lax.rsqrt(ms + RMS_EPS) * g_ref[...]


def _final_norm(h, g):
    n, d = h.shape
    tm = _pick(n, 512, 8)
    return pl.pallas_call(
        _final_norm_kernel,
        grid=(n // tm,),
        in_specs=[pl.BlockSpec((tm, d), lambda i: (i, 0)), pl.BlockSpec((1, d), lambda i: (0, 0))],
        out_specs=pl.BlockSpec((tm, d), lambda i: (i, 0)),
        out_shape=jax.ShapeDtypeStruct((n, d), F32),
        compiler_params=_params("arbitrary"),
        name="final_norm",
    )(h, g.reshape(1, d))


def _cum_prompt_kernel(m_ref, bf_ref, logf_ref, cum_ref):
    logf = _log_sigmoid(m_ref[:, 0:LANES] + bf_ref[...])
    logf_ref[...] = logf
    t = logf.shape[0]
    row = _iota(logf.shape, 0)
    c = logf
    s = 1
    while s < t:
        c = c + jnp.where(row >= s, pltpu.roll(c, s, axis=0), 0.0)
        s *= 2
    cum_ref[...] = c


def _cum_prompt(pf, bf, nb, t):
    return pl.pallas_call(
        _cum_prompt_kernel,
        grid=(nb,),
        in_specs=[pl.BlockSpec((t, MISC_W), lambda b: (b, C_MISC // MISC_W)),
                  pl.BlockSpec((1, LANES), lambda b: (0, 0))],
        out_specs=[pl.BlockSpec((t, LANES), lambda b: (b, 0)), pl.BlockSpec((t, LANES), lambda b: (b, 0))],
        out_shape=[jax.ShapeDtypeStruct((nb * t, LANES), F32), jax.ShapeDtypeStruct((nb * t, LANES), F32)],
        compiler_params=_params("arbitrary"),
        name="cum_prompt",
    )(pf, bf)


def _stack_pair(q_ref, p, lane):
    qp = q_ref[:, p * LANES:(p + 1) * LANES] * jnp.asarray(SCALE, BF16)
    zero = jnp.zeros_like(qp)
    return jnp.concatenate([jnp.where(lane < HEAD_DIM, qp, zero), jnp.where(lane >= HEAD_DIM, qp, zero)], axis=0)


def _unstack_pair(x, lane):
    return jnp.where(lane < HEAD_DIM, x[0:QB], x[QB:2 * QB])


def _fox_prompt_kernel(q_ref, k_ref, v_ref, cq_ref, ck_ref, o_ref):
    i = pl.program_id(1)
    lane = _iota((QB, LANES), 1)
    qpos2 = i * QB + jnp.bitwise_and(_iota((2 * QB, 1), 0), QB - 1)
    kl = _iota((1, KB), 1)
    npair = N_HEADS // 2
    q2 = [_stack_pair(q_ref, p, lane) for p in range(npair)]
    cq2 = [jnp.concatenate([cq_ref[:, 2 * p:2 * p + 1], cq_ref[:, 2 * p + 1:2 * p + 2]], axis=0) for p in range(npair)]

    def step(kb, carry, masked):
        k0 = pl.multiple_of(kb * KB, KB)
        out = []
        for p in range(npair):
            m, l, acc = carry[p]
            cols = slice(p * LANES, (p + 1) * LANES)
            s = _dot_nt(q2[p], k_ref[pl.ds(k0, KB), cols])
            ck2 = jnp.concatenate([jnp.broadcast_to(ck_ref[0, 2 * p, pl.ds(kb, 1), :], (QB, KB)),
                                   jnp.broadcast_to(ck_ref[0, 2 * p + 1, pl.ds(kb, 1), :], (QB, KB))], axis=0)
            s = s + cq2[p] - ck2
            if masked:
                s = jnp.where((k0 + kl) <= qpos2, s, NEG_INF)
            m_new = jnp.maximum(m, jnp.max(s, axis=-1, keepdims=True))
            pr = jnp.exp(s - m_new)
            alpha = jnp.exp(m - m_new)
            l = alpha * l + jnp.sum(pr, axis=-1, keepdims=True)
            acc = alpha * acc + _dot(pr.astype(BF16), v_ref[pl.ds(k0, KB), cols])
            out.append((m_new, l, acc))
        return tuple(out)

    init = tuple((jnp.full((2 * QB, 1), NEG_INF, F32), jnp.zeros((2 * QB, 1), F32), jnp.zeros((2 * QB, LANES), F32))
                 for _ in range(npair))
    last = (i * QB) // KB
    carry = lax.fori_loop(0, last, lambda kb, c: step(kb, c, False), init)
    carry = step(last, carry, True)
    for p in range(npair):
        m, l, acc = carry[p]
        o_ref[:, p * LANES:(p + 1) * LANES] = _unstack_pair(acc / jnp.maximum(l, TINY), lane).astype(BF16)


def _fox_prompt(pb, cum, cum_t, nb, t):
    nq = t // QB
    return pl.pallas_call(
        _fox_prompt_kernel,
        grid=(nb, nq),
        in_specs=[pl.BlockSpec((QB, HW), lambda b, i: (b * nq + i, C_QA // HW)),
                  pl.BlockSpec((t, HW), lambda b, i: (b, C_KA // HW)),
                  pl.BlockSpec((t, HW), lambda b, i: (b, C_VA // HW)),
                  pl.BlockSpec((QB, LANES), lambda b, i: (b * nq + i, 0)),
                  pl.BlockSpec((1, N_HEADS, t // KB, KB), lambda b, i: (b, 0, 0, 0))],
        out_specs=pl.BlockSpec((QB, HW), lambda b, i: (b * nq + i, 0)),
        out_shape=jax.ShapeDtypeStruct((nb * t, HW), BF16),
        compiler_params=_params("arbitrary", "arbitrary"),
        name="fox_prompt",
    )(pb, pb, pb, cum, cum_t)


def _sb_prompt_kernel(q_ref, k_ref, v_ref, o_ref):
    i = pl.program_id(1)
    lane = _iota((QB, LANES), 1)
    qpos2 = i * QB + jnp.bitwise_and(_iota((2 * QB, 1), 0), QB - 1)
    kl = _iota((1, KB), 1)
    tri = _tri(KB)
    npair = N_HEADS // 2
    q2 = [_stack_pair(q_ref, p, lane) for p in range(npair)]

    def step(kb, carry, masked):
        k0 = pl.multiple_of(kb * KB, KB)
        out = []
        for p in range(npair):
            r, acc = carry[p]
            cols = slice(p * LANES, (p + 1) * LANES)
            z = _dot_nt(q2[p], k_ref[pl.ds(k0, KB), cols])
            sp = _softplus(z)
            if masked:
                mask = (k0 + kl) < qpos2
                lk = jnp.where(mask, -sp, 0.0)
            else:
                lk = -sp
            a = jnp.exp(z - sp + _split_dot(lk, tri) + r)
            if masked:
                a = jnp.where(mask, a, 0.0)
            acc = acc + _dot(a.astype(BF16), v_ref[pl.ds(k0, KB), cols])
            r = r + jnp.sum(lk, axis=-1, keepdims=True)
            out.append((r, acc))
        return tuple(out)

    init = tuple((jnp.zeros((2 * QB, 1), F32), jnp.zeros((2 * QB, LANES), F32)) for _ in range(npair))
    last = (i * QB) // KB
    carry = step(last, init, True)
    carry = lax.fori_loop(0, last, lambda n, c: step(last - 1 - n, c, False), carry)
    for p in range(npair):
        o_ref[:, p * LANES:(p + 1) * LANES] = _unstack_pair(carry[p][1], lane).astype(BF16)


def _sb_prompt(pb, nb, t):
    nq = t // QB
    return pl.pallas_call(
        _sb_prompt_kernel,
        grid=(nb, nq),
        in_specs=[pl.BlockSpec((QB, HW), lambda b, i: (b * nq + i, C_QC // HW)),
                  pl.BlockSpec((t, HW), lambda b, i: (b, C_KC // HW)),
                  pl.BlockSpec((t, HW), lambda b, i: (b, C_VC // HW))],
        out_specs=pl.BlockSpec((QB, HW), lambda b, i: (b * nq + i, 0)),
        out_shape=jax.ShapeDtypeStruct((nb * t, HW), BF16),
        compiler_params=_params("arbitrary", "arbitrary"),
        name="sb_prompt",
    )(pb, pb, pb)


def _compress_chunks(ch_ref, w1_ref, pe_ref, w2_ref, o_ref):
    nch = ch_ref.shape[1]
    hw = NSA_G * CMP_HIDDEN
    for c in range(2):
        w1 = w1_ref[0, c]
        pe = _dot(pe_ref[0, c].astype(BF16), w1)
        y = _dot(ch_ref[c].astype(BF16), w1)
        out = None
        for g in range(NSA_G):
            lo = slice(g * CMP_HIDDEN, (g + 1) * CMP_HIDDEN)
            hi = slice(hw + g * CMP_HIDDEN, hw + (g + 1) * CMP_HIDDEN)
            pre = y[:, lo] + pltpu.roll(y[:, hi], nch - 1, axis=0) + pe[0:1, lo] + pe[1:2, hi]
            hid = 0.5 * pre * (1.0 + jnp.tanh(math.sqrt(2.0 / math.pi) * (pre + 0.044715 * pre * pre * pre)))
            cg = _dot(hid.astype(BF16), w2_ref[0, c, g])
            out = cg if out is None else out + cg
        o_ref[c, 0] = out.astype(BF16)


def _compress_weight_specs(layer, imap):
    return [pl.BlockSpec((1, 2, CHUNK_W, 2 * NSA_G * CMP_HIDDEN), imap((layer, 0, 0, 0))),
            pl.BlockSpec((1, 2, 8, CHUNK_W), imap((layer, 0, 0, 0))),
            pl.BlockSpec((1, 2, NSA_G, CMP_HIDDEN, LANES), imap((layer, 0, 0, 0, 0)))]


def _compress_prompt_kernel(xk_ref, xv_ref, w1_ref, pe_ref, w2_ref, o_ref, ch_ref):
    nch = ch_ref.shape[1]
    for c, x_ref in enumerate((xk_ref, xv_ref)):
        for r in range(CMP_STRIDE):
            ch_ref[c, :, r * LANES:(r + 1) * LANES] = x_ref[pl.ds(r, nch, stride=CMP_STRIDE), :]
    _compress_chunks(ch_ref, w1_ref, pe_ref, w2_ref, o_ref)


def _compress_prompt(pf, w1big, pe_rows, w2p, layer, nb, t):
    nch = t // CMP_STRIDE
    return pl.pallas_call(
        _compress_prompt_kernel,
        grid=(nb,),
        in_specs=[pl.BlockSpec((t, LANES), lambda b: (b, C_CMP // LANES)),
                  pl.BlockSpec((t, LANES), lambda b: (b, C_CMP // LANES + 1))]
        + _compress_weight_specs(layer, lambda idx: (lambda b: idx)),
        out_specs=pl.BlockSpec((2, 1, nch, LANES), lambda b: (0, b, 0, 0)),
        out_shape=jax.ShapeDtypeStruct((2, nb, nch, LANES), BF16),
        scratch_shapes=[pltpu.VMEM((2, nch, CHUNK_W), F32)],
        compiler_params=_params("arbitrary"),
        name="nsa_compress_prompt",
    )(pf, pf, w1big, pe_rows, w2p)


def _lut(tab, bk):
    parts = [jnp.take_along_axis(tab, bk[:, c:c + LANES], axis=1) for c in range(0, bk.shape[1], LANES)]
    return parts[0] if len(parts) == 1 else jnp.concatenate(parts, axis=1)


def _nsa_prompt_kernel(t5_ref, q_ref, ck_ref, cv_ref, sel_ref, win_ref, misc_ref, o_ref, *, n_s):
    i = pl.program_id(1)
    q0 = i * QB
    qpos = q0 + _iota((QB, 1), 0)
    lane = _iota((QB, LANES), 1)
    nch = ck_ref.shape[2]
    rows = N_HEADS * QB

    def per_head(f):
        return jnp.concatenate([f(h) for h in range(N_HEADS)], axis=0)

    def per_group(x0, x1):
        return jnp.concatenate([x0] * NSA_HG + [x1] * NSA_HG, axis=0)

    def bias_of(rel):
        bk = _bucket(rel)
        return per_head(lambda h: _lut(jnp.broadcast_to(t5_ref[h:h + 1, :], (QB, LANES)), bk))

    def q_head(head):
        g = head // NSA_HG
        x = q_ref[:, (head // 2) * LANES:(head // 2 + 1) * LANES].astype(F32) * SCALE
        if head % 2 != g:
            x = pltpu.roll(x, HEAD_DIM, axis=1)
        return jnp.where((lane >= HEAD_DIM) == (g == 1), x, 0.0).astype(BF16)

    q8 = per_head(q_head)
    rel_d = _iota((QB, QB), 0) - _iota((QB, QB), 1)
    bias_d = bias_of(rel_d)
    bias_p = bias_of(rel_d + QB)
    bias_far = per_head(lambda h: jnp.broadcast_to(t5_ref[h:h + 1, N_BUCKETS - 1:N_BUCKETS], (QB, 1)))

    ncol = _iota((QB, nch), 1)
    rel_c = qpos - (ncol * CMP_STRIDE + (CMP_BLOCK - 1))
    vis_c = jnp.where((rel_c >= 0) & (ncol < nch - 1), 1.0, 0.0)
    pc = _masked_softmax(_dot_nt(q8, ck_ref[0, 0]) + bias_of(rel_c), per_group(vis_c, vis_c) > 0.5)
    o_cmp = _dot(pc.astype(BF16), cv_ref[0, 0])
    psums = []
    for g in range(NSA_G):
        ps = pc[g * NSA_HG * QB:(g * NSA_HG + 1) * QB]
        for hg in range(1, NSA_HG):
            ps = ps + pc[(g * NSA_HG + hg) * QB:(g * NSA_HG + hg + 1) * QB]
        psums.append(ps)
    selm = _select_blocks(_split_dot(jnp.concatenate(psums, axis=0), _overlap(nch)),
                          jnp.concatenate([qpos] * NSA_G, axis=0), n_s).astype(BF16)

    def attend(kv_ref, k0, width, vis, bias, carry, real=None):
        m, l, acc = carry
        k0 = pl.multiple_of(k0, QB)
        kk = kv_ref[pl.ds(k0, width), 0:LANES]
        vv = kv_ref[pl.ds(k0, width), LANES:2 * LANES]
        if real is not None:
            kk = jnp.where(real, kk, jnp.zeros_like(kk))
            vv = jnp.where(real, vv, jnp.zeros_like(vv))
        mask = vis > 0.5
        s = jnp.where(mask, _dot_nt(q8, kk) + bias, NEG_INF)
        m_new = jnp.maximum(m, jnp.max(s, axis=-1, keepdims=True))
        pr = jnp.where(mask, jnp.exp(s - m_new), 0.0)
        alpha = jnp.exp(m - m_new)
        l = alpha * l + jnp.sum(pr, axis=-1, keepdims=True)
        acc = alpha * acc + _dot(pr.astype(BF16), vv)
        return m_new, l, acc

    def init():
        return (jnp.full((rows, 1), NEG_INF, F32), jnp.zeros((rows, 1), F32), jnp.zeros((rows, LANES), F32))

    kl = _iota((1, QB), 1)
    carry = init()
    for dlt in range(WINDOW // QB + 1):
        rel = qpos - ((i - dlt) * QB + kl)
        vis = jnp.where((rel >= 0) & (rel < WINDOW), 1.0, 0.0)
        bias = bias_d if dlt == 0 else (bias_p if dlt == 1 else bias_far)
        carry = attend(win_ref, jnp.maximum(i - dlt, 0) * QB, QB, per_group(vis, vis), bias, carry,
                       real=None if dlt == 0 else i - dlt >= 0)
    o_win = carry[2] / jnp.maximum(carry[1], TINY)

    def sel_vis(k0, width, causal, valid=None):
        blk = jnp.right_shift(k0 + _iota((LANES, width), 1), 6)
        expand = jnp.where(_iota((LANES, width), 0) == blk, 1.0, 0.0).astype(BF16)
        picked = _dot(selm, expand)
        if causal:
            picked = picked * jnp.concatenate([jnp.where((k0 + _iota((1, width), 1)) <= qpos, 1.0, 0.0)] * NSA_G, axis=0)
        if valid is not None:
            picked = picked * jnp.where(valid, 1.0, 0.0)
        return per_group(picked[0:QB], picked[QB:2 * QB])

    carry = attend(sel_ref, q0, QB, sel_vis(q0, QB, True), bias_d, init())
    k_prev = jnp.maximum(q0 - QB, 0)
    carry = attend(sel_ref, k_prev, QB, sel_vis(k_prev, QB, False, i >= 1), bias_p, carry)
    k_odd = jnp.maximum(q0 - 2 * QB, 0)
    carry = attend(sel_ref, k_odd, QB, sel_vis(k_odd, QB, False, (i >= 2) & (jnp.bitwise_and(i, 1) == 0)), bias_far, carry)
    carry = lax.fori_loop(0, jnp.right_shift(jnp.maximum(i - 1, 0), 1),
                          lambda kb, c: attend(sel_ref, kb * KB, KB, sel_vis(kb * KB, KB, False), bias_far, c), carry)
    o_sel = carry[2] / jnp.maximum(carry[1], TINY)

    sig = _sigmoid(misc_ref[:, 0:LANES])

    def gate(br):
        return per_head(lambda h: sig[:, 8 + br * N_HEADS + h:9 + br * N_HEADS + h])

    o8 = gate(0) * o_cmp + gate(1) * o_sel + gate(2) * o_win
    pieces = []
    for head in range(N_HEADS):
        x = o8[head * QB:(head + 1) * QB]
        if head % 2 != head // NSA_HG:
            x = pltpu.roll(x, HEAD_DIM, axis=1)
        pieces.append(x)
    for p in range(N_HEADS // 2):
        o_ref[:, p * LANES:(p + 1) * LANES] = jnp.where(lane < HEAD_DIM, pieces[2 * p], pieces[2 * p + 1]).astype(BF16)


def _nsa_prompt(pb, pf, comp, t5, nb, t):
    nq = t // QB
    nch = comp.shape[2]
    kernel = functools.partial(_nsa_prompt_kernel, n_s=-(-t // SEL_BLOCK))
    return pl.pallas_call(
        kernel,
        grid=(nb, nq),
        in_specs=[pl.BlockSpec((N_HEADS, LANES), lambda b, i: (0, 0)),
                  pl.BlockSpec((QB, HW), lambda b, i: (b * nq + i, C_QB // HW)),
                  pl.BlockSpec((1, 1, nch, LANES), lambda b, i: (0, b, 0, 0)),
                  pl.BlockSpec((1, 1, nch, LANES), lambda b, i: (1, b, 0, 0)),
                  pl.BlockSpec((t, 2 * LANES), lambda b, i: (b, C_SEL // (2 * LANES))),
                  pl.BlockSpec((t, 2 * LANES), lambda b, i: (b, C_WIN // (2 * LANES))),
                  pl.BlockSpec((QB, MISC_W), lambda b, i: (b * nq + i, C_MISC // MISC_W))],
        out_specs=pl.BlockSpec((QB, HW), lambda b, i: (b * nq + i, 0)),
        out_shape=jax.ShapeDtypeStruct((nb * t, HW), BF16),
        compiler_params=_params("arbitrary", "arbitrary"),
        name="nsa_prompt",
    )(t5, pb, comp, comp, pb, pb, pf)


def _page_specs(npg, block, layer, tail):
    def spec(j):
        return pl.BlockSpec(block, lambda b, pt: (pt[b * npg + j], layer) + tail)
    return [spec(j) for j in range(npg)]


def _cum_sample_kernel(pt_ref, *refs, npg):
    pages = refs[:npg]
    m_ref, bf_ref, cp_ref, logf_ref, cn_ref = refs[npg:]
    x = jnp.concatenate([pg[0, 0] for pg in pages], axis=0)
    lane = _iota(x.shape, 1)
    s = 1
    while s < PAGE:
        x = x + jnp.where(lane >= s, pltpu.roll(x, s, axis=1), 0.0)
        s *= 2
    off = jnp.zeros((N_HEADS, 1), F32)
    for j in range(npg):
        blk = x[j * N_HEADS:(j + 1) * N_HEADS]
        cp_ref[0, j] = blk + off
        off = off + blk[:, PAGE - 1:PAGE]
    eye = _iota((N_HEADS, LANES), 0) == _iota((N_HEADS, LANES), 1)
    tot = jnp.sum(jnp.where(eye, off, 0.0), axis=0, keepdims=True)
    logf = _log_sigmoid(m_ref[:, 0:LANES] + bf_ref[...])
    logf_ref[...] = logf
    row = _iota(logf.shape, 0)
    c = logf
    s = 1
    while s < DEC_T:
        c = c + jnp.where(row >= s, pltpu.roll(c, s, axis=0), 0.0)
        s *= 2
    cn_ref[...] = c + tot


def _cum_sample(pt, logf_view, pf, bf, layer, nbs, npg, row0):
    kernel = functools.partial(_cum_sample_kernel, npg=npg)
    grid_spec = pltpu.PrefetchScalarGridSpec(
        num_scalar_prefetch=1,
        grid=(nbs,),
        in_specs=_page_specs(npg, (1, 1, N_HEADS, PAGE), layer, (0, 0)) + [
            pl.BlockSpec((DEC_T, MISC_W), lambda b, pt: (row0 // DEC_T + b, C_MISC // MISC_W)),
            pl.BlockSpec((1, LANES), lambda b, pt: (0, 0))],
        out_specs=[pl.BlockSpec((1, npg, N_HEADS, PAGE), lambda b, pt: (b, 0, 0, 0)),
                   pl.BlockSpec((DEC_T, LANES), lambda b, pt: (b, 0)),
                   pl.BlockSpec((DEC_T, LANES), lambda b, pt: (b, 0))],
    )
    return pl.pallas_call(
        kernel,
        grid_spec=grid_spec,
        out_shape=[jax.ShapeDtypeStruct((nbs, npg, N_HEADS, PAGE), F32),
                   jax.ShapeDtypeStruct((nbs * DEC_T, LANES), F32),
                   jax.ShapeDtypeStruct((nbs * DEC_T, LANES), F32)],
        compiler_params=_params("arbitrary"),
        name="cum_sample",
    )(pt, *([logf_view] * npg), pf, bf)


def _block_diag_q(q):
    qt = jnp.concatenate([q] * N_HEADS, axis=0)
    same = jnp.right_shift(_iota(qt.shape, 1), 6) == jnp.right_shift(_iota(qt.shape, 0), 3)
    return jnp.where(same, qt * SCALE, 0.0).astype(BF16)


def _rows_per_head(x):
    return jnp.concatenate([jnp.broadcast_to(x[h:h + 1], (DEC_T, x.shape[1])) for h in range(N_HEADS)], axis=0)


def _col_per_head(x):
    return jnp.concatenate([x[:, h:h + 1] for h in range(N_HEADS)], axis=0)


def _own_head_lanes(acc):
    lane_h = jnp.right_shift(_iota((DEC_T, HW), 1), 6)
    out = jnp.zeros((DEC_T, HW), F32)
    for h in range(N_HEADS):
        out = jnp.where(lane_h == h, acc[h * DEC_T:(h + 1) * DEC_T], out)
    return out


def _pad_rows(x, n):
    return jnp.concatenate([x, jnp.zeros((n - x.shape[0], x.shape[1]), x.dtype)], axis=0)


def _new_key_mask(strict):
    shape = (N_HEADS * DEC_T, PAGE)
    t_row = jnp.bitwise_and(_iota(shape, 0), DEC_T - 1)
    return (_iota(shape, 1) < t_row) if strict else (_iota(shape, 1) <= t_row)


def _fox_sample_kernel(pt_ref, *refs, npg):
    pages = refs[:npg]
    q_ref, k_ref, v_ref, cp_ref, cn_ref, cnt_ref, o_ref = refs[npg:]
    qbd = _block_diag_q(q_ref[...])
    cq = _col_per_head(cn_ref[...])
    s_pages = []
    for j in range(npg):
        s = _dot(qbd, pages[j][0, 0, 0].astype(BF16))
        s_pages.append(s + cq - _rows_per_head(cp_ref[0, j]))
    k_new = _pad_rows(k_ref[...], PAGE).astype(BF16)
    s_new = _dot_nt(qbd, k_new) + cq - _rows_per_head(cnt_ref[0])
    mask_new = _new_key_mask(strict=False)
    s_new = jnp.where(mask_new, s_new, NEG_INF)
    m = jnp.max(s_new, axis=-1, keepdims=True)
    for s in s_pages:
        m = jnp.maximum(m, jnp.max(s, axis=-1, keepdims=True))
    p_new = jnp.where(mask_new, jnp.exp(s_new - m), 0.0)
    l = jnp.sum(p_new, axis=-1, keepdims=True)
    acc = _dot(p_new.astype(BF16), _pad_rows(v_ref[...], PAGE).astype(BF16))
    for j in range(npg):
        pr = jnp.exp(s_pages[j] - m)
        l = l + jnp.sum(pr, axis=-1, keepdims=True)
        acc = acc + _dot_nt(pr.astype(BF16), pages[j][0, 0, 1].astype(BF16))
    o_ref[...] = _own_head_lanes(acc / jnp.maximum(l, TINY))


def _fox_sample(pt, kv_view, pf, cum_past, cum_new, cum_new_t, layer, nbs, npg, row0):
    kernel = functools.partial(_fox_sample_kernel, npg=npg)
    rb = row0 // DEC_T
    grid_spec = pltpu.PrefetchScalarGridSpec(
        num_scalar_prefetch=1,
        grid=(nbs,),
        in_specs=_page_specs(npg, (1, 1, 2, HW, PAGE), layer, (0, 0, 0)) + [
            pl.BlockSpec((DEC_T, HW), lambda b, pt: (rb + b, C_QA // HW)),
            pl.BlockSpec((DEC_T, HW), lambda b, pt: (rb + b, C_KA // HW)),
            pl.BlockSpec((DEC_T, HW), lambda b, pt: (rb + b, C_VA // HW)),
            pl.BlockSpec((1, npg, N_HEADS, PAGE), lambda b, pt: (b, 0, 0, 0)),
            pl.BlockSpec((DEC_T, LANES), lambda b, pt: (b, 0)),
            pl.BlockSpec((1, N_HEADS, LANES), lambda b, pt: (b, 0, 0))],
        out_specs=pl.BlockSpec((DEC_T, HW), lambda b, pt: (b, 0)),
    )
    return pl.pallas_call(
        kernel,
        grid_spec=grid_spec,
        out_shape=jax.ShapeDtypeStruct((nbs * DEC_T, HW), F32),
        compiler_params=_params("arbitrary"),
        name="fox_sample",
    )(pt, *([kv_view] * npg), pf, pf, pf, cum_past, cum_new, cum_new_t)


def _sb_sample_kernel(pt_ref, *refs, npg):
    pages = refs[:npg]
    q_ref, k_ref, v_ref, o_ref = refs[npg:]
    qbd = _block_diag_q(q_ref[...])
    tri = _tri(PAGE)
    z = _dot_nt(qbd, _pad_rows(k_ref[...], PAGE).astype(BF16))
    mask = _new_key_mask(strict=True)
    sp = _softplus(z)
    lk = jnp.where(mask, -sp, 0.0)
    between = _split_dot(lk, tri)
    a = jnp.where(mask, jnp.exp(z - sp + between), 0.0)
    acc = _dot(a.astype(BF16), _pad_rows(v_ref[...], PAGE).astype(BF16))
    r = jnp.sum(lk, axis=-1, keepdims=True)
    rows = N_HEADS * DEC_T
    zs = [_dot(qbd, pages[j][0, 0, 0].astype(BF16)) for j in range(npg)]
    sps = [_softplus(z) for z in zs]
    within = _split_dot(jnp.concatenate([-sp for sp in sps], axis=0), tri)
    for j in reversed(range(npg)):
        a = jnp.exp(zs[j] - sps[j] + within[j * rows:(j + 1) * rows] + r)
        acc = acc + _dot_nt(a.astype(BF16), pages[j][0, 0, 1].astype(BF16))
        r = r - jnp.sum(sps[j], axis=-1, keepdims=True)
    o_ref[...] = _own_head_lanes(acc)


def _sb_sample(pt, kv_view, pf, layer, nbs, npg, row0):
    kernel = functools.partial(_sb_sample_kernel, npg=npg)
    rb = row0 // DEC_T
    grid_spec = pltpu.PrefetchScalarGridSpec(
        num_scalar_prefetch=1,
        grid=(nbs,),
        in_specs=_page_specs(npg, (1, 1, 2, HW, PAGE), layer, (0, 0, 0)) + [
            pl.BlockSpec((DEC_T, HW), lambda b, pt: (rb + b, C_QC // HW)),
            pl.BlockSpec((DEC_T, HW), lambda b, pt: (rb + b, C_KC // HW)),
            pl.BlockSpec((DEC_T, HW), lambda b, pt: (rb + b, C_VC // HW))],
        out_specs=pl.BlockSpec((DEC_T, HW), lambda b, pt: (b, 0)),
    )
    return pl.pallas_call(
        kernel,
        grid_spec=grid_spec,
        out_shape=jax.ShapeDtypeStruct((nbs * DEC_T, HW), F32),
        compiler_params=_params("arbitrary"),
        name="sb_sample",
    )(pt, *([kv_view] * npg), pf, pf, pf)


def _compress_sample_kernel(pt_ref, *refs, npg):
    pages = refs[:npg]
    w1_ref, pe_ref, w2_ref, o_ref, ch_ref, xp_ref = refs[npg:]
    per_page = PAGE // CMP_STRIDE
    for j in range(npg):
        for c in range(2):
            xp_ref[2 * j + c] = pages[j][0, 0, 0, c].T
    for j in range(npg):
        for c in range(2):
            for r in range(CMP_STRIDE):
                ch_ref[c, j * per_page:(j + 1) * per_page, r * LANES:(r + 1) * LANES] = (
                    xp_ref[2 * j + c, pl.ds(r, per_page, stride=CMP_STRIDE), :])
    _compress_chunks(ch_ref, w1_ref, pe_ref, w2_ref, o_ref)


def _compress_sample(pt, nsa_view, w1big, pe_rows, w2p, layer, nbs, npg):
    kernel = functools.partial(_compress_sample_kernel, npg=npg)
    nch = npg * PAGE // CMP_STRIDE
    grid_spec = pltpu.PrefetchScalarGridSpec(
        num_scalar_prefetch=1,
        grid=(nbs,),
        in_specs=_page_specs(npg, (1, 1, 1, 2, LANES, PAGE), layer, (0, 0, 0, 0))
        + _compress_weight_specs(layer, lambda idx: (lambda b, pt: idx)),
        out_specs=pl.BlockSpec((2, 1, nch, LANES), lambda b, pt: (0, b, 0, 0)),
        scratch_shapes=[pltpu.VMEM((2, nch, CHUNK_W), F32), pltpu.VMEM((2 * npg, PAGE, LANES), F32)],
    )
    return pl.pallas_call(
        kernel,
        grid_spec=grid_spec,
        out_shape=jax.ShapeDtypeStruct((2, nbs, nch, LANES), BF16),
        compiler_params=_params("arbitrary"),
        name="nsa_compress_sample",
    )(pt, *([nsa_view] * npg), w1big, pe_rows, w2p)


def _nsa_sample_kernel(pt_ref, *refs, npg, n_s):
    pages = refs[:npg]
    q_ref, seln_ref, winn_ref, misc_ref, ck_ref, cv_ref, wst_ref, t5_ref, exp_ref, o_ref = refs[npg:]
    past = npg * PAGE
    rows = N_HEADS * DEC_T
    nch = ck_ref.shape[2]
    wb = wst_ref.shape[4]
    lane8 = _iota((DEC_T, LANES), 1)
    t5c = t5_ref[...]

    def lut(rel):
        return _lut(t5c, _bucket(rel))

    def t_of(shape):
        return jnp.bitwise_and(_iota(shape, 0), DEC_T - 1)

    def g_rows(x):
        return jnp.concatenate([x] * NSA_HG, axis=0)

    q = q_ref[...]
    qrows = []
    for head in range(N_HEADS):
        g = head // NSA_HG
        x = q[:, (head // 2) * LANES:(head // 2 + 1) * LANES] * SCALE
        if head % 2 != g:
            x = pltpu.roll(x, HEAD_DIM, axis=1)
        qrows.append(jnp.where((lane8 >= HEAD_DIM) == (g == 1), x, 0.0))
    qbd = jnp.concatenate(qrows, axis=0).astype(BF16)
    bias_far = t5c[:, N_BUCKETS - 1:N_BUCKETS]

    qpos_c = past + t_of((rows, nch))
    rel_c = qpos_c - (_iota((rows, nch), 1) * CMP_STRIDE + (CMP_BLOCK - 1))
    mask_c = (rel_c >= 0) & (_iota((rows, nch), 1) < nch - 1)
    pc = _masked_softmax(_dot_nt(qbd, ck_ref[0, 0]) + lut(rel_c), mask_c)
    o_cmp = _dot(pc.astype(BF16), cv_ref[0, 0])
    ov = _overlap(nch)
    psums = []
    for g in range(NSA_G):
        base = g * NSA_HG * DEC_T
        psum = pc[base:base + DEC_T]
        for hg in range(1, NSA_HG):
            psum = psum + pc[base + hg * DEC_T:base + (hg + 1) * DEC_T]
        psums.append(psum)
    qpos_g = past + jnp.bitwise_and(_iota((NSA_G * DEC_T, 1), 0), DEC_T - 1)
    picked = _select_blocks(_split_dot(jnp.concatenate(psums, axis=0), ov), qpos_g, n_s)
    picked = jnp.concatenate([g_rows(picked[g * DEC_T:(g + 1) * DEC_T]) for g in range(NSA_G)], axis=0)

    sel_past = _dot(picked.astype(BF16), exp_ref[...])
    last_blk = past // SEL_BLOCK
    sel_new = jnp.sum(jnp.where(_iota((rows, LANES), 1) == last_blk, picked, 0.0), axis=-1, keepdims=True) > 0.5
    rel_new = t_of((rows, PAGE)) - _iota((rows, PAGE), 1)
    in_new = _iota((rows, PAGE), 1) < DEC_T
    bias_new = lut(rel_new)
    s_pages, m_pages = [], []
    for j in range(npg):
        s = _dot(qbd, pages[j][0, 0, 0, 0].astype(BF16))
        if past - (j + 1) * PAGE + 1 >= MAX_DISTANCE:
            s = s + bias_far
        else:
            s = s + lut(past + t_of((rows, PAGE)) - (j * PAGE + _iota((rows, PAGE), 1)))
        mk = sel_past[:, j * PAGE:(j + 1) * PAGE] > 0.5
        s_pages.append(jnp.where(mk, s, NEG_INF))
        m_pages.append(mk)
    mask_n = sel_new & (rel_new >= 0) & in_new
    seln = _pad_rows(seln_ref[...], PAGE).astype(BF16)
    s_new = jnp.where(mask_n, _dot_nt(qbd, seln[:, 0:LANES]) + bias_new, NEG_INF)
    m = jnp.max(s_new, axis=-1, keepdims=True)
    for s in s_pages:
        m = jnp.maximum(m, jnp.max(s, axis=-1, keepdims=True))
    p_new = jnp.where(mask_n, jnp.exp(s_new - m), 0.0)
    l = jnp.sum(p_new, axis=-1, keepdims=True)
    acc = _dot(p_new.astype(BF16), seln[:, LANES:2 * LANES])
    for j in range(npg):
        pr = jnp.where(m_pages[j], jnp.exp(s_pages[j] - m), 0.0)
        l = l + jnp.sum(pr, axis=-1, keepdims=True)
        acc = acc + _dot_nt(pr.astype(BF16), pages[j][0, 0, 0, 1].astype(BF16))
    o_sel = acc / jnp.maximum(l, TINY)

    rel_w = (wb + t_of((rows, wb))) - _iota((rows, wb), 1)
    mask_w = (rel_w >= 0) & (rel_w < WINDOW)
    s_w = jnp.where(mask_w, _dot(qbd, wst_ref[0, 0, 0].astype(BF16)) + lut(rel_w), NEG_INF)
    mask_wn = (rel_new >= 0) & (rel_new < WINDOW) & in_new
    winn = _pad_rows(winn_ref[...], PAGE).astype(BF16)
    s_wn = jnp.where(mask_wn, _dot_nt(qbd, winn[:, 0:LANES]) + bias_new, NEG_INF)
    m = jnp.maximum(jnp.max(s_w, axis=-1, keepdims=True), jnp.max(s_wn, axis=-1, keepdims=True))
    p_w = jnp.where(mask_w, jnp.exp(s_w - m), 0.0)
    p_wn = jnp.where(mask_wn, jnp.exp(s_wn - m), 0.0)
    l = jnp.sum(p_w, axis=-1, keepdims=True) + jnp.sum(p_wn, axis=-1, keepdims=True)
    acc = _dot_nt(p_w.astype(BF16), wst_ref[0, 0, 1].astype(BF16)) + _dot(p_wn.astype(BF16), winn[:, LANES:2 * LANES])
    o_win = acc / jnp.maximum(l, TINY)

    sig = _sigmoid(misc_ref[:, 0:LANES])

    def gate(br):
        return jnp.concatenate(
            [sig[:, 8 + br * N_HEADS + h:9 + br * N_HEADS + h] for h in range(N_HEADS)], axis=0)

    o_all = gate(0) * o_cmp + gate(1) * o_sel + gate(2) * o_win
    pieces = []
    for head in range(N_HEADS):
        x = o_all[head * DEC_T:(head + 1) * DEC_T]
        if head % 2 != head // NSA_HG:
            x = pltpu.roll(x, HEAD_DIM, axis=1)
        pieces.append(x)
    for p in range(N_HEADS // 2):
        o_ref[:, p * LANES:(p + 1) * LANES] = jnp.where(lane8 < HEAD_DIM, pieces[2 * p], pieces[2 * p + 1])


def _nsa_sample(pt, nsa_view, pf, comp, win_view, t5col, expand, layer, nbs, npg, row0):
    nch = comp.shape[2]
    wb = win_view.shape[4]
    past = npg * PAGE
    kernel = functools.partial(_nsa_sample_kernel, npg=npg, n_s=-(-(past + DEC_T) // SEL_BLOCK))
    rb = row0 // DEC_T
    grid_spec = pltpu.PrefetchScalarGridSpec(
        num_scalar_prefetch=1,
        grid=(nbs,),
        in_specs=_page_specs(npg, (1, 1, 1, 2, LANES, PAGE), layer, (1, 0, 0, 0)) + [
            pl.BlockSpec((DEC_T, HW), lambda b, pt: (rb + b, C_QB // HW)),
            pl.BlockSpec((DEC_T, 2 * LANES), lambda b, pt: (rb + b, C_SEL // (2 * LANES))),
            pl.BlockSpec((DEC_T, 2 * LANES), lambda b, pt: (rb + b, C_WIN // (2 * LANES))),
            pl.BlockSpec((DEC_T, MISC_W), lambda b, pt: (rb + b, C_MISC // MISC_W)),
            pl.BlockSpec((1, 1, nch, LANES), lambda b, pt: (0, b, 0, 0)),
            pl.BlockSpec((1, 1, nch, LANES), lambda b, pt: (1, b, 0, 0)),
            pl.BlockSpec((1, 1, 2, LANES, wb), lambda b, pt: (b, layer, 0, 0, 0)),
            pl.BlockSpec((N_HEADS * DEC_T, LANES), lambda b, pt: (0, 0)),
            pl.BlockSpec((LANES, past), lambda b, pt: (0, 0))],
        out_specs=pl.BlockSpec((DEC_T, HW), lambda b, pt: (b, 0)),
    )
    return pl.pallas_call(
        kernel,
        grid_spec=grid_spec,
        out_shape=jax.ShapeDtypeStruct((nbs * DEC_T, HW), F32),
        compiler_params=_params("arbitrary"),
        name="nsa_sample",
    )(pt, *([nsa_view] * npg), pf, pf, pf, pf, comp, comp, win_view, t5col, expand)


def _reorder_w_in(w_in, d):
    sizes = (HW, 2 * HW, N_HEADS, HW, 6 * NSA_G * HEAD_DIM, 3 * N_HEADS, HW, 2 * HW, 3 * d)
    offs = np.concatenate([[0], np.cumsum(sizes)])
    seg = [w_in[:, :, offs[i]:offs[i + 1]] for i in range(len(sizes))]
    q_a, kv_a, f_a, q_b, kv_b, g_b, q_c, kv_c, g_m = seg
    pad = jnp.zeros(w_in.shape[:2] + (MISC_W - N_HEADS - 3 * N_HEADS,), w_in.dtype)
    return jnp.concatenate([q_a, kv_a, q_b, q_c, kv_c, kv_b, f_a, g_b, pad, g_m], axis=-1).astype(BF16)


def kernel(x_prompt, x_sample, cache_fox_kv, cache_fox_logf, cache_nsa_kv, cache_sb_kv, state_nsa_win_kv, page_table, norm_mix_g, norm_ffn_g, norm_final_g, w_in, b_forget, t5_table, cmp_pe, cmp_w1, cmp_w2, w_out_a, w_out_b, w_out_c, w_out, router_group_w, router_group_b, router_expert_w, router_expert_b, expert_w_gate, expert_w_up, expert_w_down):
    nb, t, d = x_prompt.shape
    nbs, dec_t, _ = x_sample.shape
    depth = w_in.shape[0]
    npool = cache_fox_kv.shape[0]
    npg = page_table.shape[1]
    past = npg * PAGE
    wb = state_nsa_win_kv.shape[2]
    assert dec_t == DEC_T and t % QB == 0 and d % 128 == 0 and wb == WINDOW and past >= WINDOW
    n_p, n_s_rows = nb * t, nbs * DEC_T
    nq = t // QB

    fox_view = jnp.transpose(cache_fox_kv, (0, 1, 3, 4, 5, 2)).reshape(npool, depth, 2, HW, PAGE)
    sb_view = jnp.transpose(cache_sb_kv, (0, 1, 3, 4, 5, 2)).reshape(npool, depth, 2, HW, PAGE)
    nsa_view = jnp.transpose(cache_nsa_kv, (0, 1, 3, 4, 5, 6, 2)).reshape(npool, depth, 2, 2, LANES, PAGE)
    logf_view = jnp.transpose(cache_fox_logf, (0, 1, 3, 2))
    win_view = jnp.transpose(state_nsa_win_kv, (0, 1, 3, 4, 5, 2)).reshape(nbs, depth, 2, LANES, wb)
    pt = page_table.reshape(-1).astype(jnp.int32)

    w_in_r = _reorder_w_in(w_in, d)
    bf_pad = jnp.pad(b_forget.astype(F32), ((0, 0), (0, LANES - N_HEADS))).reshape(depth, 1, LANES)
    t5 = jnp.pad(t5_table.astype(F32).T, ((0, 0), (0, LANES - N_BUCKETS)))
    t5col = jnp.repeat(t5, DEC_T, axis=0)
    expand = jnp.asarray(np.arange(LANES)[:, None] == (np.arange(past)[None, :] // SEL_BLOCK), BF16)
    w1r = cmp_w1.reshape(depth, 2, 2, CMP_STRIDE, HEAD_DIM, CMP_HIDDEN)
    w1big = jnp.einsum("zchldk,gG->zclgdhGk", w1r, jnp.eye(NSA_G, dtype=w1r.dtype)).reshape(
        depth, 2, CHUNK_W, 2 * NSA_G * CMP_HIDDEN).astype(BF16)
    pe_rows = jnp.broadcast_to(cmp_pe.reshape(depth, 2, 2, CMP_STRIDE, 1, HEAD_DIM),
                               (depth, 2, 2, CMP_STRIDE, NSA_G, HEAD_DIM)).reshape(depth, 2, 2, CHUNK_W)
    pe_rows = jnp.pad(pe_rows, ((0, 0), (0, 0), (0, 6), (0, 0))).astype(F32)
    w2p = jnp.stack([jnp.pad(cmp_w2, ((0, 0), (0, 0), (0, 0), (g * HEAD_DIM, LANES - (g + 1) * HEAD_DIM)))
                     for g in range(NSA_G)], axis=2).astype(BF16)
    w_router = jnp.pad(jnp.concatenate([router_group_w, router_expert_w], axis=-1),
                       ((0, 0), (0, 0), (0, LANES - N_GROUPS - N_EXPERTS))).astype(F32)
    b_router = jnp.pad(jnp.concatenate([router_group_b, router_expert_b], axis=-1),
                       ((0, 0), (0, LANES - N_GROUPS - N_EXPERTS))).astype(F32).reshape(depth, 1, LANES)
    wa, wb_, wc, wo = (w.astype(BF16) for w in (w_out_a, w_out_b, w_out_c, w_out))
    wg, wu, wd = (w.astype(BF16) for w in (expert_w_gate, expert_w_up, expert_w_down))

    h = jnp.concatenate([x_prompt.reshape(n_p, d), x_sample.reshape(n_s_rows, d)], axis=0)
    st_p = [[] for _ in range(5)]
    st_s = [[] for _ in range(5)]
    for l in range(depth):
        pf, pb = _proj(h, norm_mix_g[l], w_in_r, l)

        logf_p, cum_p = _cum_prompt(pf, bf_pad[l], nb, t)
        cum_t = jnp.transpose(cum_p[:, :N_HEADS].reshape(nb, t, N_HEADS), (0, 2, 1)).reshape(nb, N_HEADS, t // KB, KB)
        o_a_p = _fox_prompt(pb, cum_p, cum_t, nb, t)
        o_c_p = _sb_prompt(pb, nb, t)
        comp_p = _compress_prompt(pf, w1big, pe_rows, w2p, l, nb, t)
        o_b_p = _nsa_prompt(pb, pf, comp_p, t5, nb, t)

        cum_past, logf_s, cum_new = _cum_sample(pt, logf_view, pf, bf_pad[l], l, nbs, npg, n_p)
        cum_new_t = jnp.pad(jnp.transpose(cum_new[:, :N_HEADS].reshape(nbs, DEC_T, N_HEADS), (0, 2, 1)),
                            ((0, 0), (0, 0), (0, LANES - DEC_T)))
        o_a_s = _fox_sample(pt, fox_view, pf, cum_past, cum_new, cum_new_t, l, nbs, npg, n_p)
        o_c_s = _sb_sample(pt, sb_view, pf, l, nbs, npg, n_p)
        comp_s = _compress_sample(pt, nsa_view, w1big, pe_rows, w2p, l, nbs, npg)
        o_b_s = _nsa_sample(pt, nsa_view, pf, comp_s, win_view, t5col, expand, l, nbs, npg, n_p)

        o_a = jnp.concatenate([o_a_p, o_a_s.astype(BF16)], axis=0)
        o_b = jnp.concatenate([o_b_p, o_b_s.astype(BF16)], axis=0)
        o_c = jnp.concatenate([o_c_p, o_c_s.astype(BF16)], axis=0)
        mixed = _merge(o_a, o_b, o_c, pf, wa, wb_, wc, d, l)
        h = _mm_res(mixed, wo, h, l)
        xn, comb = _router(h, norm_ffn_g[l], w_router[l], b_router[l])
        h = _moe(xn, comb, h, wg, wu, wd, l)

        def rows(c0, width, shape, lo, hi):
            return pf[lo:hi, c0:c0 + width].reshape(shape)

        win_new_p = rows(C_WIN, 2 * LANES, (nb, t, 2, NSA_G, HEAD_DIM), 0, n_p)
        win_new_s = rows(C_WIN, 2 * LANES, (nbs, DEC_T, 2, NSA_G, HEAD_DIM), n_p, n_p + n_s_rows)
        st_p[0].append(rows(C_KA, 2 * HW, (nb, t, 2, N_HEADS, HEAD_DIM), 0, n_p))
        st_p[1].append(logf_p[:, :N_HEADS].reshape(nb, t, N_HEADS))
        st_p[2].append(rows(C_CMP, 4 * LANES, (nb, t, 2, 2, NSA_G, HEAD_DIM), 0, n_p))
        st_p[3].append(rows(C_KC, 2 * HW, (nb, t, 2, N_HEADS, HEAD_DIM), 0, n_p))
        st_p[4].append(win_new_p[:, t - min(WINDOW, t):])
        st_s[0].append(rows(C_KA, 2 * HW, (nbs, DEC_T, 2, N_HEADS, HEAD_DIM), n_p, n_p + n_s_rows))
        st_s[1].append(logf_s[:, :N_HEADS].reshape(nbs, DEC_T, N_HEADS))
        st_s[2].append(rows(C_CMP, 4 * LANES, (nbs, DEC_T, 2, 2, NSA_G, HEAD_DIM), n_p, n_p + n_s_rows))
        st_s[3].append(rows(C_KC, 2 * HW, (nbs, DEC_T, 2, N_HEADS, HEAD_DIM), n_p, n_p + n_s_rows))
        wkv = jnp.concatenate([state_nsa_win_kv[:, l], win_new_s], axis=1)
        st_s[4].append(wkv[:, wb + DEC_T - min(WINDOW, wb + DEC_T):])

    y = _final_norm(h, norm_final_g)
    y_prompt = y[:n_p].reshape(nb, t, d)
    y_sample = y[n_p:].reshape(nbs, DEC_T, d)
    sp = [jnp.stack(s, axis=1) for s in st_p]
    ss = [jnp.stack(s, axis=1) for s in st_s]
    return (y_prompt, y_sample, sp[0], ss[0], sp[1], ss[1], sp[2], ss[2], sp[3], ss[3], sp[4], ss[4])
```

```python
import functools
import math

import numpy as np
import jax
import jax.numpy as jnp
from jax import lax
from jax.experimental import pallas as pl
from jax.experimental.pallas import tpu as pltpu

F32 = jnp.float32
BF16 = jnp.bfloat16

HEAD_DIM = 64
N_HEADS = 8
NSA_G = 2
NSA_HG = N_HEADS // NSA_G
HW = N_HEADS * HEAD_DIM
PAGE = 128
DEC_T = 8
CMP_BLOCK = 32
CMP_STRIDE = 16
CMP_HIDDEN = 128
SEL_BLOCK = 64
SEL_TOPK = 8
WINDOW = 512
N_BUCKETS = 32
MAX_DISTANCE = 128
N_GROUPS = 4
EXPERTS_PER_GROUP = 4
N_EXPERTS = N_GROUPS * EXPERTS_PER_GROUP
RMS_EPS = 1e-6
NEG_INF = -1e30
FORCE_SCORE = 1e9
BELOW_ALL = -3e38
SCALE = HEAD_DIM ** -0.5
TINY = float(np.finfo(np.float32).tiny)
QB = 128
KB = 256
LANES = 128
VMEM_LIMIT = 56 * 1024 * 1024

C_QA, C_KA, C_VA, C_QB, C_QC, C_KC, C_VC = 0, 512, 1024, 1536, 2048, 2560, 3072
C_CMP, C_SEL, C_WIN, C_MISC, C_GM = 3584, 3840, 4096, 4352, 4608
MISC_W = 256
CHUNK_W = CMP_STRIDE * LANES


def _params(*sem):
    return pltpu.CompilerParams(dimension_semantics=sem, vmem_limit_bytes=VMEM_LIMIT)


def _pick(n, cap, mult):
    t = (min(cap, n) // mult) * mult
    while t > 0 and n % t:
        t -= mult
    assert t > 0, (n, cap, mult)
    return t


def _iota(shape, axis):
    return lax.broadcasted_iota(jnp.int32, shape, axis)


def _dot(a, b):
    return jnp.dot(a, b, preferred_element_type=F32)


def _dot_nt(a, b):
    return lax.dot_general(a, b, (((1,), (1,)), ((), ())), preferred_element_type=F32)


def _softplus(x):
    return jnp.maximum(x, 0.0) + jnp.log(1.0 + jnp.exp(-jnp.abs(x)))


def _log_sigmoid(x):
    return jnp.minimum(x, 0.0) - jnp.log1p(jnp.exp(-jnp.abs(x)))


def _sigmoid(x):
    return 1.0 / (1.0 + jnp.exp(-x))


def _split_dot(x, w):
    hi = x.astype(BF16)
    lo = (x - hi.astype(F32)).astype(BF16)
    return _dot(hi, w) + _dot(lo, w)


def _tri(w):
    return jnp.where(_iota((w, w), 0) > _iota((w, w), 1), 1.0, 0.0).astype(BF16)


def _bucket(rel):
    n = jnp.maximum(rel, 0)
    exact = N_BUCKETS // 2
    far = jnp.log(jnp.maximum(n, exact).astype(F32) / exact) / math.log(MAX_DISTANCE / exact)
    far = exact + (far * (N_BUCKETS - exact)).astype(jnp.int32)
    return jnp.where(n < exact, n, jnp.minimum(far, N_BUCKETS - 1))


def _masked_softmax(logits, mask):
    l = jnp.where(mask, logits, NEG_INF)
    e = jnp.where(mask, jnp.exp(l - jnp.max(l, axis=-1, keepdims=True)), 0.0)
    return e / jnp.maximum(jnp.sum(e, axis=-1, keepdims=True), TINY)


def _overlap(nch):
    cs = _iota((nch, LANES), 0) * CMP_STRIDE
    ss = _iota((nch, LANES), 1) * SEL_BLOCK
    ov = jnp.maximum(jnp.minimum(cs + CMP_BLOCK, ss + SEL_BLOCK) - jnp.maximum(cs, ss), 0)
    return (ov.astype(F32) / CMP_BLOCK).astype(BF16)


def _select_blocks(score, qpos, n_s):
    lane = _iota(score.shape, 1)
    lane_f = lane.astype(F32)
    valid = lane * SEL_BLOCK <= qpos
    forced = (lane == jnp.right_shift(qpos, 6)) | (lane == 0)
    sc = jnp.where(forced, FORCE_SCORE, jnp.where(valid, score, NEG_INF))
    sc = jnp.where(lane < n_s, sc, BELOW_ALL)
    rank = jnp.zeros(score.shape, F32)
    for k in range(n_s):
        other = sc[:, k:k + 1]
        rank = rank + jnp.where((other > sc) | ((other == sc) & (lane > k)), 1.0, 0.0)
    return jnp.where((rank < min(SEL_TOPK, n_s)) & (lane < n_s), 1.0, 0.0)


def _proj_kernel(x_ref, g_ref, w_ref, of_ref, ob_ref, xn_ref):
    @pl.when(pl.program_id(1) == 0)
    def _():
        x = x_ref[...]
        ms = jnp.mean(x * x, axis=-1, keepdims=True)
        xn_ref[...] = (x * lax.rsqrt(ms + RMS_EPS) * g_ref[...]).astype(BF16)

    y = _dot(xn_ref[...], w_ref[...])
    of_ref[...] = y
    ob_ref[...] = y.astype(BF16)


def _proj(h, g, w, layer):
    n, d = h.shape
    wp = w.shape[2]
    tm = _pick(n, 512, 16)
    tn = _pick(wp, 1536, 128)
    return pl.pallas_call(
        _proj_kernel,
        grid=(n // tm, wp // tn),
        in_specs=[pl.BlockSpec((tm, d), lambda i, j: (i, 0)),
                  pl.BlockSpec((1, d), lambda i, j: (0, 0)),
                  pl.BlockSpec((None, d, tn), lambda i, j: (layer, 0, j))],
        out_specs=[pl.BlockSpec((tm, tn), lambda i, j: (i, j)),
                   pl.BlockSpec((tm, tn), lambda i, j: (i, j))],
        out_shape=[jax.ShapeDtypeStruct((n, wp), F32), jax.ShapeDtypeStruct((n, wp), BF16)],
        scratch_shapes=[pltpu.VMEM((tm, d), BF16)],
        compiler_params=_params("arbitrary", "arbitrary"),
        name="proj",
    )(h, g.reshape(1, d), w)


def _merge_kernel(oa_ref, ob_ref, oc_ref, wa_ref, wb_ref, wc_ref, g0_ref, g1_ref, g2_ref, o_ref):
    m = _sigmoid(g0_ref[...]) * _dot(oa_ref[...], wa_ref[...])
    m = m + _sigmoid(g1_ref[...]) * _dot(ob_ref[...], wb_ref[...])
    m = m + _sigmoid(g2_ref[...]) * _dot(oc_ref[...], wc_ref[...])
    o_ref[...] = m.astype(BF16)


def _merge(o_a, o_b, o_c, pf, wa, wb, wc, d, layer):
    n = o_a.shape[0]
    tm = _pick(n, 512, 16)
    tn = _pick(d, 512, 128)
    gm0 = C_GM // tn
    o_spec = pl.BlockSpec((tm, HW), lambda i, j: (i, 0))
    w_spec = pl.BlockSpec((None, HW, tn), lambda i, j: (layer, 0, j))

    def g_spec(k):
        return pl.BlockSpec((tm, tn), lambda i, j: (i, gm0 + k * (d // tn) + j))

    return pl.pallas_call(
        _merge_kernel,
        grid=(n // tm, d // tn),
        in_specs=[o_spec, o_spec, o_spec, w_spec, w_spec, w_spec, g_spec(0), g_spec(1), g_spec(2)],
        out_specs=pl.BlockSpec((tm, tn), lambda i, j: (i, j)),
        out_shape=jax.ShapeDtypeStruct((n, d), BF16),
        compiler_params=_params("arbitrary", "arbitrary"),
        name="merge",
    )(o_a, o_b, o_c, wa, wb, wc, pf, pf, pf)


def _mm_res_kernel(x_ref, w_ref, r_ref, o_ref):
    o_ref[...] = r_ref[...] + _dot(x_ref[...], w_ref[...])


def _mm_res(x, w, res, layer):
    n, k = x.shape
    d = w.shape[2]
    tm = _pick(n, 512, 16)
    tn = _pick(d, 512, 128)
    return pl.pallas_call(
        _mm_res_kernel,
        grid=(n // tm, d // tn),
        in_specs=[pl.BlockSpec((tm, k), lambda i, j: (i, 0)),
                  pl.BlockSpec((None, k, tn), lambda i, j: (layer, 0, j)),
                  pl.BlockSpec((tm, tn), lambda i, j: (i, j))],
        out_specs=pl.BlockSpec((tm, tn), lambda i, j: (i, j)),
        out_shape=jax.ShapeDtypeStruct((n, d), F32),
        compiler_params=_params("arbitrary", "arbitrary"),
        name="out_proj",
    )(x, w, res)


def _router_kernel(h_ref, g_ref, w_ref, b_ref, xn_ref, comb_ref):
    x = h_ref[...]
    ms = jnp.mean(x * x, axis=-1, keepdims=True)
    xn = x * lax.rsqrt(ms + RMS_EPS) * g_ref[...]
    xn_ref[...] = xn.astype(BF16)
    logits = jnp.dot(xn, w_ref[...], precision=lax.Precision.HIGHEST, preferred_element_type=F32) + b_ref[...]
    lane = _iota(logits.shape, 1)
    lane_f = lane.astype(F32)
    is_grp = lane < N_GROUPS
    gl = jnp.where(is_grp, logits, BELOW_ALL)
    gmax = jnp.max(gl, axis=-1, keepdims=True)
    gsum = jnp.sum(jnp.where(is_grp, jnp.exp(gl - gmax), 0.0), axis=-1, keepdims=True)
    w_grp = 1.0 / gsum
    g_star = jnp.min(jnp.where(is_grp & (gl == gmax), lane_f, 1e9), axis=-1, keepdims=True)
    lo = N_GROUPS + g_star * EXPERTS_PER_GROUP
    in_grp = (lane_f >= lo) & (lane_f < lo + EXPERTS_PER_GROUP)
    el = jnp.where(in_grp, logits, BELOW_ALL)
    v1 = jnp.max(el, axis=-1, keepdims=True)
    i1 = jnp.min(jnp.where(el == v1, lane_f, 1e9), axis=-1, keepdims=True)
    el2 = jnp.where(lane_f == i1, BELOW_ALL, el)
    v2 = jnp.max(el2, axis=-1, keepdims=True)
    i2 = jnp.min(jnp.where(el2 == v2, lane_f, 1e9), axis=-1, keepdims=True)
    e2 = jnp.exp(v2 - v1)
    den = 1.0 + e2
    comb = jnp.where(lane_f == i1, w_grp / den, 0.0) + jnp.where(lane_f == i2, w_grp * e2 / den, 0.0)
    comb_ref[...] = comb


def _router(h, g, w, b):
    n, d = h.shape
    tm = _pick(n, 512, 16)
    return pl.pallas_call(
        _router_kernel,
        grid=(n // tm,),
        in_specs=[pl.BlockSpec((tm, d), lambda i: (i, 0)),
                  pl.BlockSpec((1, d), lambda i: (0, 0)),
                  pl.BlockSpec((d, LANES), lambda i: (0, 0)),
                  pl.BlockSpec((1, LANES), lambda i: (0, 0))],
        out_specs=[pl.BlockSpec((tm, d), lambda i: (i, 0)),
                   pl.BlockSpec((tm, LANES), lambda i: (i, 0))],
        out_shape=[jax.ShapeDtypeStruct((n, d), BF16), jax.ShapeDtypeStruct((n, LANES), F32)],
        compiler_params=_params("arbitrary"),
        name="router",
    )(h, g.reshape(1, d), w, b)


def _moe_kernel(x_ref, c_ref, h_ref, wg_ref, wu_ref, wd_ref, o_ref, acc_ref):
    e = pl.program_id(1)

    @pl.when(e == 0)
    def _():
        acc_ref[...] = h_ref[...]

    x = x_ref[...]
    comb = c_ref[...]
    ce = jnp.sum(jnp.where(_iota(comb.shape, 1) == e + N_GROUPS, comb, 0.0), axis=-1, keepdims=True)
    gate = _dot(x, wg_ref[0])
    up = _dot(x, wu_ref[0])
    hh = gate * _sigmoid(gate) * up * ce
    acc_ref[...] += _dot(hh.astype(BF16), wd_ref[0])

    @pl.when(e == pl.num_programs(1) - 1)
    def _():
        o_ref[...] = acc_ref[...]


def _moe(xn, comb, h, wg, wu, wd, layer):
    n, d = h.shape
    _, ne, _, ff = wg.shape
    tm = _pick(n, 512, 16)
    return pl.pallas_call(
        _moe_kernel,
        grid=(n // tm, ne),
        in_specs=[pl.BlockSpec((tm, d), lambda i, e: (i, 0)),
                  pl.BlockSpec((tm, LANES), lambda i, e: (i, 0)),
                  pl.BlockSpec((tm, d), lambda i, e: (i, 0)),
                  pl.BlockSpec((None, 1, d, ff), lambda i, e: (layer, e, 0, 0)),
                  pl.BlockSpec((None, 1, d, ff), lambda i, e: (layer, e, 0, 0)),
                  pl.BlockSpec((None, 1, ff, d), lambda i, e: (layer, e, 0, 0))],
        out_specs=pl.BlockSpec((tm, d), lambda i, e: (i, 0)),
        out_shape=jax.ShapeDtypeStruct((n, d), F32),
        scratch_shapes=[pltpu.VMEM((tm, d), F32)],
        compiler_params=_params("arbitrary", "arbitrary"),
        name="moe",
    )(xn, comb, h, wg, wu, wd)


def _final_norm_kernel(h_ref, g_ref, o_ref):
    x = h_ref[...]
    ms = jnp.mean(x * x, axis=-1, keepdims=True)
    o_ref[...] = x * lax.rsqrt(ms + RMS_EPS) * g_ref[...]


def _final_norm(h, g):
    n, d = h.shape
    tm = _pick(n, 512, 8)
    return pl.pallas_call(
        _final_norm_kernel,
        grid=(n // tm,),
        in_specs=[pl.BlockSpec((tm, d), lambda i: (i, 0)), pl.BlockSpec((1, d), lambda i: (0, 0))],
        out_specs=pl.BlockSpec((tm, d), lambda i: (i, 0)),
        out_shape=jax.ShapeDtypeStruct((n, d), F32),
        compiler_params=_params("arbitrary"),
        name="final_norm",
    )(h, g.reshape(1, d))


def _cum_prompt_kernel(m_ref, bf_ref, logf_ref, cum_ref):
    logf = _log_sigmoid(m_ref[:, 0:LANES] + bf_ref[...])
    logf_ref[...] = logf
    t = logf.shape[0]
    row = _iota(logf.shape, 0)
    c = logf
    s = 1
    while s < t:
        c = c + jnp.where(row >= s, pltpu.roll(c, s, axis=0), 0.0)
        s *= 2
    cum_ref[...] = c


def _cum_prompt(pf, bf, nb, t):
    return pl.pallas_call(
        _cum_prompt_kernel,
        grid=(nb,),
        in_specs=[pl.BlockSpec((t, MISC_W), lambda b: (b, C_MISC // MISC_W)),
                  pl.BlockSpec((1, LANES), lambda b: (0, 0))],
        out_specs=[pl.BlockSpec((t, LANES), lambda b: (b, 0)), pl.BlockSpec((t, LANES), lambda b: (b, 0))],
        out_shape=[jax.ShapeDtypeStruct((nb * t, LANES), F32), jax.ShapeDtypeStruct((nb * t, LANES), F32)],
        compiler_params=_params("arbitrary"),
        name="cum_prompt",
    )(pf, bf)


def _stack_pair(q_ref, p, lane):
    qp = q_ref[:, p * LANES:(p + 1) * LANES] * jnp.asarray(SCALE, BF16)
    zero = jnp.zeros_like(qp)
    return jnp.concatenate([jnp.where(lane < HEAD_DIM, qp, zero), jnp.where(lane >= HEAD_DIM, qp, zero)], axis=0)


def _compress_chunks(ch_ref, w1_ref, pe_ref, w2_ref, o_ref):
    nch = ch_ref.shape[1]
    hw = NSA_G * CMP_HIDDEN
    for c in range(2):
        w1 = w1_ref[0, c]
        pe = _dot(pe_ref[0, c].astype(BF16), w1)
        y = _dot(ch_ref[c].astype(BF16), w1)
        out = None
        for g in range(NSA_G):
            lo = slice(g * CMP_HIDDEN, (g + 1) * CMP_HIDDEN)
            hi = slice(hw + g * CMP_HIDDEN, hw + (g + 1) * CMP_HIDDEN)
            pre = y[:, lo] + pltpu.roll(y[:, hi], nch - 1, axis=0) + pe[0:1, lo] + pe[1:2, hi]
            hid = 0.5 * pre * (1.0 + jnp.tanh(math.sqrt(2.0 / math.pi) * (pre + 0.044715 * pre * pre * pre)))
            cg = _dot(hid.astype(BF16), w2_ref[0, c, g])
            out = cg if out is None else out + cg
        o_ref[c, 0] = out.astype(BF16)


def _compress_weight_specs(layer, imap):
    return [pl.BlockSpec((1, 2, CHUNK_W, 2 * NSA_G * CMP_HIDDEN), imap((layer, 0, 0, 0))),
            pl.BlockSpec((1, 2, 8, CHUNK_W), imap((layer, 0, 0, 0))),
            pl.BlockSpec((1, 2, NSA_G, CMP_HIDDEN, LANES), imap((layer, 0, 0, 0, 0)))]


def _compress_prompt_kernel(xk_ref, xv_ref, w1_ref, pe_ref, w2_ref, o_ref, ch_ref):
    nch = ch_ref.shape[1]
    for c, x_ref in enumerate((xk_ref, xv_ref)):
        for r in range(CMP_STRIDE):
            ch_ref[c, :, r * LANES:(r + 1) * LANES] = x_ref[pl.ds(r, nch, stride=CMP_STRIDE), :]
    _compress_chunks(ch_ref, w1_ref, pe_ref, w2_ref, o_ref)


def _compress_prompt(pf, w1big, pe_rows, w2p, layer, nb, t):
    nch = t // CMP_STRIDE
    return pl.pallas_call(
        _compress_prompt_kernel,
        grid=(nb,),
        in_specs=[pl.BlockSpec((t, LANES), lambda b: (b, C_CMP // LANES)),
                  pl.BlockSpec((t, LANES), lambda b: (b, C_CMP // LANES + 1))]
        + _compress_weight_specs(layer, lambda idx: (lambda b: idx)),
        out_specs=pl.BlockSpec((2, 1, nch, LANES), lambda b: (0, b, 0, 0)),
        out_shape=jax.ShapeDtypeStruct((2, nb, nch, LANES), BF16),
        scratch_shapes=[pltpu.VMEM((2, nch, CHUNK_W), F32)],
        compiler_params=_params("arbitrary"),
        name="nsa_compress_prompt",
    )(pf, pf, w1big, pe_rows, w2p)


def _lut(tab, bk):
    parts = [jnp.take_along_axis(tab, bk[:, c:c + LANES], axis=1) for c in range(0, bk.shape[1], LANES)]
    return parts[0] if len(parts) == 1 else jnp.concatenate(parts, axis=1)


def _transpose_bf16(x):
    return x.astype(F32).T.astype(BF16)


def _pair_out(o_t, lane_dtype=BF16):
    own = jnp.concatenate([o_t[0:HEAD_DIM, 0:QB], o_t[HEAD_DIM:LANES, QB:2 * QB]], axis=0)
    return own.T.astype(lane_dtype)


def _fox_t_kernel(q_ref, k_ref, v_ref, cum_ref, cq_ref, o_ref, vt_ref, ckb_ref):
    i = pl.program_id(1)
    npair = N_HEADS // 2
    nkb = vt_ref.shape[0]

    @pl.when(i == 0)
    def _():
        for kb in range(nkb):
            for p in range(npair):
                vt_ref[kb, p * LANES:(p + 1) * LANES, :] = _transpose_bf16(v_ref[kb * KB:(kb + 1) * KB, p * LANES:(p + 1) * LANES])
        for h in range(N_HEADS):
            ckb_ref[h] = jnp.broadcast_to(cum_ref[:, h:h + 1], ckb_ref.shape[1:])

    lane = _iota((QB, LANES), 1)
    qrow = i * QB + jnp.bitwise_and(_iota((1, 2 * QB), 1), QB - 1)
    krow = _iota((KB, 1), 0)
    q2 = [_stack_pair(q_ref, p, lane) for p in range(npair)]
    cq2 = [jnp.concatenate([cq_ref[0, 2 * p, pl.ds(i, 1), :], cq_ref[0, 2 * p + 1, pl.ds(i, 1), :]], axis=1)
           for p in range(npair)]

    def step(kb, carry, masked):
        k0 = pl.multiple_of(kb * KB, KB)
        scores = [_dot_nt(k_ref[pl.ds(k0, KB), p * LANES:(p + 1) * LANES], q2[p]) for p in range(npair)]
        probs, stats = [], []
        for p in range(npair):
            m, l, _ = carry[p]
            ck = jnp.concatenate([ckb_ref[2 * p, pl.ds(k0, KB), :], ckb_ref[2 * p + 1, pl.ds(k0, KB), :]], axis=1)
            s = scores[p] + cq2[p] - ck
            if masked:
                s = jnp.where((k0 + krow) <= qrow, s, NEG_INF)
            m_new = jnp.maximum(m, jnp.max(s, axis=0, keepdims=True))
            pr = jnp.exp(s - m_new)
            alpha = jnp.exp(m - m_new)
            probs.append(pr.astype(BF16))
            stats.append((m_new, alpha * l + jnp.sum(pr, axis=0, keepdims=True), alpha))
        out = []
        for p in range(npair):
            m_new, l, alpha = stats[p]
            acc = alpha * carry[p][2] + _dot(vt_ref[kb, p * LANES:(p + 1) * LANES, :], probs[p])
            out.append((m_new, l, acc))
        return tuple(out)

    init = tuple((jnp.full((1, 2 * QB), NEG_INF, F32), jnp.zeros((1, 2 * QB), F32), jnp.zeros((LANES, 2 * QB), F32))
                 for _ in range(npair))
    last = (i * QB) // KB
    carry = lax.fori_loop(0, last, lambda kb, c: step(kb, c, False), init)
    carry = step(last, carry, True)
    for p in range(npair):
        m, l, acc = carry[p]
        o_ref[:, p * LANES:(p + 1) * LANES] = _pair_out(acc / jnp.maximum(l, TINY))


def _fox_prompt_t(pb, cum, cum_t, nb, t):
    nq = t // QB
    return pl.pallas_call(
        _fox_t_kernel,
        grid=(nb, nq),
        in_specs=[pl.BlockSpec((QB, HW), lambda b, i: (b * nq + i, C_QA // HW)),
                  pl.BlockSpec((t, HW), lambda b, i: (b, C_KA // HW)),
                  pl.BlockSpec((t, HW), lambda b, i: (b, C_VA // HW)),
                  pl.BlockSpec((t, LANES), lambda b, i: (b, 0)),
                  pl.BlockSpec((1, N_HEADS, nq, QB), lambda b, i: (b, 0, 0, 0))],
        out_specs=pl.BlockSpec((QB, HW), lambda b, i: (b * nq + i, 0)),
        out_shape=jax.ShapeDtypeStruct((nb * t, HW), BF16),
        scratch_shapes=[pltpu.VMEM((t // KB, HW, KB), BF16), pltpu.VMEM((N_HEADS, t, LANES), F32)],
        compiler_params=_params("arbitrary", "arbitrary"),
        name="fox_prompt",
    )(pb, pb, pb, cum, cum_t)


def _sb_t_kernel(q_ref, k_ref, v_ref, o_ref, vt_ref):
    i = pl.program_id(1)
    npair = N_HEADS // 2
    nkb = vt_ref.shape[0]

    @pl.when(i == 0)
    def _():
        for kb in range(nkb):
            for p in range(npair):
                vt_ref[kb, p * LANES:(p + 1) * LANES, :] = _transpose_bf16(v_ref[kb * KB:(kb + 1) * KB, p * LANES:(p + 1) * LANES])

    lane = _iota((QB, LANES), 1)
    qrow = i * QB + jnp.bitwise_and(_iota((1, 2 * QB), 1), QB - 1)
    krow = _iota((KB, 1), 0)
    later = jnp.where(_iota((KB, KB), 1) > _iota((KB, KB), 0), 1.0, 0.0).astype(BF16)
    q2 = [_stack_pair(q_ref, p, lane) for p in range(npair)]

    def step(kb, carry, masked):
        k0 = pl.multiple_of(kb * KB, KB)
        vis = (k0 + krow) < qrow
        zs = [_dot_nt(k_ref[pl.ds(k0, KB), p * LANES:(p + 1) * LANES], q2[p]) for p in range(npair)]
        sps = [_softplus(z) for z in zs]
        lks = [jnp.where(vis, -sp, 0.0) if masked else -sp for sp in sps]
        his = [lk.astype(BF16) for lk in lks]
        los = [(lk - hi.astype(F32)).astype(BF16) for lk, hi in zip(lks, his)]
        betweens = [_dot(later, hi) + _dot(later, lo) for hi, lo in zip(his, los)]
        weights = []
        for p in range(npair):
            a = jnp.exp(zs[p] - sps[p] + betweens[p] + carry[p][0])
            weights.append((jnp.where(vis, a, 0.0) if masked else a).astype(BF16))
        out = []
        for p in range(npair):
            r, acc = carry[p]
            acc = acc + _dot(vt_ref[kb, p * LANES:(p + 1) * LANES, :], weights[p])
            out.append((r + jnp.sum(lks[p], axis=0, keepdims=True), acc))
        return tuple(out)

    init = tuple((jnp.zeros((1, 2 * QB), F32), jnp.zeros((LANES, 2 * QB), F32)) for _ in range(npair))
    last = (i * QB) // KB
    carry = step(last, init, True)
    carry = lax.fori_loop(0, last, lambda n, c: step(last - 1 - n, c, False), carry)
    for p in range(npair):
        o_ref[:, p * LANES:(p + 1) * LANES] = _pair_out(carry[p][1])


def _sb_prompt_t(pb, nb, t):
    nq = t // QB
    return pl.pallas_call(
        _sb_t_kernel,
        grid=(nb, nq),
        in_specs=[pl.BlockSpec((QB, HW), lambda b, i: (b * nq + i, C_QC // HW)),
                  pl.BlockSpec((t, HW), lambda b, i: (b, C_KC // HW)),
                  pl.BlockSpec((t, HW), lambda b, i: (b, C_VC // HW))],
        out_specs=pl.BlockSpec((QB, HW), lambda b, i: (b * nq + i, 0)),
        out_shape=jax.ShapeDtypeStruct((nb * t, HW), BF16),
        scratch_shapes=[pltpu.VMEM((t // KB, HW, KB), BF16)],
        compiler_params=_params("arbitrary", "arbitrary"),
        name="sb_prompt",
    )(pb, pb, pb)


def _select_blocks_t(score, qrow, n_s, sc_ref):
    blk = _iota(score.shape, 0)
    blk_f = blk.astype(F32)
    valid = blk * SEL_BLOCK <= qrow
    forced = (blk == jnp.right_shift(qrow, 6)) | (blk == 0)
    sc = jnp.where(forced, FORCE_SCORE, jnp.where(valid, score, NEG_INF))
    sc = jnp.where(blk < n_s, sc, BELOW_ALL)
    rows = min(score.shape[0], -(-n_s // 8) * 8)
    sc, blk = sc[0:rows], _iota((rows, score.shape[1]), 0)
    sc_ref[0:rows, :] = sc
    rank = jnp.zeros(sc.shape, F32)
    for k in range(n_s):
        other = sc_ref[k:k + 1, :]
        rank = rank + jnp.where((other > sc) | ((other == sc) & (blk > k)), 1.0, 0.0)
    sel = jnp.where((rank < min(SEL_TOPK, n_s)) & (blk < n_s), 1.0, 0.0)
    if rows < score.shape[0]:
        sel = jnp.concatenate([sel, jnp.zeros((score.shape[0] - rows, score.shape[1]), F32)], axis=0)
    return sel


def _nsa_t_kernel(t5_ref, q_ref, ck_ref, cv_ref, sel_ref, win_ref, misc_ref, o_ref, svt_ref, wvt_ref, sc_ref, *, n_s):
    i = pl.program_id(1)
    q0 = i * QB
    nch = ck_ref.shape[2]
    nblk = svt_ref.shape[0]
    cols = N_HEADS * QB

    @pl.when(i == 0)
    def _():
        for kb in range(nblk):
            svt_ref[kb] = _transpose_bf16(sel_ref[kb * QB:(kb + 1) * QB, LANES:2 * LANES])
            wvt_ref[kb] = _transpose_bf16(win_ref[kb * QB:(kb + 1) * QB, LANES:2 * LANES])

    lane = _iota((QB, LANES), 1)
    qrow = q0 + _iota((1, QB), 1)

    def per_head(f):
        return jnp.concatenate([f(h) for h in range(N_HEADS)], axis=1)

    def per_group(x0, x1):
        return jnp.concatenate([x0] * NSA_HG + [x1] * NSA_HG, axis=1)

    def bias_of(rel):
        bk = _bucket(rel)
        return per_head(lambda h: _lut(jnp.broadcast_to(t5_ref[h:h + 1, :], (rel.shape[0], LANES)), bk))

    def q_head(head):
        g = head // NSA_HG
        x = q_ref[:, (head // 2) * LANES:(head // 2 + 1) * LANES].astype(F32) * SCALE
        if head % 2 != g:
            x = pltpu.roll(x, HEAD_DIM, axis=1)
        return jnp.where((lane >= HEAD_DIM) == (g == 1), x, 0.0).astype(BF16)

    q8 = jnp.concatenate([q_head(h) for h in range(N_HEADS)], axis=0)
    rel_d = _iota((QB, QB), 1) - _iota((QB, QB), 0)
    bias_d = bias_of(rel_d)
    bias_p = bias_of(rel_d + QB)
    bias_far = per_head(lambda h: jnp.broadcast_to(t5_ref[h:h + 1, N_BUCKETS - 1:N_BUCKETS], (1, QB)))

    nrow = _iota((nch, QB), 0)
    rel_c = qrow - (nrow * CMP_STRIDE + (CMP_BLOCK - 1))
    vis_c = jnp.where((rel_c >= 0) & (nrow < nch - 1), 1.0, 0.0)
    mask_c = per_group(vis_c, vis_c) > 0.5
    lc = jnp.where(mask_c, _dot_nt(ck_ref[0, 0], q8) + bias_of(rel_c), NEG_INF)
    ec = jnp.where(mask_c, jnp.exp(lc - jnp.max(lc, axis=0, keepdims=True)), 0.0)
    pc = ec / jnp.maximum(jnp.sum(ec, axis=0, keepdims=True), TINY)
    o_cmp = _dot(_transpose_bf16(cv_ref[0, 0]), pc.astype(BF16))
    psums = []
    for g in range(NSA_G):
        ps = pc[:, g * NSA_HG * QB:(g * NSA_HG + 1) * QB]
        for hg in range(1, NSA_HG):
            ps = ps + pc[:, (g * NSA_HG + hg) * QB:(g * NSA_HG + hg + 1) * QB]
        psums.append(ps)
    psum = jnp.concatenate(psums, axis=1)
    cs = _iota((LANES, nch), 1) * CMP_STRIDE
    ss = _iota((LANES, nch), 0) * SEL_BLOCK
    ov_t = (jnp.maximum(jnp.minimum(cs + CMP_BLOCK, ss + SEL_BLOCK) - jnp.maximum(cs, ss), 0).astype(F32)
            / CMP_BLOCK).astype(BF16)
    p_hi = psum.astype(BF16)
    p_lo = (psum - p_hi.astype(F32)).astype(BF16)
    selm = _select_blocks_t(_dot(ov_t, p_hi) + _dot(ov_t, p_lo), jnp.concatenate([qrow] * NSA_G, axis=1),
                            n_s, sc_ref).astype(BF16)

    def attend(steps):
        loaded = []
        for k_ref, vt_ref, k0, width, _, _, _, real in steps:
            k0 = pl.multiple_of(k0, QB)
            kk = k_ref[pl.ds(k0, width), 0:LANES]
            kb = jnp.right_shift(k0, 7)
            vt = vt_ref[kb] if width == QB else jnp.concatenate([vt_ref[kb], vt_ref[kb + 1]], axis=1)
            if real is not None:
                kk = jnp.where(real, kk, jnp.zeros_like(kk))
                vt = jnp.where(real, vt, jnp.zeros_like(vt))
            loaded.append((kk, vt))
        scores = [_dot_nt(kk, q8) for kk, _ in loaded]
        soft = []
        for (_, _, _, _, hide, bias, (m, l, _), _), s in zip(steps, scores):
            s = s + (bias + hide)
            m_new = jnp.maximum(m, jnp.max(s, axis=0, keepdims=True))
            pr = jnp.exp(s - m_new)
            alpha = jnp.exp(m - m_new)
            soft.append((m_new, alpha * l + jnp.sum(pr, axis=0, keepdims=True), alpha, pr.astype(BF16)))
        return [(m_new, l, alpha * step[6][2] + _dot(vt, pr))
                for step, (_, vt), (m_new, l, alpha, pr) in zip(steps, loaded, soft)]

    def init():
        return (jnp.full((1, cols), NEG_INF, F32), jnp.zeros((1, cols), F32), jnp.zeros((LANES, cols), F32))

    def win_step(dlt, carry):
        rel = qrow - ((i - dlt) * QB + _iota((QB, 1), 0))
        hide = jnp.where((rel >= 0) & (rel < WINDOW), 0.0, NEG_INF)
        bias = bias_d if dlt == 0 else (bias_p if dlt == 1 else bias_far)
        return (win_ref, wvt_ref, jnp.maximum(i - dlt, 0) * QB, QB, per_group(hide, hide), bias, carry,
                None if dlt == 0 else i - dlt >= 0)

    def sel_step(k0, width, bias, carry, causal=False, valid=None):
        blk = jnp.right_shift(k0 + _iota((width, LANES), 0), 6)
        expand = jnp.where(_iota((width, LANES), 1) == blk, 1.0, 0.0).astype(BF16)
        picked = _dot(expand, selm)
        if causal:
            picked = picked * jnp.concatenate([jnp.where((k0 + _iota((width, 1), 0)) <= qrow, 1.0, 0.0)] * NSA_G, axis=1)
        if valid is not None:
            picked = picked * jnp.where(valid, 1.0, 0.0)
        hide = jnp.where(picked > 0.5, 0.0, NEG_INF)
        return (sel_ref, svt_ref, k0, width, per_group(hide[:, 0:QB], hide[:, QB:2 * QB]), bias, carry, None)

    k_prev = jnp.maximum(q0 - QB, 0)
    k_odd = jnp.maximum(q0 - 2 * QB, 0)
    c_win, c_sel = attend([win_step(0, init()), sel_step(q0, QB, bias_d, init(), causal=True)])
    c_win, c_sel = attend([win_step(1, c_win), sel_step(k_prev, QB, bias_p, c_sel, valid=i >= 1)])
    c_win, c_sel = attend([win_step(2, c_win),
                           sel_step(k_odd, QB, bias_far, c_sel, valid=(i >= 2) & (jnp.bitwise_and(i, 1) == 0))])
    for dlt in range(3, WINDOW // QB + 1):
        (c_win,) = attend([win_step(dlt, c_win)])
    o_win = c_win[2] / jnp.maximum(c_win[1], TINY)
    c_sel = lax.fori_loop(0, jnp.right_shift(jnp.maximum(i - 1, 0), 1),
                          lambda kb, c: attend([sel_step(kb * KB, KB, bias_far, c)])[0], c_sel)
    o_sel = c_sel[2] / jnp.maximum(c_sel[1], TINY)

    sig_t = _sigmoid(misc_ref[:, 0:LANES]).T

    def gate(br):
        return per_head(lambda h: sig_t[8 + br * N_HEADS + h:9 + br * N_HEADS + h, :])

    o8 = gate(0) * o_cmp + gate(1) * o_sel + gate(2) * o_win
    for p in range(N_HEADS // 2):
        own = []
        for head in (2 * p, 2 * p + 1):
            g = head // NSA_HG
            own.append(o8[g * HEAD_DIM:(g + 1) * HEAD_DIM, head * QB:(head + 1) * QB])
        o_ref[:, p * LANES:(p + 1) * LANES] = jnp.concatenate(own, axis=0).T.astype(BF16)


def _nsa_prompt_t(pb, pf, comp, t5, nb, t):
    nq = t // QB
    nch = comp.shape[2]
    kernel = functools.partial(_nsa_t_kernel, n_s=-(-t // SEL_BLOCK))
    return pl.pallas_call(
        kernel,
        grid=(nb, nq),
        in_specs=[pl.BlockSpec((N_HEADS, LANES), lambda b, i: (0, 0)),
                  pl.BlockSpec((QB, HW), lambda b, i: (b * nq + i, C_QB // HW)),
                  pl.BlockSpec((1, 1, nch, LANES), lambda b, i: (0, b, 0, 0)),
                  pl.BlockSpec((1, 1, nch, LANES), lambda b, i: (1, b, 0, 0)),
                  pl.BlockSpec((t, 2 * LANES), lambda b, i: (b, C_SEL // (2 * LANES))),
                  pl.BlockSpec((t, 2 * LANES), lambda b, i: (b, C_WIN // (2 * LANES))),
                  pl.BlockSpec((QB, MISC_W), lambda b, i: (b * nq + i, C_MISC // MISC_W))],
        out_specs=pl.BlockSpec((QB, HW), lambda b, i: (b * nq + i, 0)),
        out_shape=jax.ShapeDtypeStruct((nb * t, HW), BF16),
        scratch_shapes=[pltpu.VMEM((nq, LANES, QB), BF16), pltpu.VMEM((nq, LANES, QB), BF16),
                        pltpu.VMEM((LANES, NSA_G * QB), F32)],
        compiler_params=_params("arbitrary", "arbitrary"),
        name="nsa_prompt",
    )(t5, pb, comp, comp, pb, pb, pf)


def _page_specs(npg, block, layer, tail):
    def spec(j):
        return pl.BlockSpec(block, lambda b, pt: (pt[b * npg + j], layer) + tail)
    return [spec(j) for j in range(npg)]


def _cum_sample_kernel(pt_ref, *refs, npg):
    pages = refs[:npg]
    m_ref, bf_ref, cp_ref, logf_ref, cn_ref = refs[npg:]
    x = jnp.concatenate([pg[0, 0] for pg in pages], axis=0)
    lane = _iota(x.shape, 1)
    s = 1
    while s < PAGE:
        x = x + jnp.where(lane >= s, pltpu.roll(x, s, axis=1), 0.0)
        s *= 2
    off = jnp.zeros((N_HEADS, 1), F32)
    for j in range(npg):
        blk = x[j * N_HEADS:(j + 1) * N_HEADS]
        cp_ref[0, j] = blk + off
        off = off + blk[:, PAGE - 1:PAGE]
    eye = _iota((N_HEADS, LANES), 0) == _iota((N_HEADS, LANES), 1)
    tot = jnp.sum(jnp.where(eye, off, 0.0), axis=0, keepdims=True)
    logf = _log_sigmoid(m_ref[:, 0:LANES] + bf_ref[...])
    logf_ref[...] = logf
    row = _iota(logf.shape, 0)
    c = logf
    s = 1
    while s < DEC_T:
        c = c + jnp.where(row >= s, pltpu.roll(c, s, axis=0), 0.0)
        s *= 2
    cn_ref[...] = c + tot


def _cum_sample(pt, logf_view, pf, bf, layer, nbs, npg, row0):
    kernel = functools.partial(_cum_sample_kernel, npg=npg)
    grid_spec = pltpu.PrefetchScalarGridSpec(
        num_scalar_prefetch=1,
        grid=(nbs,),
        in_specs=_page_specs(npg, (1, 1, N_HEADS, PAGE), layer, (0, 0)) + [
            pl.BlockSpec((DEC_T, MISC_W), lambda b, pt: (row0 // DEC_T + b, C_MISC // MISC_W)),
            pl.BlockSpec((1, LANES), lambda b, pt: (0, 0))],
        out_specs=[pl.BlockSpec((1, npg, N_HEADS, PAGE), lambda b, pt: (b, 0, 0, 0)),
                   pl.BlockSpec((DEC_T, LANES), lambda b, pt: (b, 0)),
                   pl.BlockSpec((DEC_T, LANES), lambda b, pt: (b, 0))],
    )
    return pl.pallas_call(
        kernel,
        grid_spec=grid_spec,
        out_shape=[jax.ShapeDtypeStruct((nbs, npg, N_HEADS, PAGE), F32),
                   jax.ShapeDtypeStruct((nbs * DEC_T, LANES), F32),
                   jax.ShapeDtypeStruct((nbs * DEC_T, LANES), F32)],
        compiler_params=_params("arbitrary"),
        name="cum_sample",
    )(pt, *([logf_view] * npg), pf, bf)


def _block_diag_q(q):
    qt = jnp.concatenate([q] * N_HEADS, axis=0)
    same = jnp.right_shift(_iota(qt.shape, 1), 6) == jnp.right_shift(_iota(qt.shape, 0), 3)
    return jnp.where(same, qt * SCALE, 0.0).astype(BF16)


def _rows_per_head(x):
    return jnp.concatenate([jnp.broadcast_to(x[h:h + 1], (DEC_T, x.shape[1])) for h in range(N_HEADS)], axis=0)


def _col_per_head(x):
    return jnp.concatenate([x[:, h:h + 1] for h in range(N_HEADS)], axis=0)


def _own_head_lanes(acc):
    lane_h = jnp.right_shift(_iota((DEC_T, HW), 1), 6)
    out = jnp.zeros((DEC_T, HW), F32)
    for h in range(N_HEADS):
        out = jnp.where(lane_h == h, acc[h * DEC_T:(h + 1) * DEC_T], out)
    return out


def _pad_rows(x, n):
    return jnp.concatenate([x, jnp.zeros((n - x.shape[0], x.shape[1]), x.dtype)], axis=0)


def _new_key_mask(strict):
    shape = (N_HEADS * DEC_T, PAGE)
    t_row = jnp.bitwise_and(_iota(shape, 0), DEC_T - 1)
    return (_iota(shape, 1) < t_row) if strict else (_iota(shape, 1) <= t_row)


def _fox_sample_kernel(pt_ref, *refs, npg):
    pages = refs[:npg]
    q_ref, k_ref, v_ref, cp_ref, cn_ref, cnt_ref, o_ref = refs[npg:]
    qbd = _block_diag_q(q_ref[...])
    cq = _col_per_head(cn_ref[...])
    s_pages = []
    for j in range(npg):
        s = _dot(qbd, pages[j][0, 0, 0].astype(BF16))
        s_pages.append(s + cq - _rows_per_head(cp_ref[0, j]))
    k_new = _pad_rows(k_ref[...], PAGE).astype(BF16)
    s_new = _dot_nt(qbd, k_new) + cq - _rows_per_head(cnt_ref[0])
    mask_new = _new_key_mask(strict=False)
    s_new = jnp.where(mask_new, s_new, NEG_INF)
    m = jnp.max(s_new, axis=-1, keepdims=True)
    for s in s_pages:
        m = jnp.maximum(m, jnp.max(s, axis=-1, keepdims=True))
    p_new = jnp.where(mask_new, jnp.exp(s_new - m), 0.0)
    l = jnp.sum(p_new, axis=-1, keepdims=True)
    acc = _dot(p_new.astype(BF16), _pad_rows(v_ref[...], PAGE).astype(BF16))
    for j in range(npg):
        pr = jnp.exp(s_pages[j] - m)
        l = l + jnp.sum(pr, axis=-1, keepdims=True)
        acc = acc + _dot_nt(pr.astype(BF16), pages[j][0, 0, 1].astype(BF16))
    o_ref[...] = _own_head_lanes(acc / jnp.maximum(l, TINY))


def _fox_sample(pt, kv_view, pf, cum_past, cum_new, cum_new_t, layer, nbs, npg, row0):
    kernel = functools.partial(_fox_sample_kernel, npg=npg)
    rb = row0 // DEC_T
    grid_spec = pltpu.PrefetchScalarGridSpec(
        num_scalar_prefetch=1,
        grid=(nbs,),
        in_specs=_page_specs(npg, (1, 1, 2, HW, PAGE), layer, (0, 0, 0)) + [
            pl.BlockSpec((DEC_T, HW), lambda b, pt: (rb + b, C_QA // HW)),
            pl.BlockSpec((DEC_T, HW), lambda b, pt: (rb + b, C_KA // HW)),
            pl.BlockSpec((DEC_T, HW), lambda b, pt: (rb + b, C_VA // HW)),
            pl.BlockSpec((1, npg, N_HEADS, PAGE), lambda b, pt: (b, 0, 0, 0)),
            pl.BlockSpec((DEC_T, LANES), lambda b, pt: (b, 0)),
            pl.BlockSpec((1, N_HEADS, LANES), lambda b, pt: (b, 0, 0))],
        out_specs=pl.BlockSpec((DEC_T, HW), lambda b, pt: (b, 0)),
    )
    return pl.pallas_call(
        kernel,
        grid_spec=grid_spec,
        out_shape=jax.ShapeDtypeStruct((nbs * DEC_T, HW), F32),
        compiler_params=_params("arbitrary"),
        name="fox_sample",
    )(pt, *([kv_view] * npg), pf, pf, pf, cum_past, cum_new, cum_new_t)


def _sb_sample_kernel(pt_ref, *refs, npg):
    pages = refs[:npg]
    q_ref, k_ref, v_ref, o_ref = refs[npg:]
    qbd = _block_diag_q(q_ref[...])
    tri = _tri(PAGE)
    z = _dot_nt(qbd, _pad_rows(k_ref[...], PAGE).astype(BF16))
    mask = _new_key_mask(strict=True)
    sp = _softplus(z)
    lk = jnp.where(mask, -sp, 0.0)
    between = _split_dot(lk, tri)
    a = jnp.where(mask, jnp.exp(z - sp + between), 0.0)
    acc = _dot(a.astype(BF16), _pad_rows(v_ref[...], PAGE).astype(BF16))
    r = jnp.sum(lk, axis=-1, keepdims=True)
    rows = N_HEADS * DEC_T
    zs = [_dot(qbd, pages[j][0, 0, 0].astype(BF16)) for j in range(npg)]
    sps = [_softplus(z) for z in zs]
    within = _split_dot(jnp.concatenate([-sp for sp in sps], axis=0), tri)
    for j in reversed(range(npg)):
        a = jnp.exp(zs[j] - sps[j] + within[j * rows:(j + 1) * rows] + r)
        acc = acc + _dot_nt(a.astype(BF16), pages[j][0, 0, 1].astype(BF16))
        r = r - jnp.sum(sps[j], axis=-1, keepdims=True)
    o_ref[...] = _own_head_lanes(acc)


def _sb_sample(pt, kv_view, pf, layer, nbs, npg, row0):
    kernel = functools.partial(_sb_sample_kernel, npg=npg)
    rb = row0 // DEC_T
    grid_spec = pltpu.PrefetchScalarGridSpec(
        num_scalar_prefetch=1,
        grid=(nbs,),
        in_specs=_page_specs(npg, (1, 1, 2, HW, PAGE), layer, (0, 0, 0)) + [
            pl.BlockSpec((DEC_T, HW), lambda b, pt: (rb + b, C_QC // HW)),
            pl.BlockSpec((DEC_T, HW), lambda b, pt: (rb + b, C_KC // HW)),
            pl.BlockSpec((DEC_T, HW), lambda b, pt: (rb + b, C_VC // HW))],
        out_specs=pl.BlockSpec((DEC_T, HW), lambda b, pt: (b, 0)),
    )
    return pl.pallas_call(
        kernel,
        grid_spec=grid_spec,
        out_shape=jax.ShapeDtypeStruct((nbs * DEC_T, HW), F32),
        compiler_params=_params("arbitrary"),
        name="sb_sample",
    )(pt, *([kv_view] * npg), pf, pf, pf)


def _compress_sample_kernel(pt_ref, *refs, npg):
    pages = refs[:npg]
    w1_ref, pe_ref, w2_ref, o_ref, ch_ref, xp_ref = refs[npg:]
    per_page = PAGE // CMP_STRIDE
    for j in range(npg):
        for c in range(2):
            xp_ref[2 * j + c] = pages[j][0, 0, 0, c].T
    for j in range(npg):
        for c in range(2):
            for r in range(CMP_STRIDE):
                ch_ref[c, j * per_page:(j + 1) * per_page, r * LANES:(r + 1) * LANES] = (
                    xp_ref[2 * j + c, pl.ds(r, per_page, stride=CMP_STRIDE), :])
    _compress_chunks(ch_ref, w1_ref, pe_ref, w2_ref, o_ref)


def _compress_sample(pt, nsa_view, w1big, pe_rows, w2p, layer, nbs, npg):
    kernel = functools.partial(_compress_sample_kernel, npg=npg)
    nch = npg * PAGE // CMP_STRIDE
    grid_spec = pltpu.PrefetchScalarGridSpec(
        num_scalar_prefetch=1,
        grid=(nbs,),
        in_specs=_page_specs(npg, (1, 1, 1, 2, LANES, PAGE), layer, (0, 0, 0, 0))
        + _compress_weight_specs(layer, lambda idx: (lambda b, pt: idx)),
        out_specs=pl.BlockSpec((2, 1, nch, LANES), lambda b, pt: (0, b, 0, 0)),
        scratch_shapes=[pltpu.VMEM((2, nch, CHUNK_W), F32), pltpu.VMEM((2 * npg, PAGE, LANES), F32)],
    )
    return pl.pallas_call(
        kernel,
        grid_spec=grid_spec,
        out_shape=jax.ShapeDtypeStruct((2, nbs, nch, LANES), BF16),
        compiler_params=_params("arbitrary"),
        name="nsa_compress_sample",
    )(pt, *([nsa_view] * npg), w1big, pe_rows, w2p)


def _nsa_sample_kernel(pt_ref, *refs, npg, n_s):
    pages = refs[:npg]
    q_ref, seln_ref, winn_ref, misc_ref, ck_ref, cv_ref, wst_ref, t5_ref, exp_ref, o_ref = refs[npg:]
    past = npg * PAGE
    rows = N_HEADS * DEC_T
    nch = ck_ref.shape[2]
    wb = wst_ref.shape[4]
    lane8 = _iota((DEC_T, LANES), 1)
    t5c = t5_ref[...]

    def lut(rel):
        return _lut(t5c, _bucket(rel))

    def t_of(shape):
        return jnp.bitwise_and(_iota(shape, 0), DEC_T - 1)

    def g_rows(x):
        return jnp.concatenate([x] * NSA_HG, axis=0)

    q = q_ref[...]
    qrows = []
    for head in range(N_HEADS):
        g = head // NSA_HG
        x = q[:, (head // 2) * LANES:(head // 2 + 1) * LANES] * SCALE
        if head % 2 != g:
            x = pltpu.roll(x, HEAD_DIM, axis=1)
        qrows.append(jnp.where((lane8 >= HEAD_DIM) == (g == 1), x, 0.0))
    qbd = jnp.concatenate(qrows, axis=0).astype(BF16)
    bias_far = t5c[:, N_BUCKETS - 1:N_BUCKETS]

    raw_c = _dot_nt(qbd, ck_ref[0, 0])
    raw_pages = [_dot(qbd, pages[j][0, 0, 0, 0].astype(BF16)) for j in range(npg)]
    seln = _pad_rows(seln_ref[...], PAGE).astype(BF16)
    winn = _pad_rows(winn_ref[...], PAGE).astype(BF16)
    raw_new = _dot_nt(qbd, seln[:, 0:LANES])
    raw_w = _dot(qbd, wst_ref[0, 0, 0].astype(BF16))
    raw_wn = _dot_nt(qbd, winn[:, 0:LANES])

    qpos_c = past + t_of((rows, nch))
    rel_c = qpos_c - (_iota((rows, nch), 1) * CMP_STRIDE + (CMP_BLOCK - 1))
    mask_c = (rel_c >= 0) & (_iota((rows, nch), 1) < nch - 1)
    pc = _masked_softmax(raw_c + lut(rel_c), mask_c)
    psums = []
    for g in range(NSA_G):
        base = g * NSA_HG * DEC_T
        psum = pc[base:base + DEC_T]
        for hg in range(1, NSA_HG):
            psum = psum + pc[base + hg * DEC_T:base + (hg + 1) * DEC_T]
        psums.append(psum)

    rel_new = t_of((rows, PAGE)) - _iota((rows, PAGE), 1)
    in_new = _iota((rows, PAGE), 1) < DEC_T
    bias_new = lut(rel_new)
    rel_w = (wb + t_of((rows, wb))) - _iota((rows, wb), 1)
    mask_w = (rel_w >= 0) & (rel_w < WINDOW)
    s_w = jnp.where(mask_w, raw_w + lut(rel_w), NEG_INF)
    mask_wn = (rel_new >= 0) & (rel_new < WINDOW) & in_new
    s_wn = jnp.where(mask_wn, raw_wn + bias_new, NEG_INF)
    m = jnp.maximum(jnp.max(s_w, axis=-1, keepdims=True), jnp.max(s_wn, axis=-1, keepdims=True))
    p_w = jnp.where(mask_w, jnp.exp(s_w - m), 0.0)
    p_wn = jnp.where(mask_wn, jnp.exp(s_wn - m), 0.0)
    l_w = jnp.sum(p_w, axis=-1, keepdims=True) + jnp.sum(p_wn, axis=-1, keepdims=True)

    score = _split_dot(jnp.concatenate(psums, axis=0), _overlap(nch))
    o_cmp = _dot(pc.astype(BF16), cv_ref[0, 0])
    acc_w = _dot_nt(p_w.astype(BF16), wst_ref[0, 0, 1].astype(BF16)) + _dot(p_wn.astype(BF16), winn[:, LANES:2 * LANES])
    o_win = acc_w / jnp.maximum(l_w, TINY)

    qpos_g = past + jnp.bitwise_and(_iota((NSA_G * DEC_T, 1), 0), DEC_T - 1)
    picked = _select_blocks(score, qpos_g, n_s)
    picked = jnp.concatenate([g_rows(picked[g * DEC_T:(g + 1) * DEC_T]) for g in range(NSA_G)], axis=0)
    sel_past = _dot(picked.astype(BF16), exp_ref[...])
    last_blk = past // SEL_BLOCK
    sel_new = jnp.sum(jnp.where(_iota((rows, LANES), 1) == last_blk, picked, 0.0), axis=-1, keepdims=True) > 0.5
    s_pages, m_pages = [], []
    for j in range(npg):
        if past - (j + 1) * PAGE + 1 >= MAX_DISTANCE:
            s = raw_pages[j] + bias_far
        else:
            s = raw_pages[j] + lut(past + t_of((rows, PAGE)) - (j * PAGE + _iota((rows, PAGE), 1)))
        mk = sel_past[:, j * PAGE:(j + 1) * PAGE] > 0.5
        s_pages.append(jnp.where(mk, s, NEG_INF))
        m_pages.append(mk)
    mask_n = sel_new & (rel_new >= 0) & in_new
    s_new = jnp.where(mask_n, raw_new + bias_new, NEG_INF)
    m = jnp.max(s_new, axis=-1, keepdims=True)
    for s in s_pages:
        m = jnp.maximum(m, jnp.max(s, axis=-1, keepdims=True))
    p_new = jnp.where(mask_n, jnp.exp(s_new - m), 0.0)
    l = jnp.sum(p_new, axis=-1, keepdims=True)
    probs = []
    for j in range(npg):
        pr = jnp.where(m_pages[j], jnp.exp(s_pages[j] - m), 0.0)
        l = l + jnp.sum(pr, axis=-1, keepdims=True)
        probs.append(pr.astype(BF16))
    acc = _dot(p_new.astype(BF16), seln[:, LANES:2 * LANES])
    for j in range(npg):
        acc = acc + _dot_nt(probs[j], pages[j][0, 0, 0, 1].astype(BF16))
    o_sel = acc / jnp.maximum(l, TINY)

    sig = _sigmoid(misc_ref[:, 0:LANES])

    def gate(br):
        return jnp.concatenate(
            [sig[:, 8 + br * N_HEADS + h:9 + br * N_HEADS + h] for h in range(N_HEADS)], axis=0)

    o_all = gate(0) * o_cmp + gate(1) * o_sel + gate(2) * o_win
    pieces = []
    for head in range(N_HEADS):
        x = o_all[head * DEC_T:(head + 1) * DEC_T]
        if head % 2 != head // NSA_HG:
            x = pltpu.roll(x, HEAD_DIM, axis=1)
        pieces.append(x)
    for p in range(N_HEADS // 2):
        o_ref[:, p * LANES:(p + 1) * LANES] = jnp.where(lane8 < HEAD_DIM, pieces[2 * p], pieces[2 * p + 1])


def _nsa_sample(pt, nsa_view, pf, comp, win_view, t5col, expand, layer, nbs, npg, row0):
    nch = comp.shape[2]
    wb = win_view.shape[4]
    past = npg * PAGE
    kernel = functools.partial(_nsa_sample_kernel, npg=npg, n_s=-(-(past + DEC_T) // SEL_BLOCK))
    rb = row0 // DEC_T
    grid_spec = pltpu.PrefetchScalarGridSpec(
        num_scalar_prefetch=1,
        grid=(nbs,),
        in_specs=_page_specs(npg, (1, 1, 1, 2, LANES, PAGE), layer, (1, 0, 0, 0)) + [
            pl.BlockSpec((DEC_T, HW), lambda b, pt: (rb + b, C_QB // HW)),
            pl.BlockSpec((DEC_T, 2 * LANES), lambda b, pt: (rb + b, C_SEL // (2 * LANES))),
            pl.BlockSpec((DEC_T, 2 * LANES), lambda b, pt: (rb + b, C_WIN // (2 * LANES))),
            pl.BlockSpec((DEC_T, MISC_W), lambda b, pt: (rb + b, C_MISC // MISC_W)),
            pl.BlockSpec((1, 1, nch, LANES), lambda b, pt: (0, b, 0, 0)),
            pl.BlockSpec((1, 1, nch, LANES), lambda b, pt: (1, b, 0, 0)),
            pl.BlockSpec((1, 1, 2, LANES, wb), lambda b, pt: (b, layer, 0, 0, 0)),
            pl.BlockSpec((N_HEADS * DEC_T, LANES), lambda b, pt: (0, 0)),
            pl.BlockSpec((LANES, past), lambda b, pt: (0, 0))],
        out_specs=pl.BlockSpec((DEC_T, HW), lambda b, pt: (b, 0)),
    )
    return pl.pallas_call(
        kernel,
        grid_spec=grid_spec,
        out_shape=jax.ShapeDtypeStruct((nbs * DEC_T, HW), F32),
        compiler_params=_params("arbitrary"),
        name="nsa_sample",
    )(pt, *([nsa_view] * npg), pf, pf, pf, pf, comp, comp, win_view, t5col, expand)


def _reorder_w_in(w_in, d):
    sizes = (HW, 2 * HW, N_HEADS, HW, 6 * NSA_G * HEAD_DIM, 3 * N_HEADS, HW, 2 * HW, 3 * d)
    offs = np.concatenate([[0], np.cumsum(sizes)])
    seg = [w_in[:, :, offs[i]:offs[i + 1]] for i in range(len(sizes))]
    q_a, kv_a, f_a, q_b, kv_b, g_b, q_c, kv_c, g_m = seg
    pad = jnp.zeros(w_in.shape[:2] + (MISC_W - N_HEADS - 3 * N_HEADS,), w_in.dtype)
    return jnp.concatenate([q_a, kv_a, q_b, q_c, kv_c, kv_b, f_a, g_b, pad, g_m], axis=-1).astype(BF16)


def kernel(x_prompt, x_sample, cache_fox_kv, cache_fox_logf, cache_nsa_kv, cache_sb_kv, state_nsa_win_kv, page_table, norm_mix_g, norm_ffn_g, norm_final_g, w_in, b_forget, t5_table, cmp_pe, cmp_w1, cmp_w2, w_out_a, w_out_b, w_out_c, w_out, router_group_w, router_group_b, router_expert_w, router_expert_b, expert_w_gate, expert_w_up, expert_w_down):
    nb, t, d = x_prompt.shape
    nbs, dec_t, _ = x_sample.shape
    depth = w_in.shape[0]
    npool = cache_fox_kv.shape[0]
    npg = page_table.shape[1]
    past = npg * PAGE
    wb = state_nsa_win_kv.shape[2]
    assert dec_t == DEC_T and t % KB == 0 and d % 128 == 0 and wb == WINDOW and past >= WINDOW
    n_p, n_s_rows = nb * t, nbs * DEC_T

    fox_view = jnp.transpose(cache_fox_kv, (0, 1, 3, 4, 5, 2)).reshape(npool, depth, 2, HW, PAGE)
    sb_view = jnp.transpose(cache_sb_kv, (0, 1, 3, 4, 5, 2)).reshape(npool, depth, 2, HW, PAGE)
    nsa_view = jnp.transpose(cache_nsa_kv, (0, 1, 3, 4, 5, 6, 2)).reshape(npool, depth, 2, 2, LANES, PAGE)
    logf_view = jnp.transpose(cache_fox_logf, (0, 1, 3, 2))
    win_view = jnp.transpose(state_nsa_win_kv, (0, 1, 3, 4, 5, 2)).reshape(nbs, depth, 2, LANES, wb)
    pt = page_table.reshape(-1).astype(jnp.int32)

    w_in_r = _reorder_w_in(w_in, d)
    bf_pad = jnp.pad(b_forget.astype(F32), ((0, 0), (0, LANES - N_HEADS))).reshape(depth, 1, LANES)
    t5 = jnp.pad(t5_table.astype(F32).T, ((0, 0), (0, LANES - N_BUCKETS)))
    t5col = jnp.repeat(t5, DEC_T, axis=0)
    expand = jnp.asarray(np.arange(LANES)[:, None] == (np.arange(past)[None, :] // SEL_BLOCK), BF16)
    w1r = cmp_w1.reshape(depth, 2, 2, CMP_STRIDE, HEAD_DIM, CMP_HIDDEN)
    w1big = jnp.einsum("zchldk,gG->zclgdhGk", w1r, jnp.eye(NSA_G, dtype=w1r.dtype)).reshape(
        depth, 2, CHUNK_W, 2 * NSA_G * CMP_HIDDEN).astype(BF16)
    pe_rows = jnp.broadcast_to(cmp_pe.reshape(depth, 2, 2, CMP_STRIDE, 1, HEAD_DIM),
                               (depth, 2, 2, CMP_STRIDE, NSA_G, HEAD_DIM)).reshape(depth, 2, 2, CHUNK_W)
    pe_rows = jnp.pad(pe_rows, ((0, 0), (0, 0), (0, 6), (0, 0))).astype(F32)
    w2p = jnp.stack([jnp.pad(cmp_w2, ((0, 0), (0, 0), (0, 0), (g * HEAD_DIM, LANES - (g + 1) * HEAD_DIM)))
                     for g in range(NSA_G)], axis=2).astype(BF16)
    w_router = jnp.pad(jnp.concatenate([router_group_w, router_expert_w], axis=-1),
                       ((0, 0), (0, 0), (0, LANES - N_GROUPS - N_EXPERTS))).astype(F32)
    b_router = jnp.pad(jnp.concatenate([router_group_b, router_expert_b], axis=-1),
                       ((0, 0), (0, LANES - N_GROUPS - N_EXPERTS))).astype(F32).reshape(depth, 1, LANES)
    wa, wb_, wc, wo = (w.astype(BF16) for w in (w_out_a, w_out_b, w_out_c, w_out))
    wg, wu, wd = (w.astype(BF16) for w in (expert_w_gate, expert_w_up, expert_w_down))

    h = jnp.concatenate([x_prompt.reshape(n_p, d), x_sample.reshape(n_s_rows, d)], axis=0)
    st_p = [[] for _ in range(5)]
    st_s = [[] for _ in range(5)]
    for l in range(depth):
        pf, pb = _proj(h, norm_mix_g[l], w_in_r, l)

        logf_p, cum_p = _cum_prompt(pf, bf_pad[l], nb, t)
        cum_t = jnp.transpose(cum_p[:, :N_HEADS].reshape(nb, t, N_HEADS), (0, 2, 1)).reshape(nb, N_HEADS, t // QB, QB)
        o_a_p = _fox_prompt_t(pb, cum_p, cum_t, nb, t)
        o_c_p = _sb_prompt_t(pb, nb, t)
        comp_p = _compress_prompt(pf, w1big, pe_rows, w2p, l, nb, t)
        o_b_p = _nsa_prompt_t(pb, pf, comp_p, t5, nb, t)

        cum_past, logf_s, cum_new = _cum_sample(pt, logf_view, pf, bf_pad[l], l, nbs, npg, n_p)
        cum_new_t = jnp.pad(jnp.transpose(cum_new[:, :N_HEADS].reshape(nbs, DEC_T, N_HEADS), (0, 2, 1)),
                            ((0, 0), (0, 0), (0, LANES - DEC_T)))
        o_a_s = _fox_sample(pt, fox_view, pf, cum_past, cum_new, cum_new_t, l, nbs, npg, n_p)
        o_c_s = _sb_sample(pt, sb_view, pf, l, nbs, npg, n_p)
        comp_s = _compress_sample(pt, nsa_view, w1big, pe_rows, w2p, l, nbs, npg)
        o_b_s = _nsa_sample(pt, nsa_view, pf, comp_s, win_view, t5col, expand, l, nbs, npg, n_p)

        o_a = jnp.concatenate([o_a_p, o_a_s.astype(BF16)], axis=0)
        o_b = jnp.concatenate([o_b_p, o_b_s.astype(BF16)], axis=0)
        o_c = jnp.concatenate([o_c_p, o_c_s.astype(BF16)], axis=0)
        mixed = _merge(o_a, o_b, o_c, pf, wa, wb_, wc, d, l)
        h = _mm_res(mixed, wo, h, l)
        xn, comb = _router(h, norm_ffn_g[l], w_router[l], b_router[l])
        h = _moe(xn, comb, h, wg, wu, wd, l)

        def rows(c0, width, shape, lo, hi):
            return pf[lo:hi, c0:c0 + width].reshape(shape)

        win_new_p = rows(C_WIN, 2 * LANES, (nb, t, 2, NSA_G, HEAD_DIM), 0, n_p)
        win_new_s = rows(C_WIN, 2 * LANES, (nbs, DEC_T, 2, NSA_G, HEAD_DIM), n_p, n_p + n_s_rows)
        st_p[0].append(rows(C_KA, 2 * HW, (nb, t, 2, N_HEADS, HEAD_DIM), 0, n_p))
        st_p[1].append(logf_p[:, :N_HEADS].reshape(nb, t, N_HEADS))
        st_p[2].append(rows(C_CMP, 4 * LANES, (nb, t, 2, 2, NSA_G, HEAD_DIM), 0, n_p))
        st_p[3].append(rows(C_KC, 2 * HW, (nb, t, 2, N_HEADS, HEAD_DIM), 0, n_p))
        st_p[4].append(win_new_p[:, t - min(WINDOW, t):])
        st_s[0].append(rows(C_KA, 2 * HW, (nbs, DEC_T, 2, N_HEADS, HEAD_DIM), n_p, n_p + n_s_rows))
        st_s[1].append(logf_s[:, :N_HEADS].reshape(nbs, DEC_T, N_HEADS))
        st_s[2].append(rows(C_CMP, 4 * LANES, (nbs, DEC_T, 2, 2, NSA_G, HEAD_DIM), n_p, n_p + n_s_rows))
        st_s[3].append(rows(C_KC, 2 * HW, (nbs, DEC_T, 2, N_HEADS, HEAD_DIM), n_p, n_p + n_s_rows))
        wkv = jnp.concatenate([state_nsa_win_kv[:, l], win_new_s], axis=1)
        st_s[4].append(wkv[:, wb + DEC_T - min(WINDOW, wb + DEC_T):])

    y = _final_norm(h, norm_final_g)
    y_prompt = y[:n_p].reshape(nb, t, d)
    y_sample = y[n_p:].reshape(nbs, DEC_T, d)
    sp = [jnp.stack(s, axis=1) for s in st_p]
    ss = [jnp.stack(s, axis=1) for s in st_s]
    return (y_prompt, y_sample, sp[0], ss[0], sp[1], ss[1], sp[2], ss[2], sp[3], ss[3], sp[4], ss[4])
```

```python
import functools
import math

import numpy as np
import jax
import jax.numpy as jnp
from jax import lax
from jax.experimental import pallas as pl
from jax.experimental.pallas import tpu as pltpu

F32 = jnp.float32
BF16 = jnp.bfloat16

HEAD_DIM = 64
N_HEADS = 8
NSA_G = 2
NSA_HG = N_HEADS // NSA_G
HW = N_HEADS * HEAD_DIM
PAGE = 128
DEC_T = 8
CMP_BLOCK = 32
CMP_STRIDE = 16
CMP_HIDDEN = 128
SEL_BLOCK = 64
SEL_TOPK = 8
WINDOW = 512
N_BUCKETS = 32
MAX_DISTANCE = 128
N_GROUPS = 4
EXPERTS_PER_GROUP = 4
N_EXPERTS = N_GROUPS * EXPERTS_PER_GROUP
RMS_EPS = 1e-6
NEG_INF = -1e30
FORCE_SCORE = 1e9
BELOW_ALL = -3e38
SCALE = HEAD_DIM ** -0.5
TINY = float(np.finfo(np.float32).tiny)
QB = 128
KB = 256
LANES = 128
VMEM_LIMIT = 56 * 1024 * 1024

C_QA, C_KA, C_VA, C_QB, C_QC, C_KC, C_VC = 0, 512, 1024, 1536, 2048, 2560, 3072
C_CMP, C_SEL, C_WIN, C_MISC, C_GM = 3584, 3840, 4096, 4352, 4608
MISC_W = 256
CHUNK_W = CMP_STRIDE * LANES


def _params(*sem):
    return pltpu.CompilerParams(dimension_semantics=sem, vmem_limit_bytes=VMEM_LIMIT)


def _pick(n, cap, mult):
    t = (min(cap, n) // mult) * mult
    while t > 0 and n % t:
        t -= mult
    assert t > 0, (n, cap, mult)
    return t


def _iota(shape, axis):
    return lax.broadcasted_iota(jnp.int32, shape, axis)


def _dot(a, b):
    return jnp.dot(a, b, preferred_element_type=F32)


def _dot_nt(a, b):
    return lax.dot_general(a, b, (((1,), (1,)), ((), ())), preferred_element_type=F32)


def _softplus(x):
    return jnp.maximum(x, 0.0) + jnp.log(1.0 + jnp.exp(-jnp.abs(x)))


def _log_sigmoid(x):
    return jnp.minimum(x, 0.0) - jnp.log1p(jnp.exp(-jnp.abs(x)))


def _sigmoid(x):
    return 1.0 / (1.0 + jnp.exp(-x))


def _split_dot(x, w):
    hi = x.astype(BF16)
    lo = (x - hi.astype(F32)).astype(BF16)
    return _dot(hi, w) + _dot(lo, w)


def _tri(w):
    return jnp.where(_iota((w, w), 0) > _iota((w, w), 1), 1.0, 0.0).astype(BF16)


def _bucket(rel):
    n = jnp.maximum(rel, 0)
    exact = N_BUCKETS // 2
    far = jnp.log(jnp.maximum(n, exact).astype(F32) / exact) / math.log(MAX_DISTANCE / exact)
    far = exact + (far * (N_BUCKETS - exact)).astype(jnp.int32)
    return jnp.where(n < exact, n, jnp.minimum(far, N_BUCKETS - 1))


def _masked_softmax(logits, mask):
    l = jnp.where(mask, logits, NEG_INF)
    e = jnp.where(mask, jnp.exp(l - jnp.max(l, axis=-1, keepdims=True)), 0.0)
    return e / jnp.maximum(jnp.sum(e, axis=-1, keepdims=True), TINY)


def _overlap(nch):
    cs = _iota((nch, LANES), 0) * CMP_STRIDE
    ss = _iota((nch, LANES), 1) * SEL_BLOCK
    ov = jnp.maximum(jnp.minimum(cs + CMP_BLOCK, ss + SEL_BLOCK) - jnp.maximum(cs, ss), 0)
    return (ov.astype(F32) / CMP_BLOCK).astype(BF16)


def _select_blocks(score, qpos, n_s):
    lane = _iota(score.shape, 1)
    lane_f = lane.astype(F32)
    valid = lane * SEL_BLOCK <= qpos
    forced = (lane == jnp.right_shift(qpos, 6)) | (lane == 0)
    sc = jnp.where(forced, FORCE_SCORE, jnp.where(valid, score, NEG_INF))
    sc = jnp.where(lane < n_s, sc, BELOW_ALL)
    rank = jnp.zeros(score.shape, F32)
    for k in range(n_s):
        other = sc[:, k:k + 1]
        rank = rank + jnp.where((other > sc) | ((other == sc) & (lane > k)), 1.0, 0.0)
    return jnp.where((rank < min(SEL_TOPK, n_s)) & (lane < n_s), 1.0, 0.0)


def _proj_kernel(x_ref, g_ref, w_ref, of_ref, ob_ref, xn_ref):
    @pl.when(pl.program_id(1) == 0)
    def _():
        x = x_ref[...]
        ms = jnp.mean(x * x, axis=-1, keepdims=True)
        xn_ref[...] = (x * lax.rsqrt(ms + RMS_EPS) * g_ref[...]).astype(BF16)

    y = _dot(xn_ref[...], w_ref[...])
    of_ref[...] = y
    ob_ref[...] = y.astype(BF16)


def _proj(h, g, w, layer):
    n, d = h.shape
    wp = w.shape[2]
    tm = _pick(n, 1024, 16)
    tn = _pick(wp, 768, 128)
    return pl.pallas_call(
        _proj_kernel,
        grid=(n // tm, wp // tn),
        in_specs=[pl.BlockSpec((tm, d), lambda i, j: (i, 0)),
                  pl.BlockSpec((1, d), lambda i, j: (0, 0)),
                  pl.BlockSpec((None, d, tn), lambda i, j: (layer, 0, j))],
        out_specs=[pl.BlockSpec((tm, tn), lambda i, j: (i, j)),
                   pl.BlockSpec((tm, tn), lambda i, j: (i, j))],
        out_shape=[jax.ShapeDtypeStruct((n, wp), F32), jax.ShapeDtypeStruct((n, wp), BF16)],
        scratch_shapes=[pltpu.VMEM((tm, d), BF16)],
        compiler_params=_params("arbitrary", "arbitrary"),
        name="proj",
    )(h, g.reshape(1, d), w)


def _merge_kernel(oa_ref, ob_ref, oc_ref, wa_ref, wb_ref, wc_ref, g0_ref, g1_ref, g2_ref, o_ref):
    m = _sigmoid(g0_ref[...]) * _dot(oa_ref[...], wa_ref[...])
    m = m + _sigmoid(g1_ref[...]) * _dot(ob_ref[...], wb_ref[...])
    m = m + _sigmoid(g2_ref[...]) * _dot(oc_ref[...], wc_ref[...])
    o_ref[...] = m.astype(BF16)


def _merge(o_a, o_b, o_c, pf, wa, wb, wc, d, layer):
    n = o_a.shape[0]
    tm = _pick(n, 512, 16)
    tn = _pick(d, 512, 128)
    gm0 = C_GM // tn
    o_spec = pl.BlockSpec((tm, HW), lambda i, j: (i, 0))
    w_spec = pl.BlockSpec((None, HW, tn), lambda i, j: (layer, 0, j))

    def g_spec(k):
        return pl.BlockSpec((tm, tn), lambda i, j: (i, gm0 + k * (d // tn) + j))

    return pl.pallas_call(
        _merge_kernel,
        grid=(n // tm, d // tn),
        in_specs=[o_spec, o_spec, o_spec, w_spec, w_spec, w_spec, g_spec(0), g_spec(1), g_spec(2)],
        out_specs=pl.BlockSpec((tm, tn), lambda i, j: (i, j)),
        out_shape=jax.ShapeDtypeStruct((n, d), BF16),
        compiler_params=_params("arbitrary", "arbitrary"),
        name="merge",
    )(o_a, o_b, o_c, wa, wb, wc, pf, pf, pf)


def _mm_res_kernel(x_ref, w_ref, r_ref, o_ref):
    o_ref[...] = r_ref[...] + _dot(x_ref[...], w_ref[...])


def _mm_res(x, w, res, layer):
    n, k = x.shape
    d = w.shape[2]
    tm = _pick(n, 512, 16)
    tn = _pick(d, 512, 128)
    return pl.pallas_call(
        _mm_res_kernel,
        grid=(n // tm, d // tn),
        in_specs=[pl.BlockSpec((tm, k), lambda i, j: (i, 0)),
                  pl.BlockSpec((None, k, tn), lambda i, j: (layer, 0, j)),
                  pl.BlockSpec((tm, tn), lambda i, j: (i, j))],
        out_specs=pl.BlockSpec((tm, tn), lambda i, j: (i, j)),
        out_shape=jax.ShapeDtypeStruct((n, d), F32),
        compiler_params=_params("arbitrary", "arbitrary"),
        name="out_proj",
    )(x, w, res)


def _router_kernel(h_ref, g_ref, w_ref, b_ref, xn_ref, comb_ref):
    x = h_ref[...]
    ms = jnp.mean(x * x, axis=-1, keepdims=True)
    xn = x * lax.rsqrt(ms + RMS_EPS) * g_ref[...]
    xn_ref[...] = xn.astype(BF16)
    logits = jnp.dot(xn, w_ref[...], precision=lax.Precision.HIGHEST, preferred_element_type=F32) + b_ref[...]
    lane = _iota(logits.shape, 1)
    lane_f = lane.astype(F32)
    is_grp = lane < N_GROUPS
    gl = jnp.where(is_grp, logits, BELOW_ALL)
    gmax = jnp.max(gl, axis=-1, keepdims=True)
    gsum = jnp.sum(jnp.where(is_grp, jnp.exp(gl - gmax), 0.0), axis=-1, keepdims=True)
    w_grp = 1.0 / gsum
    g_star = jnp.min(jnp.where(is_grp & (gl == gmax), lane_f, 1e9), axis=-1, keepdims=True)
    lo = N_GROUPS + g_star * EXPERTS_PER_GROUP
    in_grp = (lane_f >= lo) & (lane_f < lo + EXPERTS_PER_GROUP)
    el = jnp.where(in_grp, logits, BELOW_ALL)
    v1 = jnp.max(el, axis=-1, keepdims=True)
    i1 = jnp.min(jnp.where(el == v1, lane_f, 1e9), axis=-1, keepdims=True)
    el2 = jnp.where(lane_f == i1, BELOW_ALL, el)
    v2 = jnp.max(el2, axis=-1, keepdims=True)
    i2 = jnp.min(jnp.where(el2 == v2, lane_f, 1e9), axis=-1, keepdims=True)
    e2 = jnp.exp(v2 - v1)
    den = 1.0 + e2
    comb = jnp.where(lane_f == i1, w_grp / den, 0.0) + jnp.where(lane_f == i2, w_grp * e2 / den, 0.0)
    comb_ref[...] = comb


def _router(h, g, w, b):
    n, d = h.shape
    tm = _pick(n, 512, 16)
    return pl.pallas_call(
        _router_kernel,
        grid=(n // tm,),
        in_specs=[pl.BlockSpec((tm, d), lambda i: (i, 0)),
                  pl.BlockSpec((1, d), lambda i: (0, 0)),
                  pl.BlockSpec((d, LANES), lambda i: (0, 0)),
                  pl.BlockSpec((1, LANES), lambda i: (0, 0))],
        out_specs=[pl.BlockSpec((tm, d), lambda i: (i, 0)),
                   pl.BlockSpec((tm, LANES), lambda i: (i, 0))],
        out_shape=[jax.ShapeDtypeStruct((n, d), BF16), jax.ShapeDtypeStruct((n, LANES), F32)],
        compiler_params=_params("arbitrary"),
        name="router",
    )(h, g.reshape(1, d), w, b)


def _moe_kernel(x_ref, c_ref, h_ref, wg_ref, wu_ref, wd_ref, o_ref, acc_ref):
    e = pl.program_id(1)

    @pl.when(e == 0)
    def _():
        acc_ref[...] = h_ref[...]

    x = x_ref[...]
    comb = c_ref[...]
    ce = jnp.sum(jnp.where(_iota(comb.shape, 1) == e + N_GROUPS, comb, 0.0), axis=-1, keepdims=True)
    gate = _dot(x, wg_ref[0])
    up = _dot(x, wu_ref[0])
    hh = gate * _sigmoid(gate) * up * ce
    acc_ref[...] += _dot(hh.astype(BF16), wd_ref[0])

    @pl.when(e == pl.num_programs(1) - 1)
    def _():
        o_ref[...] = acc_ref[...]


def _moe(xn, comb, h, wg, wu, wd, layer):
    n, d = h.shape
    _, ne, _, ff = wg.shape
    tm = _pick(n, 512, 16)
    return pl.pallas_call(
        _moe_kernel,
        grid=(n // tm, ne),
        in_specs=[pl.BlockSpec((tm, d), lambda i, e: (i, 0)),
                  pl.BlockSpec((tm, LANES), lambda i, e: (i, 0)),
                  pl.BlockSpec((tm, d), lambda i, e: (i, 0)),
                  pl.BlockSpec((None, 1, d, ff), lambda i, e: (layer, e, 0, 0)),
                  pl.BlockSpec((None, 1, d, ff), lambda i, e: (layer, e, 0, 0)),
                  pl.BlockSpec((None, 1, ff, d), lambda i, e: (layer, e, 0, 0))],
        out_specs=pl.BlockSpec((tm, d), lambda i, e: (i, 0)),
        out_shape=jax.ShapeDtypeStruct((n, d), F32),
        scratch_shapes=[pltpu.VMEM((tm, d), F32)],
        compiler_params=_params("arbitrary", "arbitrary"),
        name="moe",
    )(xn, comb, h, wg, wu, wd)


def _final_norm_kernel(h_ref, g_ref, o_ref):
    x = h_ref[...]
    ms = jnp.mean(x * x, axis=-1, keepdims=True)
    o_ref[...] = x * lax.rsqrt(ms + RMS_EPS) * g_ref[...]


def _final_norm(h, g):
    n, d = h.shape
    tm = _pick(n, 512, 8)
    return pl.pallas_call(
        _final_norm_kernel,
        grid=(n // tm,),
        in_specs=[pl.BlockSpec((tm, d), lambda i: (i, 0)), pl.BlockSpec((1, d), lambda i: (0, 0))],
        out_specs=pl.BlockSpec((tm, d), lambda i: (i, 0)),
        out_shape=jax.ShapeDtypeStruct((n, d), F32),
        compiler_params=_params("arbitrary"),
        name="final_norm",
    )(h, g.reshape(1, d))


def _cum_prompt_kernel(m_ref, bf_ref, logf_ref, cum_ref):
    logf = _log_sigmoid(m_ref[:, 0:LANES] + bf_ref[...])
    logf_ref[...] = logf
    t = logf.shape[0]
    row = _iota(logf.shape, 0)
    c = logf
    s = 1
    while s < t:
        c = c + jnp.where(row >= s, pltpu.roll(c, s, axis=0), 0.0)
        s *= 2
    cum_ref[...] = c


def _cum_prompt(pf, bf, nb, t):
    return pl.pallas_call(
        _cum_prompt_kernel,
        grid=(nb,),
        in_specs=[pl.BlockSpec((t, MISC_W), lambda b: (b, C_MISC // MISC_W)),
                  pl.BlockSpec((1, LANES), lambda b: (0, 0))],
        out_specs=[pl.BlockSpec((t, LANES), lambda b: (b, 0)), pl.BlockSpec((t, LANES), lambda b: (b, 0))],
        out_shape=[jax.ShapeDtypeStruct((nb * t, LANES), F32), jax.ShapeDtypeStruct((nb * t, LANES), F32)],
        compiler_params=_params("arbitrary"),
        name="cum_prompt",
    )(pf, bf)


def _stack_pair(q_ref, p, lane):
    qp = q_ref[:, p * LANES:(p + 1) * LANES] * jnp.asarray(SCALE, BF16)
    zero = jnp.zeros_like(qp)
    return jnp.concatenate([jnp.where(lane < HEAD_DIM, qp, zero), jnp.where(lane >= HEAD_DIM, qp, zero)], axis=0)


def _compress_chunks(ch_ref, w1_ref, pe_ref, w2_ref, o_ref):
    nch = ch_ref.shape[1]
    hw = NSA_G * CMP_HIDDEN
    for c in range(2):
        w1 = w1_ref[0, c]
        pe = _dot(pe_ref[0, c].astype(BF16), w1)
        y = _dot(ch_ref[c].astype(BF16), w1)
        out = None
        for g in range(NSA_G):
            lo = slice(g * CMP_HIDDEN, (g + 1) * CMP_HIDDEN)
            hi = slice(hw + g * CMP_HIDDEN, hw + (g + 1) * CMP_HIDDEN)
            pre = y[:, lo] + pltpu.roll(y[:, hi], nch - 1, axis=0) + pe[0:1, lo] + pe[1:2, hi]
            hid = 0.5 * pre * (1.0 + jnp.tanh(math.sqrt(2.0 / math.pi) * (pre + 0.044715 * pre * pre * pre)))
            cg = _dot(hid.astype(BF16), w2_ref[0, c, g])
            out = cg if out is None else out + cg
        o_ref[c, 0] = out.astype(BF16)


def _compress_weight_specs(layer, imap):
    return [pl.BlockSpec((1, 2, CHUNK_W, 2 * NSA_G * CMP_HIDDEN), imap((layer, 0, 0, 0))),
            pl.BlockSpec((1, 2, 8, CHUNK_W), imap((layer, 0, 0, 0))),
            pl.BlockSpec((1, 2, NSA_G, CMP_HIDDEN, LANES), imap((layer, 0, 0, 0, 0)))]


def _compress_prompt_kernel(xk_ref, xv_ref, w1_ref, pe_ref, w2_ref, o_ref, ch_ref):
    nch = ch_ref.shape[1]
    for c, x_ref in enumerate((xk_ref, xv_ref)):
        for r in range(CMP_STRIDE):
            ch_ref[c, :, r * LANES:(r + 1) * LANES] = x_ref[pl.ds(r, nch, stride=CMP_STRIDE), :]
    _compress_chunks(ch_ref, w1_ref, pe_ref, w2_ref, o_ref)


def _compress_prompt(pf, w1big, pe_rows, w2p, layer, nb, t):
    nch = t // CMP_STRIDE
    return pl.pallas_call(
        _compress_prompt_kernel,
        grid=(nb,),
        in_specs=[pl.BlockSpec((t, LANES), lambda b: (b, C_CMP // LANES)),
                  pl.BlockSpec((t, LANES), lambda b: (b, C_CMP // LANES + 1))]
        + _compress_weight_specs(layer, lambda idx: (lambda b: idx)),
        out_specs=pl.BlockSpec((2, 1, nch, LANES), lambda b: (0, b, 0, 0)),
        out_shape=jax.ShapeDtypeStruct((2, nb, nch, LANES), BF16),
        scratch_shapes=[pltpu.VMEM((2, nch, CHUNK_W), F32)],
        compiler_params=_params("arbitrary"),
        name="nsa_compress_prompt",
    )(pf, pf, w1big, pe_rows, w2p)


def _lut(tab, bk):
    parts = [jnp.take_along_axis(tab, bk[:, c:c + LANES], axis=1) for c in range(0, bk.shape[1], LANES)]
    return parts[0] if len(parts) == 1 else jnp.concatenate(parts, axis=1)


def _transpose_bf16(x):
    return x.astype(F32).T.astype(BF16)


def _pair_out(o_t, lane_dtype=BF16):
    own = jnp.concatenate([o_t[0:HEAD_DIM, 0:QB], o_t[HEAD_DIM:LANES, QB:2 * QB]], axis=0)
    return own.T.astype(lane_dtype)


def _fox_t_kernel(q_ref, k_ref, v_ref, cum_ref, cq_ref, o_ref, vt_ref, ckb_ref):
    i = pl.program_id(1)
    npair = N_HEADS // 2
    nkb = vt_ref.shape[0]

    @pl.when(i == 0)
    def _():
        for kb in range(nkb):
            for p in range(npair):
                vt_ref[kb, p * LANES:(p + 1) * LANES, :] = _transpose_bf16(v_ref[kb * KB:(kb + 1) * KB, p * LANES:(p + 1) * LANES])
        for h in range(N_HEADS):
            ckb_ref[h] = jnp.broadcast_to(cum_ref[:, h:h + 1], ckb_ref.shape[1:])

    lane = _iota((QB, LANES), 1)
    qrow = i * QB + jnp.bitwise_and(_iota((1, 2 * QB), 1), QB - 1)
    krow = _iota((KB, 1), 0)
    q2 = [_stack_pair(q_ref, p, lane) for p in range(npair)]
    cq2 = [jnp.concatenate([cq_ref[0, 2 * p, pl.ds(i, 1), :], cq_ref[0, 2 * p + 1, pl.ds(i, 1), :]], axis=1)
           for p in range(npair)]

    def step(kb, carry, masked):
        k0 = pl.multiple_of(kb * KB, KB)
        scores = [_dot_nt(k_ref[pl.ds(k0, KB), p * LANES:(p + 1) * LANES], q2[p]) for p in range(npair)]
        probs, stats = [], []
        for p in range(npair):
            m, l, _ = carry[p]
            ck = jnp.concatenate([ckb_ref[2 * p, pl.ds(k0, KB), :], ckb_ref[2 * p + 1, pl.ds(k0, KB), :]], axis=1)
            s = scores[p] + cq2[p] - ck
            if masked:
                s = jnp.where((k0 + krow) <= qrow, s, NEG_INF)
            m_new = jnp.maximum(m, jnp.max(s, axis=0, keepdims=True))
            pr = jnp.exp(s - m_new)
            alpha = jnp.exp(m - m_new)
            probs.append(pr.astype(BF16))
            stats.append((m_new, alpha * l + jnp.sum(pr, axis=0, keepdims=True), alpha))
        out = []
        for p in range(npair):
            m_new, l, alpha = stats[p]
            acc = alpha * carry[p][2] + _dot(vt_ref[kb, p * LANES:(p + 1) * LANES, :], probs[p])
            out.append((m_new, l, acc))
        return tuple(out)

    init = tuple((jnp.full((1, 2 * QB), NEG_INF, F32), jnp.zeros((1, 2 * QB), F32), jnp.zeros((LANES, 2 * QB), F32))
                 for _ in range(npair))
    last = (i * QB) // KB
    carry = lax.fori_loop(0, last, lambda kb, c: step(kb, c, False), init)
    carry = step(last, carry, True)
    for p in range(npair):
        m, l, acc = carry[p]
        o_ref[:, p * LANES:(p + 1) * LANES] = _pair_out(acc / jnp.maximum(l, TINY))


def _fox_prompt_t(pb, cum, cum_t, nb, t):
    nq = t // QB
    return pl.pallas_call(
        _fox_t_kernel,
        grid=(nb, nq),
        in_specs=[pl.BlockSpec((QB, HW), lambda b, i: (b * nq + i, C_QA // HW)),
                  pl.BlockSpec((t, HW), lambda b, i: (b, C_KA // HW)),
                  pl.BlockSpec((t, HW), lambda b, i: (b, C_VA // HW)),
                  pl.BlockSpec((t, LANES), lambda b, i: (b, 0)),
                  pl.BlockSpec((1, N_HEADS, nq, QB), lambda b, i: (b, 0, 0, 0))],
        out_specs=pl.BlockSpec((QB, HW), lambda b, i: (b * nq + i, 0)),
        out_shape=jax.ShapeDtypeStruct((nb * t, HW), BF16),
        scratch_shapes=[pltpu.VMEM((t // KB, HW, KB), BF16), pltpu.VMEM((N_HEADS, t, LANES), F32)],
        compiler_params=_params("arbitrary", "arbitrary"),
        name="fox_prompt",
    )(pb, pb, pb, cum, cum_t)


def _sb_t_kernel(q_ref, k_ref, v_ref, o_ref, vt_ref):
    i = pl.program_id(1)
    npair = N_HEADS // 2
    nkb = vt_ref.shape[0]

    @pl.when(i == 0)
    def _():
        for kb in range(nkb):
            for p in range(npair):
                vt_ref[kb, p * LANES:(p + 1) * LANES, :] = _transpose_bf16(v_ref[kb * KB:(kb + 1) * KB, p * LANES:(p + 1) * LANES])

    lane = _iota((QB, LANES), 1)
    qrow = i * QB + jnp.bitwise_and(_iota((1, 2 * QB), 1), QB - 1)
    krow = _iota((KB, 1), 0)
    later = jnp.where(_iota((KB, KB), 1) > _iota((KB, KB), 0), 1.0, 0.0).astype(BF16)
    q2 = [_stack_pair(q_ref, p, lane) for p in range(npair)]

    def step(kb, carry, masked):
        k0 = pl.multiple_of(kb * KB, KB)
        vis = (k0 + krow) < qrow
        zs = [_dot_nt(k_ref[pl.ds(k0, KB), p * LANES:(p + 1) * LANES], q2[p]) for p in range(npair)]
        sps = [_softplus(z) for z in zs]
        lks = [jnp.where(vis, -sp, 0.0) if masked else -sp for sp in sps]
        his = [lk.astype(BF16) for lk in lks]
        los = [(lk - hi.astype(F32)).astype(BF16) for lk, hi in zip(lks, his)]
        betweens = [_dot(later, hi) + _dot(later, lo) for hi, lo in zip(his, los)]
        weights = []
        for p in range(npair):
            a = jnp.exp(zs[p] - sps[p] + betweens[p] + carry[p][0])
            weights.append((jnp.where(vis, a, 0.0) if masked else a).astype(BF16))
        out = []
        for p in range(npair):
            r, acc = carry[p]
            acc = acc + _dot(vt_ref[kb, p * LANES:(p + 1) * LANES, :], weights[p])
            out.append((r + jnp.sum(lks[p], axis=0, keepdims=True), acc))
        return tuple(out)

    init = tuple((jnp.zeros((1, 2 * QB), F32), jnp.zeros((LANES, 2 * QB), F32)) for _ in range(npair))
    last = (i * QB) // KB
    carry = step(last, init, True)
    carry = lax.fori_loop(0, last, lambda n, c: step(last - 1 - n, c, False), carry)
    for p in range(npair):
        o_ref[:, p * LANES:(p + 1) * LANES] = _pair_out(carry[p][1])


def _sb_prompt_t(pb, nb, t):
    nq = t // QB
    return pl.pallas_call(
        _sb_t_kernel,
        grid=(nb, nq),
        in_specs=[pl.BlockSpec((QB, HW), lambda b, i: (b * nq + i, C_QC // HW)),
                  pl.BlockSpec((t, HW), lambda b, i: (b, C_KC // HW)),
                  pl.BlockSpec((t, HW), lambda b, i: (b, C_VC // HW))],
        out_specs=pl.BlockSpec((QB, HW), lambda b, i: (b * nq + i, 0)),
        out_shape=jax.ShapeDtypeStruct((nb * t, HW), BF16),
        scratch_shapes=[pltpu.VMEM((t // KB, HW, KB), BF16)],
        compiler_params=_params("arbitrary", "arbitrary"),
        name="sb_prompt",
    )(pb, pb, pb)


def _select_blocks_t(score, qrow, n_s, sc_ref):
    blk = _iota(score.shape, 0)
    blk_f = blk.astype(F32)
    valid = blk * SEL_BLOCK <= qrow
    forced = (blk == jnp.right_shift(qrow, 6)) | (blk == 0)
    sc = jnp.where(forced, FORCE_SCORE, jnp.where(valid, score, NEG_INF))
    sc = jnp.where(blk < n_s, sc, BELOW_ALL)
    rows = min(score.shape[0], -(-n_s // 8) * 8)
    sc, blk = sc[0:rows], _iota((rows, score.shape[1]), 0)
    sc_ref[0:rows, :] = sc
    rank = jnp.zeros(sc.shape, F32)
    for k in range(n_s):
        other = sc_ref[k:k + 1, :]
        rank = rank + jnp.where((other > sc) | ((other == sc) & (blk > k)), 1.0, 0.0)
    sel = jnp.where((rank < min(SEL_TOPK, n_s)) & (blk < n_s), 1.0, 0.0)
    if rows < score.shape[0]:
        sel = jnp.concatenate([sel, jnp.zeros((score.shape[0] - rows, score.shape[1]), F32)], axis=0)
    return sel


def _nsa_t_kernel(t5_ref, q_ref, ck_ref, cv_ref, sel_ref, win_ref, misc_ref, o_ref, svt_ref, wvt_ref, sc_ref, *, n_s):
    i = pl.program_id(1)
    q0 = i * QB
    nch = ck_ref.shape[2]
    nblk = svt_ref.shape[0]
    cols = N_HEADS * QB

    @pl.when(i == 0)
    def _():
        for kb in range(nblk):
            svt_ref[kb] = _transpose_bf16(sel_ref[kb * QB:(kb + 1) * QB, LANES:2 * LANES])
            wvt_ref[kb] = _transpose_bf16(win_ref[kb * QB:(kb + 1) * QB, LANES:2 * LANES])

    lane = _iota((QB, LANES), 1)
    qrow = q0 + _iota((1, QB), 1)

    def per_head(f):
        return jnp.concatenate([f(h) for h in range(N_HEADS)], axis=1)

    def per_group(x0, x1):
        return jnp.concatenate([x0] * NSA_HG + [x1] * NSA_HG, axis=1)

    def bias_of(rel):
        bk = _bucket(rel)
        return per_head(lambda h: _lut(jnp.broadcast_to(t5_ref[h:h + 1, :], (rel.shape[0], LANES)), bk))

    def q_head(head):
        g = head // NSA_HG
        x = q_ref[:, (head // 2) * LANES:(head // 2 + 1) * LANES].astype(F32) * SCALE
        if head % 2 != g:
            x = pltpu.roll(x, HEAD_DIM, axis=1)
        return jnp.where((lane >= HEAD_DIM) == (g == 1), x, 0.0).astype(BF16)

    q8 = jnp.concatenate([q_head(h) for h in range(N_HEADS)], axis=0)
    rel_d = _iota((QB, QB), 1) - _iota((QB, QB), 0)
    bias_d = bias_of(rel_d)
    bias_p = bias_of(rel_d + QB)
    bias_far = per_head(lambda h: jnp.broadcast_to(t5_ref[h:h + 1, N_BUCKETS - 1:N_BUCKETS], (1, QB)))

    nrow = _iota((nch, QB), 0)
    rel_c = qrow - (nrow * CMP_STRIDE + (CMP_BLOCK - 1))
    vis_c = jnp.where((rel_c >= 0) & (nrow < nch - 1), 1.0, 0.0)
    mask_c = per_group(vis_c, vis_c) > 0.5
    lc = jnp.where(mask_c, _dot_nt(ck_ref[0, 0], q8) + bias_of(rel_c), NEG_INF)
    ec = jnp.where(mask_c, jnp.exp(lc - jnp.max(lc, axis=0, keepdims=True)), 0.0)
    pc = ec / jnp.maximum(jnp.sum(ec, axis=0, keepdims=True), TINY)
    o_cmp = _dot(_transpose_bf16(cv_ref[0, 0]), pc.astype(BF16))
    psums = []
    for g in range(NSA_G):
        ps = pc[:, g * NSA_HG * QB:(g * NSA_HG + 1) * QB]
        for hg in range(1, NSA_HG):
            ps = ps + pc[:, (g * NSA_HG + hg) * QB:(g * NSA_HG + hg + 1) * QB]
        psums.append(ps)
    psum = jnp.concatenate(psums, axis=1)
    cs = _iota((LANES, nch), 1) * CMP_STRIDE
    ss = _iota((LANES, nch), 0) * SEL_BLOCK
    ov_t = (jnp.maximum(jnp.minimum(cs + CMP_BLOCK, ss + SEL_BLOCK) - jnp.maximum(cs, ss), 0).astype(F32)
            / CMP_BLOCK).astype(BF16)
    p_hi = psum.astype(BF16)
    p_lo = (psum - p_hi.astype(F32)).astype(BF16)
    selm = _select_blocks_t(_dot(ov_t, p_hi) + _dot(ov_t, p_lo), jnp.concatenate([qrow] * NSA_G, axis=1),
                            n_s, sc_ref).astype(BF16)

    def attend(steps):
        loaded = []
        for k_ref, vt_ref, k0, width, _, _, _, real in steps:
            k0 = pl.multiple_of(k0, QB)
            kk = k_ref[pl.ds(k0, width), 0:LANES]
            kb = jnp.right_shift(k0, 7)
            vt = vt_ref[kb] if width == QB else jnp.concatenate([vt_ref[kb], vt_ref[kb + 1]], axis=1)
            if real is not None:
                kk = jnp.where(real, kk, jnp.zeros_like(kk))
                vt = jnp.where(real, vt, jnp.zeros_like(vt))
            loaded.append((kk, vt))
        scores = [_dot_nt(kk, q8) for kk, _ in loaded]
        soft = []
        for (_, _, _, _, hide, bias, (m, l, _), _), s in zip(steps, scores):
            s = s + (bias + hide)
            m_new = jnp.maximum(m, jnp.max(s, axis=0, keepdims=True))
            pr = jnp.exp(s - m_new)
            alpha = jnp.exp(m - m_new)
            soft.append((m_new, alpha * l + jnp.sum(pr, axis=0, keepdims=True), alpha, pr.astype(BF16)))
        return [(m_new, l, alpha * step[6][2] + _dot(vt, pr))
                for step, (_, vt), (m_new, l, alpha, pr) in zip(steps, loaded, soft)]

    def init():
        return (jnp.full((1, cols), NEG_INF, F32), jnp.zeros((1, cols), F32), jnp.zeros((LANES, cols), F32))

    def win_step(dlt, carry):
        rel = qrow - ((i - dlt) * QB + _iota((QB, 1), 0))
        hide = jnp.where((rel >= 0) & (rel < WINDOW), 0.0, NEG_INF)
        bias = bias_d if dlt == 0 else (bias_p if dlt == 1 else bias_far)
        return (win_ref, wvt_ref, jnp.maximum(i - dlt, 0) * QB, QB, per_group(hide, hide), bias, carry,
                None if dlt == 0 else i - dlt >= 0)

    def sel_step(k0, width, bias, carry, causal=False, valid=None):
        blk = jnp.right_shift(k0 + _iota((width, LANES), 0), 6)
        expand = jnp.where(_iota((width, LANES), 1) == blk, 1.0, 0.0).astype(BF16)
        picked = _dot(expand, selm)
        if causal:
            picked = picked * jnp.concatenate([jnp.where((k0 + _iota((width, 1), 0)) <= qrow, 1.0, 0.0)] * NSA_G, axis=1)
        if valid is not None:
            picked = picked * jnp.where(valid, 1.0, 0.0)
        hide = jnp.where(picked > 0.5, 0.0, NEG_INF)
        return (sel_ref, svt_ref, k0, width, per_group(hide[:, 0:QB], hide[:, QB:2 * QB]), bias, carry, None)

    k_prev = jnp.maximum(q0 - QB, 0)
    k_odd = jnp.maximum(q0 - 2 * QB, 0)
    c_win, c_sel = attend([win_step(0, init()), sel_step(q0, QB, bias_d, init(), causal=True)])
    c_win, c_sel = attend([win_step(1, c_win), sel_step(k_prev, QB, bias_p, c_sel, valid=i >= 1)])
    c_win, c_sel = attend([win_step(2, c_win),
                           sel_step(k_odd, QB, bias_far, c_sel, valid=(i >= 2) & (jnp.bitwise_and(i, 1) == 0))])
    for dlt in range(3, WINDOW // QB + 1):
        (c_win,) = attend([win_step(dlt, c_win)])
    o_win = c_win[2] / jnp.maximum(c_win[1], TINY)
    c_sel = lax.fori_loop(0, jnp.right_shift(jnp.maximum(i - 1, 0), 1),
                          lambda kb, c: attend([sel_step(kb * KB, KB, bias_far, c)])[0], c_sel)
    o_sel = c_sel[2] / jnp.maximum(c_sel[1], TINY)

    sig_t = _sigmoid(misc_ref[:, 0:LANES]).T

    def gate(br):
        return per_head(lambda h: sig_t[8 + br * N_HEADS + h:9 + br * N_HEADS + h, :])

    o8 = gate(0) * o_cmp + gate(1) * o_sel + gate(2) * o_win
    for p in range(N_HEADS // 2):
        own = []
        for head in (2 * p, 2 * p + 1):
            g = head // NSA_HG
            own.append(o8[g * HEAD_DIM:(g + 1) * HEAD_DIM, head * QB:(head + 1) * QB])
        o_ref[:, p * LANES:(p + 1) * LANES] = jnp.concatenate(own, axis=0).T.astype(BF16)


def _nsa_prompt_t(pb, pf, comp, t5, nb, t):
    nq = t // QB
    nch = comp.shape[2]
    kernel = functools.partial(_nsa_t_kernel, n_s=-(-t // SEL_BLOCK))
    return pl.pallas_call(
        kernel,
        grid=(nb, nq),
        in_specs=[pl.BlockSpec((N_HEADS, LANES), lambda b, i: (0, 0)),
                  pl.BlockSpec((QB, HW), lambda b, i: (b * nq + i, C_QB // HW)),
                  pl.BlockSpec((1, 1, nch, LANES), lambda b, i: (0, b, 0, 0)),
                  pl.BlockSpec((1, 1, nch, LANES), lambda b, i: (1, b, 0, 0)),
                  pl.BlockSpec((t, 2 * LANES), lambda b, i: (b, C_SEL // (2 * LANES))),
                  pl.BlockSpec((t, 2 * LANES), lambda b, i: (b, C_WIN // (2 * LANES))),
                  pl.BlockSpec((QB, MISC_W), lambda b, i: (b * nq + i, C_MISC // MISC_W))],
        out_specs=pl.BlockSpec((QB, HW), lambda b, i: (b * nq + i, 0)),
        out_shape=jax.ShapeDtypeStruct((nb * t, HW), BF16),
        scratch_shapes=[pltpu.VMEM((nq, LANES, QB), BF16), pltpu.VMEM((nq, LANES, QB), BF16),
                        pltpu.VMEM((LANES, NSA_G * QB), F32)],
        compiler_params=_params("arbitrary", "arbitrary"),
        name="nsa_prompt",
    )(t5, pb, comp, comp, pb, pb, pf)


def _page_specs(npg, block, layer, tail):
    def spec(j):
        return pl.BlockSpec(block, lambda b, pt: (pt[b * npg + j], layer) + tail)
    return [spec(j) for j in range(npg)]


def _cum_sample_body(pages, m_ref, bf_ref, logf_ref):
    npg = len(pages)
    x = jnp.concatenate([pg[0, 0] for pg in pages], axis=0)
    lane = _iota(x.shape, 1)
    s = 1
    while s < PAGE:
        x = x + jnp.where(lane >= s, pltpu.roll(x, s, axis=1), 0.0)
        s *= 2
    off = jnp.zeros((N_HEADS, 1), F32)
    cum_pages = []
    for j in range(npg):
        blk = x[j * N_HEADS:(j + 1) * N_HEADS]
        cum_pages.append(blk + off)
        off = off + blk[:, PAGE - 1:PAGE]
    eye = _iota((N_HEADS, LANES), 0) == _iota((N_HEADS, LANES), 1)
    tot = jnp.sum(jnp.where(eye, off, 0.0), axis=0, keepdims=True)
    logf = _log_sigmoid(m_ref[:, 0:LANES] + bf_ref[...])
    logf_ref[...] = logf
    row = _iota(logf.shape, 0)
    c = logf
    s = 1
    while s < DEC_T:
        c = c + jnp.where(row >= s, pltpu.roll(c, s, axis=0), 0.0)
        s *= 2
    return cum_pages, c + tot


def _block_diag_q(q):
    qt = jnp.concatenate([q] * N_HEADS, axis=0)
    same = jnp.right_shift(_iota(qt.shape, 1), 6) == jnp.right_shift(_iota(qt.shape, 0), 3)
    return jnp.where(same, qt * SCALE, 0.0).astype(BF16)


def _rows_per_head(x):
    return jnp.concatenate([jnp.broadcast_to(x[h:h + 1], (DEC_T, x.shape[1])) for h in range(N_HEADS)], axis=0)


def _col_per_head(x):
    return jnp.concatenate([x[:, h:h + 1] for h in range(N_HEADS)], axis=0)


def _own_head_lanes(acc):
    lane_h = jnp.right_shift(_iota((DEC_T, HW), 1), 6)
    out = jnp.zeros((DEC_T, HW), F32)
    for h in range(N_HEADS):
        out = jnp.where(lane_h == h, acc[h * DEC_T:(h + 1) * DEC_T], out)
    return out


def _pad_rows(x, n):
    return jnp.concatenate([x, jnp.zeros((n - x.shape[0], x.shape[1]), x.dtype)], axis=0)


def _new_key_mask(strict):
    shape = (N_HEADS * DEC_T, PAGE)
    t_row = jnp.bitwise_and(_iota(shape, 0), DEC_T - 1)
    return (_iota(shape, 1) < t_row) if strict else (_iota(shape, 1) <= t_row)


def _fox_sample_body(pages, q_ref, k_ref, v_ref, cum_pages, cum_new, o_ref):
    npg = len(pages)
    qbd = _block_diag_q(q_ref[...])
    cq = _col_per_head(cum_new)
    cum_new_t = _pad_rows(cum_new, LANES).T[0:N_HEADS]
    raw = [_dot(qbd, pages[j][0, 0, 0].astype(BF16)) for j in range(npg)]
    s_pages = [raw[j] + cq - _rows_per_head(cum_pages[j]) for j in range(npg)]
    k_new = _pad_rows(k_ref[...], PAGE).astype(BF16)
    s_new = _dot_nt(qbd, k_new) + cq - _rows_per_head(cum_new_t)
    mask_new = _new_key_mask(strict=False)
    s_new = jnp.where(mask_new, s_new, NEG_INF)
    m = jnp.max(s_new, axis=-1, keepdims=True)
    for s in s_pages:
        m = jnp.maximum(m, jnp.max(s, axis=-1, keepdims=True))
    p_new = jnp.where(mask_new, jnp.exp(s_new - m), 0.0)
    l = jnp.sum(p_new, axis=-1, keepdims=True)
    acc = _dot(p_new.astype(BF16), _pad_rows(v_ref[...], PAGE).astype(BF16))
    for j in range(npg):
        pr = jnp.exp(s_pages[j] - m)
        l = l + jnp.sum(pr, axis=-1, keepdims=True)
        acc = acc + _dot_nt(pr.astype(BF16), pages[j][0, 0, 1].astype(BF16))
    o_ref[...] = _own_head_lanes(acc / jnp.maximum(l, TINY))


def _sb_sample_body(pages, q_ref, k_ref, v_ref, o_ref):
    npg = len(pages)
    qbd = _block_diag_q(q_ref[...])
    tri = _tri(PAGE)
    z = _dot_nt(qbd, _pad_rows(k_ref[...], PAGE).astype(BF16))
    mask = _new_key_mask(strict=True)
    sp = _softplus(z)
    lk = jnp.where(mask, -sp, 0.0)
    between = _split_dot(lk, tri)
    a = jnp.where(mask, jnp.exp(z - sp + between), 0.0)
    acc = _dot(a.astype(BF16), _pad_rows(v_ref[...], PAGE).astype(BF16))
    r = jnp.sum(lk, axis=-1, keepdims=True)
    rows = N_HEADS * DEC_T
    zs = [_dot(qbd, pages[j][0, 0, 0].astype(BF16)) for j in range(npg)]
    sps = [_softplus(z) for z in zs]
    within = _split_dot(jnp.concatenate([-sp for sp in sps], axis=0), tri)
    for j in reversed(range(npg)):
        a = jnp.exp(zs[j] - sps[j] + within[j * rows:(j + 1) * rows] + r)
        acc = acc + _dot_nt(a.astype(BF16), pages[j][0, 0, 1].astype(BF16))
        r = r - jnp.sum(sps[j], axis=-1, keepdims=True)
    o_ref[...] = _own_head_lanes(acc)


def _compress_sample_body(pages, w1_ref, pe_ref, w2_ref, o_ref, ch_ref, xp_ref):
    npg = len(pages)
    per_page = PAGE // CMP_STRIDE
    for j in range(npg):
        for c in range(2):
            xp_ref[2 * j + c] = pages[j][0, 0, 0, c].T
    for j in range(npg):
        for c in range(2):
            for r in range(CMP_STRIDE):
                ch_ref[c, j * per_page:(j + 1) * per_page, r * LANES:(r + 1) * LANES] = (
                    xp_ref[2 * j + c, pl.ds(r, per_page, stride=CMP_STRIDE), :])
    _compress_chunks(ch_ref, w1_ref, pe_ref, w2_ref, o_ref)


def _sample_fox_kernel(pt_ref, *refs, npg):
    logf_pages, fox_pages, cmp_pages = refs[0:npg], refs[npg:2 * npg], refs[2 * npg:3 * npg]
    (m_ref, bf_ref, q_ref, k_ref, v_ref, w1_ref, pe_ref, w2_ref,
     logf_ref, o_ref, comp_ref, ch_ref, xp_ref) = refs[3 * npg:]
    cum_pages, cum_new = _cum_sample_body(logf_pages, m_ref, bf_ref, logf_ref)
    _fox_sample_body(fox_pages, q_ref, k_ref, v_ref, cum_pages, cum_new, o_ref)
    _compress_sample_body(cmp_pages, w1_ref, pe_ref, w2_ref, comp_ref, ch_ref, xp_ref)


def _sample_fox(pt, logf_view, fox_view, nsa_view, pf, bf, w1big, pe_rows, w2p, layer, nbs, npg, row0):
    kernel = functools.partial(_sample_fox_kernel, npg=npg)
    nch = npg * PAGE // CMP_STRIDE
    rb = row0 // DEC_T
    grid_spec = pltpu.PrefetchScalarGridSpec(
        num_scalar_prefetch=1,
        grid=(nbs,),
        in_specs=_page_specs(npg, (1, 1, N_HEADS, PAGE), layer, (0, 0))
        + _page_specs(npg, (1, 1, 2, HW, PAGE), layer, (0, 0, 0))
        + _page_specs(npg, (1, 1, 1, 2, LANES, PAGE), layer, (0, 0, 0, 0))
        + [pl.BlockSpec((DEC_T, MISC_W), lambda b, pt: (rb + b, C_MISC // MISC_W)),
           pl.BlockSpec((1, LANES), lambda b, pt: (0, 0)),
           pl.BlockSpec((DEC_T, HW), lambda b, pt: (rb + b, C_QA // HW)),
           pl.BlockSpec((DEC_T, HW), lambda b, pt: (rb + b, C_KA // HW)),
           pl.BlockSpec((DEC_T, HW), lambda b, pt: (rb + b, C_VA // HW))]
        + _compress_weight_specs(layer, lambda idx: (lambda b, pt: idx)),
        out_specs=[pl.BlockSpec((DEC_T, LANES), lambda b, pt: (b, 0)),
                   pl.BlockSpec((DEC_T, HW), lambda b, pt: (b, 0)),
                   pl.BlockSpec((2, 1, nch, LANES), lambda b, pt: (0, b, 0, 0))],
        scratch_shapes=[pltpu.VMEM((2, nch, CHUNK_W), F32), pltpu.VMEM((2 * npg, PAGE, LANES), F32)],
    )
    return pl.pallas_call(
        kernel,
        grid_spec=grid_spec,
        out_shape=[jax.ShapeDtypeStruct((nbs * DEC_T, LANES), F32),
                   jax.ShapeDtypeStruct((nbs * DEC_T, HW), F32),
                   jax.ShapeDtypeStruct((2, nbs, nch, LANES), BF16)],
        compiler_params=_params("arbitrary"),
        name="sample_fox_compress",
    )(pt, *([logf_view] * npg), *([fox_view] * npg), *([nsa_view] * npg), pf, bf, pf, pf, pf, w1big, pe_rows, w2p)


def _nsa_sample_body(pages, q_ref, seln_ref, winn_ref, misc_ref, ck_ref, cv_ref, wst_ref, t5_ref, exp_ref, o_ref, n_s):
    npg = len(pages)
    past = npg * PAGE
    rows = N_HEADS * DEC_T
    nch = ck_ref.shape[2]
    wb = wst_ref.shape[4]
    lane8 = _iota((DEC_T, LANES), 1)
    t5c = t5_ref[...]

    def lut(rel):
        return _lut(t5c, _bucket(rel))

    def t_of(shape):
        return jnp.bitwise_and(_iota(shape, 0), DEC_T - 1)

    def g_rows(x):
        return jnp.concatenate([x] * NSA_HG, axis=0)

    q = q_ref[...]
    qrows = []
    for head in range(N_HEADS):
        g = head // NSA_HG
        x = q[:, (head // 2) * LANES:(head // 2 + 1) * LANES] * SCALE
        if head % 2 != g:
            x = pltpu.roll(x, HEAD_DIM, axis=1)
        qrows.append(jnp.where((lane8 >= HEAD_DIM) == (g == 1), x, 0.0))
    qbd = jnp.concatenate(qrows, axis=0).astype(BF16)
    bias_far = t5c[:, N_BUCKETS - 1:N_BUCKETS]

    raw_c = _dot_nt(qbd, ck_ref[0, 0])
    raw_pages = [_dot(qbd, pages[j][0, 0, 0, 0].astype(BF16)) for j in range(npg)]
    seln = _pad_rows(seln_ref[...], PAGE).astype(BF16)
    winn = _pad_rows(winn_ref[...], PAGE).astype(BF16)
    raw_new = _dot_nt(qbd, seln[:, 0:LANES])
    raw_w = _dot(qbd, wst_ref[0, 0, 0].astype(BF16))
    raw_wn = _dot_nt(qbd, winn[:, 0:LANES])

    qpos_c = past + t_of((rows, nch))
    rel_c = qpos_c - (_iota((rows, nch), 1) * CMP_STRIDE + (CMP_BLOCK - 1))
    mask_c = (rel_c >= 0) & (_iota((rows, nch), 1) < nch - 1)
    pc = _masked_softmax(raw_c + lut(rel_c), mask_c)
    psums = []
    for g in range(NSA_G):
        base = g * NSA_HG * DEC_T
        psum = pc[base:base + DEC_T]
        for hg in range(1, NSA_HG):
            psum = psum + pc[base + hg * DEC_T:base + (hg + 1) * DEC_T]
        psums.append(psum)

    rel_new = t_of((rows, PAGE)) - _iota((rows, PAGE), 1)
    in_new = _iota((rows, PAGE), 1) < DEC_T
    bias_new = lut(rel_new)
    rel_w = (wb + t_of((rows, wb))) - _iota((rows, wb), 1)
    mask_w = (rel_w >= 0) & (rel_w < WINDOW)
    s_w = jnp.where(mask_w, raw_w + lut(rel_w), NEG_INF)
    mask_wn = (rel_new >= 0) & (rel_new < WINDOW) & in_new
    s_wn = jnp.where(mask_wn, raw_wn + bias_new, NEG_INF)
    m = jnp.maximum(jnp.max(s_w, axis=-1, keepdims=True), jnp.max(s_wn, axis=-1, keepdims=True))
    p_w = jnp.where(mask_w, jnp.exp(s_w - m), 0.0)
    p_wn = jnp.where(mask_wn, jnp.exp(s_wn - m), 0.0)
    l_w = jnp.sum(p_w, axis=-1, keepdims=True) + jnp.sum(p_wn, axis=-1, keepdims=True)

    score = _split_dot(jnp.concatenate(psums, axis=0), _overlap(nch))
    o_cmp = _dot(pc.astype(BF16), cv_ref[0, 0])
    acc_w = _dot_nt(p_w.astype(BF16), wst_ref[0, 0, 1].astype(BF16)) + _dot(p_wn.astype(BF16), winn[:, LANES:2 * LANES])
    o_win = acc_w / jnp.maximum(l_w, TINY)

    qpos_g = past + jnp.bitwise_and(_iota((NSA_G * DEC_T, 1), 0), DEC_T - 1)
    picked = _select_blocks(score, qpos_g, n_s)
    picked = jnp.concatenate([g_rows(picked[g * DEC_T:(g + 1) * DEC_T]) for g in range(NSA_G)], axis=0)
    sel_past = _dot(picked.astype(BF16), exp_ref[...])
    last_blk = past // SEL_BLOCK
    sel_new = jnp.sum(jnp.where(_iota((rows, LANES), 1) == last_blk, picked, 0.0), axis=-1, keepdims=True) > 0.5
    s_pages, m_pages = [], []
    for j in range(npg):
        if past - (j + 1) * PAGE + 1 >= MAX_DISTANCE:
            s = raw_pages[j] + bias_far
        else:
            s = raw_pages[j] + lut(past + t_of((rows, PAGE)) - (j * PAGE + _iota((rows, PAGE), 1)))
        mk = sel_past[:, j * PAGE:(j + 1) * PAGE] > 0.5
        s_pages.append(jnp.where(mk, s, NEG_INF))
        m_pages.append(mk)
    mask_n = sel_new & (rel_new >= 0) & in_new
    s_new = jnp.where(mask_n, raw_new + bias_new, NEG_INF)
    m = jnp.max(s_new, axis=-1, keepdims=True)
    for s in s_pages:
        m = jnp.maximum(m, jnp.max(s, axis=-1, keepdims=True))
    p_new = jnp.where(mask_n, jnp.exp(s_new - m), 0.0)
    l = jnp.sum(p_new, axis=-1, keepdims=True)
    probs = []
    for j in range(npg):
        pr = jnp.where(m_pages[j], jnp.exp(s_pages[j] - m), 0.0)
        l = l + jnp.sum(pr, axis=-1, keepdims=True)
        probs.append(pr.astype(BF16))
    acc = _dot(p_new.astype(BF16), seln[:, LANES:2 * LANES])
    for j in range(npg):
        acc = acc + _dot_nt(probs[j], pages[j][0, 0, 0, 1].astype(BF16))
    o_sel = acc / jnp.maximum(l, TINY)

    sig = _sigmoid(misc_ref[:, 0:LANES])

    def gate(br):
        return jnp.concatenate(
            [sig[:, 8 + br * N_HEADS + h:9 + br * N_HEADS + h] for h in range(N_HEADS)], axis=0)

    o_all = gate(0) * o_cmp + gate(1) * o_sel + gate(2) * o_win
    pieces = []
    for head in range(N_HEADS):
        x = o_all[head * DEC_T:(head + 1) * DEC_T]
        if head % 2 != head // NSA_HG:
            x = pltpu.roll(x, HEAD_DIM, axis=1)
        pieces.append(x)
    for p in range(N_HEADS // 2):
        o_ref[:, p * LANES:(p + 1) * LANES] = jnp.where(lane8 < HEAD_DIM, pieces[2 * p], pieces[2 * p + 1])


def _sample_sb_nsa_kernel(pt_ref, *refs, npg, n_s):
    sb_pages, sel_pages = refs[0:npg], refs[npg:2 * npg]
    (qc_ref, kc_ref, vc_ref, q_ref, seln_ref, winn_ref, misc_ref, ck_ref, cv_ref, wst_ref, t5_ref, exp_ref,
     o_sb_ref, o_nsa_ref) = refs[2 * npg:]
    _sb_sample_body(sb_pages, qc_ref, kc_ref, vc_ref, o_sb_ref)
    _nsa_sample_body(sel_pages, q_ref, seln_ref, winn_ref, misc_ref, ck_ref, cv_ref, wst_ref, t5_ref, exp_ref,
                     o_nsa_ref, n_s)


def _sample_sb_nsa(pt, sb_view, nsa_view, pf, comp, win_view, t5col, expand, layer, nbs, npg, row0):
    nch = comp.shape[2]
    wb = win_view.shape[4]
    past = npg * PAGE
    kernel = functools.partial(_sample_sb_nsa_kernel, npg=npg, n_s=-(-(past + DEC_T) // SEL_BLOCK))
    rb = row0 // DEC_T
    grid_spec = pltpu.PrefetchScalarGridSpec(
        num_scalar_prefetch=1,
        grid=(nbs,),
        in_specs=_page_specs(npg, (1, 1, 2, HW, PAGE), layer, (0, 0, 0))
        + _page_specs(npg, (1, 1, 1, 2, LANES, PAGE), layer, (1, 0, 0, 0)) + [
            pl.BlockSpec((DEC_T, HW), lambda b, pt: (rb + b, C_QC // HW)),
            pl.BlockSpec((DEC_T, HW), lambda b, pt: (rb + b, C_KC // HW)),
            pl.BlockSpec((DEC_T, HW), lambda b, pt: (rb + b, C_VC // HW)),
            pl.BlockSpec((DEC_T, HW), lambda b, pt: (rb + b, C_QB // HW)),
            pl.BlockSpec((DEC_T, 2 * LANES), lambda b, pt: (rb + b, C_SEL // (2 * LANES))),
            pl.BlockSpec((DEC_T, 2 * LANES), lambda b, pt: (rb + b, C_WIN // (2 * LANES))),
            pl.BlockSpec((DEC_T, MISC_W), lambda b, pt: (rb + b, C_MISC // MISC_W)),
            pl.BlockSpec((1, 1, nch, LANES), lambda b, pt: (0, b, 0, 0)),
            pl.BlockSpec((1, 1, nch, LANES), lambda b, pt: (1, b, 0, 0)),
            pl.BlockSpec((1, 1, 2, LANES, wb), lambda b, pt: (b, layer, 0, 0, 0)),
            pl.BlockSpec((N_HEADS * DEC_T, LANES), lambda b, pt: (0, 0)),
            pl.BlockSpec((LANES, past), lambda b, pt: (0, 0))],
        out_specs=[pl.BlockSpec((DEC_T, HW), lambda b, pt: (b, 0)), pl.BlockSpec((DEC_T, HW), lambda b, pt: (b, 0))],
    )
    return pl.pallas_call(
        kernel,
        grid_spec=grid_spec,
        out_shape=[jax.ShapeDtypeStruct((nbs * DEC_T, HW), F32), jax.ShapeDtypeStruct((nbs * DEC_T, HW), F32)],
        compiler_params=_params("arbitrary"),
        name="sample_sb_nsa",
    )(pt, *([sb_view] * npg), *([nsa_view] * npg), pf, pf, pf, pf, pf, pf, pf, comp, comp, win_view, t5col, expand)


def _win_state_kernel(*refs):
    win_ref, new_refs, o_ref = refs[0], refs[1:-1], refs[-1]
    wb = win_ref.shape[4]
    lane = _iota((LANES, LANES), 1)
    for l, new_ref in enumerate(new_refs):
        for kv in range(2):
            shifted = pltpu.roll(win_ref[0, l, kv], wb - DEC_T, axis=1)
            new_t = _pad_rows(new_ref[:, kv * LANES:(kv + 1) * LANES], LANES).T
            tail = jnp.where(lane >= LANES - DEC_T, pltpu.roll(new_t, LANES - DEC_T, axis=1), shifted[:, wb - LANES:wb])
            o_ref[0, l, kv] = jnp.concatenate([shifted[:, 0:wb - LANES], tail], axis=1)


def _win_state(win_view, pfs, nbs, row0):
    _, depth, _, _, wb = win_view.shape
    rb = row0 // DEC_T
    blk = pl.BlockSpec((1, depth, 2, LANES, wb), lambda b: (b, 0, 0, 0, 0))
    return pl.pallas_call(
        _win_state_kernel,
        grid=(nbs,),
        in_specs=[blk] + [pl.BlockSpec((DEC_T, 2 * LANES), lambda b: (rb + b, C_WIN // (2 * LANES))) for _ in pfs],
        out_specs=blk,
        out_shape=jax.ShapeDtypeStruct(win_view.shape, F32),
        compiler_params=_params("arbitrary"),
        name="win_state",
    )(win_view, *pfs)


def _reorder_w_in(w_in, d):
    sizes = (HW, 2 * HW, N_HEADS, HW, 6 * NSA_G * HEAD_DIM, 3 * N_HEADS, HW, 2 * HW, 3 * d)
    offs = np.concatenate([[0], np.cumsum(sizes)])
    seg = [w_in[:, :, offs[i]:offs[i + 1]] for i in range(len(sizes))]
    q_a, kv_a, f_a, q_b, kv_b, g_b, q_c, kv_c, g_m = seg
    pad = jnp.zeros(w_in.shape[:2] + (MISC_W - N_HEADS - 3 * N_HEADS,), w_in.dtype)
    return jnp.concatenate([q_a, kv_a, q_b, q_c, kv_c, kv_b, f_a, g_b, pad, g_m], axis=-1).astype(BF16)


def kernel(x_prompt, x_sample, cache_fox_kv, cache_fox_logf, cache_nsa_kv, cache_sb_kv, state_nsa_win_kv, page_table, norm_mix_g, norm_ffn_g, norm_final_g, w_in, b_forget, t5_table, cmp_pe, cmp_w1, cmp_w2, w_out_a, w_out_b, w_out_c, w_out, router_group_w, router_group_b, router_expert_w, router_expert_b, expert_w_gate, expert_w_up, expert_w_down):
    nb, t, d = x_prompt.shape
    nbs, dec_t, _ = x_sample.shape
    depth = w_in.shape[0]
    npool = cache_fox_kv.shape[0]
    npg = page_table.shape[1]
    past = npg * PAGE
    wb = state_nsa_win_kv.shape[2]
    assert dec_t == DEC_T and t % KB == 0 and d % 128 == 0 and wb == WINDOW and past >= WINDOW
    n_p, n_s_rows = nb * t, nbs * DEC_T

    fox_view = jnp.transpose(cache_fox_kv, (0, 1, 3, 4, 5, 2)).reshape(npool, depth, 2, HW, PAGE)
    sb_view = jnp.transpose(cache_sb_kv, (0, 1, 3, 4, 5, 2)).reshape(npool, depth, 2, HW, PAGE)
    nsa_view = jnp.transpose(cache_nsa_kv, (0, 1, 3, 4, 5, 6, 2)).reshape(npool, depth, 2, 2, LANES, PAGE)
    logf_view = jnp.transpose(cache_fox_logf, (0, 1, 3, 2))
    win_view = jnp.transpose(state_nsa_win_kv, (0, 1, 3, 4, 5, 2)).reshape(nbs, depth, 2, LANES, wb)
    pt = page_table.reshape(-1).astype(jnp.int32)

    w_in_r = _reorder_w_in(w_in, d)
    bf_pad = jnp.pad(b_forget.astype(F32), ((0, 0), (0, LANES - N_HEADS))).reshape(depth, 1, LANES)
    t5 = jnp.pad(t5_table.astype(F32).T, ((0, 0), (0, LANES - N_BUCKETS)))
    t5col = jnp.repeat(t5, DEC_T, axis=0)
    expand = jnp.asarray(np.arange(LANES)[:, None] == (np.arange(past)[None, :] // SEL_BLOCK), BF16)
    w1r = cmp_w1.reshape(depth, 2, 2, CMP_STRIDE, HEAD_DIM, CMP_HIDDEN)
    w1big = jnp.einsum("zchldk,gG->zclgdhGk", w1r, jnp.eye(NSA_G, dtype=w1r.dtype)).reshape(
        depth, 2, CHUNK_W, 2 * NSA_G * CMP_HIDDEN).astype(BF16)
    pe_rows = jnp.broadcast_to(cmp_pe.reshape(depth, 2, 2, CMP_STRIDE, 1, HEAD_DIM),
                               (depth, 2, 2, CMP_STRIDE, NSA_G, HEAD_DIM)).reshape(depth, 2, 2, CHUNK_W)
    pe_rows = jnp.pad(pe_rows, ((0, 0), (0, 0), (0, 6), (0, 0))).astype(F32)
    w2p = jnp.stack([jnp.pad(cmp_w2, ((0, 0), (0, 0), (0, 0), (g * HEAD_DIM, LANES - (g + 1) * HEAD_DIM)))
                     for g in range(NSA_G)], axis=2).astype(BF16)
    w_router = jnp.pad(jnp.concatenate([router_group_w, router_expert_w], axis=-1),
                       ((0, 0), (0, 0), (0, LANES - N_GROUPS - N_EXPERTS))).astype(F32)
    b_router = jnp.pad(jnp.concatenate([router_group_b, router_expert_b], axis=-1),
                       ((0, 0), (0, LANES - N_GROUPS - N_EXPERTS))).astype(F32).reshape(depth, 1, LANES)
    wa, wb_, wc, wo = (w.astype(BF16) for w in (w_out_a, w_out_b, w_out_c, w_out))
    wg, wu, wd = (w.astype(BF16) for w in (expert_w_gate, expert_w_up, expert_w_down))

    h = jnp.concatenate([x_prompt.reshape(n_p, d), x_sample.reshape(n_s_rows, d)], axis=0)
    st_p = [[] for _ in range(5)]
    st_s = [[] for _ in range(5)]
    pfs = []
    for l in range(depth):
        pf, pb = _proj(h, norm_mix_g[l], w_in_r, l)

        logf_p, cum_p = _cum_prompt(pf, bf_pad[l], nb, t)
        cum_t = jnp.transpose(cum_p[:, :N_HEADS].reshape(nb, t, N_HEADS), (0, 2, 1)).reshape(nb, N_HEADS, t // QB, QB)
        o_a_p = _fox_prompt_t(pb, cum_p, cum_t, nb, t)
        o_c_p = _sb_prompt_t(pb, nb, t)
        comp_p = _compress_prompt(pf, w1big, pe_rows, w2p, l, nb, t)
        o_b_p = _nsa_prompt_t(pb, pf, comp_p, t5, nb, t)

        logf_s, o_a_s, comp_s = _sample_fox(pt, logf_view, fox_view, nsa_view, pf, bf_pad[l], w1big, pe_rows, w2p,
                                            l, nbs, npg, n_p)
        o_c_s, o_b_s = _sample_sb_nsa(pt, sb_view, nsa_view, pf, comp_s, win_view, t5col, expand, l, nbs, npg, n_p)

        o_a = jnp.concatenate([o_a_p, o_a_s.astype(BF16)], axis=0)
        o_b = jnp.concatenate([o_b_p, o_b_s.astype(BF16)], axis=0)
        o_c = jnp.concatenate([o_c_p, o_c_s.astype(BF16)], axis=0)
        mixed = _merge(o_a, o_b, o_c, pf, wa, wb_, wc, d, l)
        h = _mm_res(mixed, wo, h, l)
        xn, comb = _router(h, norm_ffn_g[l], w_router[l], b_router[l])
        h = _moe(xn, comb, h, wg, wu, wd, l)

        def rows(c0, width, shape, lo, hi):
            return pf[lo:hi, c0:c0 + width].reshape(shape)

        win_new_p = rows(C_WIN, 2 * LANES, (nb, t, 2, NSA_G, HEAD_DIM), 0, n_p)
        pfs.append(pf)
        st_p[0].append(rows(C_KA, 2 * HW, (nb, t, 2, N_HEADS, HEAD_DIM), 0, n_p))
        st_p[1].append(logf_p[:, :N_HEADS].reshape(nb, t, N_HEADS))
        st_p[2].append(rows(C_CMP, 4 * LANES, (nb, t, 2, 2, NSA_G, HEAD_DIM), 0, n_p))
        st_p[3].append(rows(C_KC, 2 * HW, (nb, t, 2, N_HEADS, HEAD_DIM), 0, n_p))
        st_p[4].append(win_new_p[:, t - min(WINDOW, t):])
        st_s[0].append(rows(C_KA, 2 * HW, (nbs, DEC_T, 2, N_HEADS, HEAD_DIM), n_p, n_p + n_s_rows))
        st_s[1].append(logf_s[:, :N_HEADS].reshape(nbs, DEC_T, N_HEADS))
        st_s[2].append(rows(C_CMP, 4 * LANES, (nbs, DEC_T, 2, 2, NSA_G, HEAD_DIM), n_p, n_p + n_s_rows))
        st_s[3].append(rows(C_KC, 2 * HW, (nbs, DEC_T, 2, N_HEADS, HEAD_DIM), n_p, n_p + n_s_rows))

    y = _final_norm(h, norm_final_g)
    y_prompt = y[:n_p].reshape(nb, t, d)
    y_sample = y[n_p:].reshape(nbs, DEC_T, d)
    sp = [jnp.stack(s, axis=1) for s in st_p]
    ss = [jnp.stack(s, axis=1) for s in st_s[:4]]
    win_s = _win_state(win_view, pfs, nbs, n_p).reshape(nbs, depth, 2, NSA_G, HEAD_DIM, wb)
    win_s = jnp.transpose(win_s, (0, 1, 5, 2, 3, 4))
    return (y_prompt, y_sample, sp[0], ss[0], sp[1], ss[1], sp[2], ss[2], sp[3], ss[3], sp[4], win_s)
```

```python
import functools
import math

import numpy as np
import jax
import jax.numpy as jnp
from jax import lax
from jax.experimental import pallas as pl
from jax.experimental.pallas import tpu as pltpu

F32 = jnp.float32
BF16 = jnp.bfloat16

HEAD_DIM = 64
N_HEADS = 8
NSA_G = 2
NSA_HG = N_HEADS // NSA_G
HW = N_HEADS * HEAD_DIM
PAGE = 128
DEC_T = 8
CMP_BLOCK = 32
CMP_STRIDE = 16
CMP_HIDDEN = 128
SEL_BLOCK = 64
SEL_TOPK = 8
WINDOW = 512
N_BUCKETS = 32
MAX_DISTANCE = 128
N_GROUPS = 4
EXPERTS_PER_GROUP = 4
N_EXPERTS = N_GROUPS * EXPERTS_PER_GROUP
RMS_EPS = 1e-6
NEG_INF = -1e30
FORCE_SCORE = 1e9
BELOW_ALL = -3e38
SCALE = HEAD_DIM ** -0.5
TINY = float(np.finfo(np.float32).tiny)
QB = 128
KB = 256
MOE_TILE = 768
MOE_CHUNK = 256
LANES = 128
VMEM_LIMIT = 56 * 1024 * 1024

C_QA, C_KA, C_VA, C_QB, C_QC, C_KC, C_VC = 0, 512, 1024, 1536, 2048, 2560, 3072
C_CMP, C_SEL, C_WIN, C_MISC, C_GM = 3584, 3840, 4096, 4352, 4608
MISC_W = 256
CHUNK_W = CMP_STRIDE * LANES


def _params(*sem):
    return pltpu.CompilerParams(dimension_semantics=sem, vmem_limit_bytes=VMEM_LIMIT)


def _pick(n, cap, mult):
    t = (min(cap, n) // mult) * mult
    while t > 0 and n % t:
        t -= mult
    assert t > 0, (n, cap, mult)
    return t


def _iota(shape, axis):
    return lax.broadcasted_iota(jnp.int32, shape, axis)


def _dot(a, b):
    return jnp.dot(a, b, preferred_element_type=F32)


def _dot_nt(a, b):
    return lax.dot_general(a, b, (((1,), (1,)), ((), ())), preferred_element_type=F32)


def _softplus(x):
    return jnp.maximum(x, 0.0) + jnp.log(1.0 + jnp.exp(-jnp.abs(x)))


def _log_sigmoid(x):
    return jnp.minimum(x, 0.0) - jnp.log1p(jnp.exp(-jnp.abs(x)))


def _sigmoid(x):
    return 1.0 / (1.0 + jnp.exp(-x))


def _split_dot(x, w):
    hi = x.astype(BF16)
    lo = (x - hi.astype(F32)).astype(BF16)
    return _dot(hi, w) + _dot(lo, w)


def _tri(w):
    return jnp.where(_iota((w, w), 0) > _iota((w, w), 1), 1.0, 0.0).astype(BF16)


def _bucket(rel):
    n = jnp.maximum(rel, 0)
    exact = N_BUCKETS // 2
    far = jnp.log(jnp.maximum(n, exact).astype(F32) / exact) / math.log(MAX_DISTANCE / exact)
    far = exact + (far * (N_BUCKETS - exact)).astype(jnp.int32)
    return jnp.where(n < exact, n, jnp.minimum(far, N_BUCKETS - 1))


def _masked_softmax(logits, mask):
    l = jnp.where(mask, logits, NEG_INF)
    e = jnp.where(mask, jnp.exp(l - jnp.max(l, axis=-1, keepdims=True)), 0.0)
    return e / jnp.maximum(jnp.sum(e, axis=-1, keepdims=True), TINY)


def _overlap(nch):
    cs = _iota((nch, LANES), 0) * CMP_STRIDE
    ss = _iota((nch, LANES), 1) * SEL_BLOCK
    ov = jnp.maximum(jnp.minimum(cs + CMP_BLOCK, ss + SEL_BLOCK) - jnp.maximum(cs, ss), 0)
    return (ov.astype(F32) / CMP_BLOCK).astype(BF16)


def _select_blocks(score, qpos, n_s):
    lane = _iota(score.shape, 1)
    lane_f = lane.astype(F32)
    valid = lane * SEL_BLOCK <= qpos
    forced = (lane == jnp.right_shift(qpos, 6)) | (lane == 0)
    sc = jnp.where(forced, FORCE_SCORE, jnp.where(valid, score, NEG_INF))
    sc = jnp.where(lane < n_s, sc, BELOW_ALL)
    rank = jnp.zeros(score.shape, F32)
    for k in range(n_s):
        other = sc[:, k:k + 1]
        rank = rank + jnp.where((other > sc) | ((other == sc) & (lane > k)), 1.0, 0.0)
    return jnp.where((rank < min(SEL_TOPK, n_s)) & (lane < n_s), 1.0, 0.0)


def _proj_kernel(x_ref, g_ref, w_ref, of_ref, ob_ref, xn_ref):
    @pl.when(pl.program_id(1) == 0)
    def _():
        x = x_ref[...]
        ms = jnp.mean(x * x, axis=-1, keepdims=True)
        xn_ref[...] = (x * lax.rsqrt(ms + RMS_EPS) * g_ref[...]).astype(BF16)

    y = _dot(xn_ref[...], w_ref[...])
    of_ref[...] = y
    ob_ref[...] = y.astype(BF16)


def _proj(h, g, w, layer):
    n, d = h.shape
    wp = w.shape[2]
    tm = _pick(n, 1024, 16)
    tn = _pick(wp, 768, 128)
    return pl.pallas_call(
        _proj_kernel,
        grid=(n // tm, wp // tn),
        in_specs=[pl.BlockSpec((tm, d), lambda i, j: (i, 0)),
                  pl.BlockSpec((1, d), lambda i, j: (0, 0)),
                  pl.BlockSpec((None, d, tn), lambda i, j: (layer, 0, j))],
        out_specs=[pl.BlockSpec((tm, tn), lambda i, j: (i, j)),
                   pl.BlockSpec((tm, tn), lambda i, j: (i, j))],
        out_shape=[jax.ShapeDtypeStruct((n, wp), F32), jax.ShapeDtypeStruct((n, wp), BF16)],
        scratch_shapes=[pltpu.VMEM((tm, d), BF16)],
        compiler_params=_params("arbitrary", "arbitrary"),
        name="proj",
    )(h, g.reshape(1, d), w)


def _merge_kernel(oa_ref, ob_ref, oc_ref, wa_ref, wb_ref, wc_ref, g0_ref, g1_ref, g2_ref, o_ref):
    m = _sigmoid(g0_ref[...]) * _dot(oa_ref[...], wa_ref[...])
    m = m + _sigmoid(g1_ref[...]) * _dot(ob_ref[...], wb_ref[...])
    m = m + _sigmoid(g2_ref[...]) * _dot(oc_ref[...], wc_ref[...])
    o_ref[...] = m.astype(BF16)


def _merge(o_a, o_b, o_c, pf, wa, wb, wc, d, layer):
    n = o_a.shape[0]
    tm = _pick(n, 512, 16)
    tn = _pick(d, 512, 128)
    gm0 = C_GM // tn
    o_spec = pl.BlockSpec((tm, HW), lambda i, j: (i, 0))
    w_spec = pl.BlockSpec((None, HW, tn), lambda i, j: (layer, 0, j))

    def g_spec(k):
        return pl.BlockSpec((tm, tn), lambda i, j: (i, gm0 + k * (d // tn) + j))

    return pl.pallas_call(
        _merge_kernel,
        grid=(n // tm, d // tn),
        in_specs=[o_spec, o_spec, o_spec, w_spec, w_spec, w_spec, g_spec(0), g_spec(1), g_spec(2)],
        out_specs=pl.BlockSpec((tm, tn), lambda i, j: (i, j)),
        out_shape=jax.ShapeDtypeStruct((n, d), BF16),
        compiler_params=_params("arbitrary", "arbitrary"),
        name="merge",
    )(o_a, o_b, o_c, wa, wb, wc, pf, pf, pf)


def _mm_res_kernel(x_ref, w_ref, r_ref, o_ref):
    o_ref[...] = r_ref[...] + _dot(x_ref[...], w_ref[...])


def _mm_res(x, w, res, layer):
    n, k = x.shape
    d = w.shape[2]
    tm = _pick(n, 512, 16)
    tn = _pick(d, 512, 128)
    return pl.pallas_call(
        _mm_res_kernel,
        grid=(n // tm, d // tn),
        in_specs=[pl.BlockSpec((tm, k), lambda i, j: (i, 0)),
                  pl.BlockSpec((None, k, tn), lambda i, j: (layer, 0, j)),
                  pl.BlockSpec((tm, tn), lambda i, j: (i, j))],
        out_specs=pl.BlockSpec((tm, tn), lambda i, j: (i, j)),
        out_shape=jax.ShapeDtypeStruct((n, d), F32),
        compiler_params=_params("arbitrary", "arbitrary"),
        name="out_proj",
    )(x, w, res)


def _router_kernel(h_ref, g_ref, w_ref, b_ref, xn_ref, comb_ref, mcol_ref, mrow_ref, cnt_ref):
    x = h_ref[...]
    ms = jnp.mean(x * x, axis=-1, keepdims=True)
    xn = x * lax.rsqrt(ms + RMS_EPS) * g_ref[...]
    xn_ref[...] = xn.astype(BF16)
    logits = jnp.dot(xn, w_ref[...], precision=lax.Precision.HIGHEST, preferred_element_type=F32) + b_ref[...]
    lane = _iota(logits.shape, 1)
    lane_f = lane.astype(F32)
    is_grp = lane < N_GROUPS
    gl = jnp.where(is_grp, logits, BELOW_ALL)
    gmax = jnp.max(gl, axis=-1, keepdims=True)
    gsum = jnp.sum(jnp.where(is_grp, jnp.exp(gl - gmax), 0.0), axis=-1, keepdims=True)
    w_grp = 1.0 / gsum
    g_star = jnp.min(jnp.where(is_grp & (gl == gmax), lane_f, 1e9), axis=-1, keepdims=True)
    lo = N_GROUPS + g_star * EXPERTS_PER_GROUP
    in_grp = (lane_f >= lo) & (lane_f < lo + EXPERTS_PER_GROUP)
    el = jnp.where(in_grp, logits, BELOW_ALL)
    v1 = jnp.max(el, axis=-1, keepdims=True)
    i1 = jnp.min(jnp.where(el == v1, lane_f, 1e9), axis=-1, keepdims=True)
    el2 = jnp.where(lane_f == i1, BELOW_ALL, el)
    v2 = jnp.max(el2, axis=-1, keepdims=True)
    i2 = jnp.min(jnp.where(el2 == v2, lane_f, 1e9), axis=-1, keepdims=True)
    e2 = jnp.exp(v2 - v1)
    den = 1.0 + e2
    comb = jnp.where(lane_f == i1, w_grp / den, 0.0) + jnp.where(lane_f == i2, w_grp * e2 / den, 0.0)
    comb_ref[...] = comb
    member = jnp.where(lane_f == g_star, 1.0, 0.0)
    tm = member.shape[0]
    row = _iota(member.shape, 0)
    seen = member
    s = 1
    while s < tm:
        seen = seen + jnp.where(row >= s, pltpu.roll(seen, s, axis=0), 0.0)
        s *= 2
    rank = jnp.sum((seen - member) * member, axis=-1, keepdims=True)
    meta = jnp.where(lane == 0, g_star, jnp.where(lane == 1, rank, 0.0))
    mcol_ref[...] = meta
    mrow_ref[0] = meta.T[0:8]
    cnt_ref[0] = jnp.broadcast_to(seen[tm - 1:tm, :], (8, LANES))


def _router(h, g, w, b):
    n, d = h.shape
    tm = _pick(n, MOE_TILE, LANES)
    nt = n // tm
    return pl.pallas_call(
        _router_kernel,
        grid=(nt,),
        in_specs=[pl.BlockSpec((tm, d), lambda i: (i, 0)),
                  pl.BlockSpec((1, d), lambda i: (0, 0)),
                  pl.BlockSpec((d, LANES), lambda i: (0, 0)),
                  pl.BlockSpec((1, LANES), lambda i: (0, 0))],
        out_specs=[pl.BlockSpec((tm, d), lambda i: (i, 0)),
                   pl.BlockSpec((tm, LANES), lambda i: (i, 0)),
                   pl.BlockSpec((tm, LANES), lambda i: (i, 0)),
                   pl.BlockSpec((1, 8, tm), lambda i: (i, 0, 0)),
                   pl.BlockSpec((1, 8, LANES), lambda i: (i, 0, 0))],
        out_shape=[jax.ShapeDtypeStruct((n, d), BF16), jax.ShapeDtypeStruct((n, LANES), F32),
                   jax.ShapeDtypeStruct((n, LANES), F32), jax.ShapeDtypeStruct((nt, 8, tm), F32),
                   jax.ShapeDtypeStruct((nt, 8, LANES), F32)],
        compiler_params=_params("arbitrary"),
        name="router",
    )(h, g.reshape(1, d), w, b)


def _moe_kernel(cnt_ref, x_ref, c_ref, mcol_ref, mrow_ref, h_ref, wg_ref, wu_ref, wd_ref, o_ref, xc_ref, cc_ref, y_ref, *, ch):
    i, e = pl.program_id(0), pl.program_id(1)
    tm = x_ref.shape[0]
    grp = jnp.right_shift(e, 2)
    grp_f = grp.astype(F32)
    nchunk = (cnt_ref[i * N_GROUPS + grp] + ch - 1) // ch

    @pl.when(e == 0)
    def _():
        o_ref[...] = h_ref[...]

    @pl.when(jnp.bitwise_and(e, EXPERTS_PER_GROUP - 1) == 0)
    def _():
        grp_row, rank_row = mrow_ref[0, 0:1, :], mrow_ref[0, 1:2, :]
        comb = c_ref[...]
        c1 = comb.astype(BF16)
        rest = comb - c1.astype(F32)
        c2 = rest.astype(BF16)
        c3 = (rest - c2.astype(F32)).astype(BF16)

        def gather(c, _):
            base = pl.multiple_of(c * ch, ch)
            slot = (base + _iota((ch, tm), 0)).astype(F32)
            pick = jnp.where((grp_row == grp_f) & (rank_row == slot), 1.0, 0.0).astype(BF16)
            xc_ref[pl.ds(base, ch), :] = _dot(pick, x_ref[...]).astype(BF16)
            cc_ref[pl.ds(base, ch), :] = _dot(pick, c1) + _dot(pick, c2) + _dot(pick, c3)
            y_ref[pl.ds(base, ch), :] = jnp.zeros((ch, y_ref.shape[1]), F32)
            return 0

        lax.fori_loop(0, nchunk, gather, 0)

    def expert(c, _):
        base = pl.multiple_of(c * ch, ch)
        xc = xc_ref[pl.ds(base, ch), :]
        cc = cc_ref[pl.ds(base, ch), :]
        ce = jnp.sum(jnp.where(_iota(cc.shape, 1) == e + N_GROUPS, cc, 0.0), axis=-1, keepdims=True)
        gate = _dot(xc, wg_ref[0])
        up = _dot(xc, wu_ref[0])
        hh = gate * _sigmoid(gate) * up * ce
        y_ref[pl.ds(base, ch), :] += _dot(hh.astype(BF16), wd_ref[0])
        return 0

    lax.fori_loop(0, nchunk, expert, 0)

    @pl.when(jnp.bitwise_and(e, EXPERTS_PER_GROUP - 1) == EXPERTS_PER_GROUP - 1)
    def _():
        grp_col, rank_col = mcol_ref[:, 0:1], mcol_ref[:, 1:2]

        def scatter(c, _):
            base = pl.multiple_of(c * ch, ch)
            slot = (base + _iota((tm, ch), 1)).astype(F32)
            place = jnp.where((grp_col == grp_f) & (rank_col == slot), 1.0, 0.0).astype(BF16)
            y = y_ref[pl.ds(base, ch), :]
            hi = y.astype(BF16)
            lo = (y - hi.astype(F32)).astype(BF16)
            o_ref[...] += _dot(place, hi) + _dot(place, lo)
            return 0

        lax.fori_loop(0, nchunk, scatter, 0)


def _moe(xn, comb, mcol, mrow, counts, h, wg, wu, wd, layer):
    n, d = h.shape
    _, ne, _, ff = wg.shape
    nt, _, tm = mrow.shape
    ch = min(MOE_CHUNK, tm)
    cap = -(-tm // ch) * ch
    kernel = functools.partial(_moe_kernel, ch=ch)
    grid_spec = pltpu.PrefetchScalarGridSpec(
        num_scalar_prefetch=1,
        grid=(nt, ne),
        in_specs=[pl.BlockSpec((tm, d), lambda i, e, cnt: (i, 0)),
                  pl.BlockSpec((tm, LANES), lambda i, e, cnt: (i, 0)),
                  pl.BlockSpec((tm, LANES), lambda i, e, cnt: (i, 0)),
                  pl.BlockSpec((1, 8, tm), lambda i, e, cnt: (i, 0, 0)),
                  pl.BlockSpec((tm, d), lambda i, e, cnt: (i, 0), pipeline_mode=pl.Buffered(1)),
                  pl.BlockSpec((None, 1, d, ff), lambda i, e, cnt: (layer, e, 0, 0)),
                  pl.BlockSpec((None, 1, d, ff), lambda i, e, cnt: (layer, e, 0, 0)),
                  pl.BlockSpec((None, 1, ff, d), lambda i, e, cnt: (layer, e, 0, 0))],
        out_specs=pl.BlockSpec((tm, d), lambda i, e, cnt: (i, 0), pipeline_mode=pl.Buffered(1)),
        scratch_shapes=[pltpu.VMEM((cap, d), BF16), pltpu.VMEM((cap, LANES), F32), pltpu.VMEM((cap, d), F32)],
    )
    return pl.pallas_call(
        kernel,
        grid_spec=grid_spec,
        out_shape=jax.ShapeDtypeStruct((n, d), F32),
        compiler_params=_params("arbitrary", "arbitrary"),
        name="moe",
    )(counts, xn, comb, mcol, mrow, h, wg, wu, wd)


def _final_norm_kernel(h_ref, g_ref, o_ref):
    x = h_ref[...]
    ms = jnp.mean(x * x, axis=-1, keepdims=True)
    o_ref[...] = x * lax.rsqrt(ms + RMS_EPS) * g_ref[...]


def _final_norm(h, g):
    n, d = h.shape
    tm = _pick(n, 512, 8)
    return pl.pallas_call(
        _final_norm_kernel,
        grid=(n // tm,),
        in_specs=[pl.BlockSpec((tm, d), lambda i: (i, 0)), pl.BlockSpec((1, d), lambda i: (0, 0))],
        out_specs=pl.BlockSpec((tm, d), lambda i: (i, 0)),
        out_shape=jax.ShapeDtypeStruct((n, d), F32),
        compiler_params=_params("arbitrary"),
        name="final_norm",
    )(h, g.reshape(1, d))


def _cum_prompt_kernel(m_ref, bf_ref, logf_ref, cum_ref):
    logf = _log_sigmoid(m_ref[:, 0:LANES] + bf_ref[...])
    logf_ref[...] = logf
    t = logf.shape[0]
    row = _iota(logf.shape, 0)
    c = logf
    s = 1
    while s < t:
        c = c + jnp.where(row >= s, pltpu.roll(c, s, axis=0), 0.0)
        s *= 2
    cum_ref[...] = c


def _cum_prompt(pf, bf, nb, t):
    return pl.pallas_call(
        _cum_prompt_kernel,
        grid=(nb,),
        in_specs=[pl.BlockSpec((t, MISC_W), lambda b: (b, C_MISC // MISC_W)),
                  pl.BlockSpec((1, LANES), lambda b: (0, 0))],
        out_specs=[pl.BlockSpec((t, LANES), lambda b: (b, 0)), pl.BlockSpec((t, LANES), lambda b: (b, 0))],
        out_shape=[jax.ShapeDtypeStruct((nb * t, LANES), F32), jax.ShapeDtypeStruct((nb * t, LANES), F32)],
        compiler_params=_params("arbitrary"),
        name="cum_prompt",
    )(pf, bf)


def _stack_pair(q_ref, p, lane):
    qp = q_ref[:, p * LANES:(p + 1) * LANES] * jnp.asarray(SCALE, BF16)
    zero = jnp.zeros_like(qp)
    return jnp.concatenate([jnp.where(lane < HEAD_DIM, qp, zero), jnp.where(lane >= HEAD_DIM, qp, zero)], axis=0)


def _compress_chunks(ch_ref, w1_ref, pe_ref, w2_ref, o_ref):
    nch = ch_ref.shape[1]
    hw = NSA_G * CMP_HIDDEN
    for c in range(2):
        w1 = w1_ref[0, c]
        pe = _dot(pe_ref[0, c].astype(BF16), w1)
        y = _dot(ch_ref[c].astype(BF16), w1)
        out = None
        for g in range(NSA_G):
            lo = slice(g * CMP_HIDDEN, (g + 1) * CMP_HIDDEN)
            hi = slice(hw + g * CMP_HIDDEN, hw + (g + 1) * CMP_HIDDEN)
            pre = y[:, lo] + pltpu.roll(y[:, hi], nch - 1, axis=0) + pe[0:1, lo] + pe[1:2, hi]
            hid = 0.5 * pre * (1.0 + jnp.tanh(math.sqrt(2.0 / math.pi) * (pre + 0.044715 * pre * pre * pre)))
            cg = _dot(hid.astype(BF16), w2_ref[0, c, g])
            out = cg if out is None else out + cg
        o_ref[c, 0] = out.astype(BF16)


def _compress_weight_specs(layer, imap):
    return [pl.BlockSpec((1, 2, CHUNK_W, 2 * NSA_G * CMP_HIDDEN), imap((layer, 0, 0, 0))),
            pl.BlockSpec((1, 2, 8, CHUNK_W), imap((layer, 0, 0, 0))),
            pl.BlockSpec((1, 2, NSA_G, CMP_HIDDEN, LANES), imap((layer, 0, 0, 0, 0)))]


def _compress_prompt_kernel(xk_ref, xv_ref, w1_ref, pe_ref, w2_ref, o_ref, ch_ref):
    nch = ch_ref.shape[1]
    for c, x_ref in enumerate((xk_ref, xv_ref)):
        for r in range(CMP_STRIDE):
            ch_ref[c, :, r * LANES:(r + 1) * LANES] = x_ref[pl.ds(r, nch, stride=CMP_STRIDE), :]
    _compress_chunks(ch_ref, w1_ref, pe_ref, w2_ref, o_ref)


def _compress_prompt(pf, w1big, pe_rows, w2p, layer, nb, t):
    nch = t // CMP_STRIDE
    return pl.pallas_call(
        _compress_prompt_kernel,
        grid=(nb,),
        in_specs=[pl.BlockSpec((t, LANES), lambda b: (b, C_CMP // LANES)),
                  pl.BlockSpec((t, LANES), lambda b: (b, C_CMP // LANES + 1))]
        + _compress_weight_specs(layer, lambda idx: (lambda b: idx)),
        out_specs=pl.BlockSpec((2, 1, nch, LANES), lambda b: (0, b, 0, 0)),
        out_shape=jax.ShapeDtypeStruct((2, nb, nch, LANES), BF16),
        scratch_shapes=[pltpu.VMEM((2, nch, CHUNK_W), F32)],
        compiler_params=_params("arbitrary"),
        name="nsa_compress_prompt",
    )(pf, pf, w1big, pe_rows, w2p)


def _lut(tab, bk):
    parts = [jnp.take_along_axis(tab, bk[:, c:c + LANES], axis=1) for c in range(0, bk.shape[1], LANES)]
    return parts[0] if len(parts) == 1 else jnp.concatenate(parts, axis=1)


def _transpose_bf16(x):
    return x.astype(F32).T.astype(BF16)


def _pair_out(o_t, lane_dtype=BF16):
    own = jnp.concatenate([o_t[0:HEAD_DIM, 0:QB], o_t[HEAD_DIM:LANES, QB:2 * QB]], axis=0)
    return own.T.astype(lane_dtype)


def _fox_t_kernel(q_ref, k_ref, v_ref, cum_ref, cq_ref, o_ref, vt_ref, ckb_ref):
    i = pl.program_id(1)
    npair = N_HEADS // 2
    nkb = vt_ref.shape[0]

    @pl.when(i == 0)
    def _():
        for kb in range(nkb):
            for p in range(npair):
                vt_ref[kb, p * LANES:(p + 1) * LANES, :] = _transpose_bf16(v_ref[kb * KB:(kb + 1) * KB, p * LANES:(p + 1) * LANES])
        for h in range(N_HEADS):
            ckb_ref[h] = jnp.broadcast_to(cum_ref[:, h:h + 1], ckb_ref.shape[1:])

    lane = _iota((QB, LANES), 1)
    qrow = i * QB + jnp.bitwise_and(_iota((1, 2 * QB), 1), QB - 1)
    krow = _iota((KB, 1), 0)
    q2 = [_stack_pair(q_ref, p, lane) for p in range(npair)]
    cq2 = [jnp.concatenate([cq_ref[0, 2 * p, pl.ds(i, 1), :], cq_ref[0, 2 * p + 1, pl.ds(i, 1), :]], axis=1)
           for p in range(npair)]

    def step(kb, carry, masked):
        k0 = pl.multiple_of(kb * KB, KB)
        scores = [_dot_nt(k_ref[pl.ds(k0, KB), p * LANES:(p + 1) * LANES], q2[p]) for p in range(npair)]
        probs, stats = [], []
        for p in range(npair):
            m, l, _ = carry[p]
            ck = jnp.concatenate([ckb_ref[2 * p, pl.ds(k0, KB), :], ckb_ref[2 * p + 1, pl.ds(k0, KB), :]], axis=1)
            s = scores[p] + cq2[p] - ck
            if masked:
                s = jnp.where((k0 + krow) <= qrow, s, NEG_INF)
            m_new = jnp.maximum(m, jnp.max(s, axis=0, keepdims=True))
            pr = jnp.exp(s - m_new)
            alpha = jnp.exp(m - m_new)
            probs.append(pr.astype(BF16))
            stats.append((m_new, alpha * l + jnp.sum(pr, axis=0, keepdims=True), alpha))
        out = []
        for p in range(npair):
            m_new, l, alpha = stats[p]
            acc = alpha * carry[p][2] + _dot(vt_ref[kb, p * LANES:(p + 1) * LANES, :], probs[p])
            out.append((m_new, l, acc))
        return tuple(out)

    init = tuple((jnp.full((1, 2 * QB), NEG_INF, F32), jnp.zeros((1, 2 * QB), F32), jnp.zeros((LANES, 2 * QB), F32))
                 for _ in range(npair))
    last = (i * QB) // KB
    carry = lax.fori_loop(0, last, lambda kb, c: step(kb, c, False), init)
    carry = step(last, carry, True)
    for p in range(npair):
        m, l, acc = carry[p]
        o_ref[:, p * LANES:(p + 1) * LANES] = _pair_out(acc / jnp.maximum(l, TINY))


def _fox_prompt_t(pb, cum, cum_t, nb, t):
    nq = t // QB
    return pl.pallas_call(
        _fox_t_kernel,
        grid=(nb, nq),
        in_specs=[pl.BlockSpec((QB, HW), lambda b, i: (b * nq + i, C_QA // HW)),
                  pl.BlockSpec((t, HW), lambda b, i: (b, C_KA // HW)),
                  pl.BlockSpec((t, HW), lambda b, i: (b, C_VA // HW)),
                  pl.BlockSpec((t, LANES), lambda b, i: (b, 0)),
                  pl.BlockSpec((1, N_HEADS, nq, QB), lambda b, i: (b, 0, 0, 0))],
        out_specs=pl.BlockSpec((QB, HW), lambda b, i: (b * nq + i, 0)),
        out_shape=jax.ShapeDtypeStruct((nb * t, HW), BF16),
        scratch_shapes=[pltpu.VMEM((t // KB, HW, KB), BF16), pltpu.VMEM((N_HEADS, t, LANES), F32)],
        compiler_params=_params("arbitrary", "arbitrary"),
        name="fox_prompt",
    )(pb, pb, pb, cum, cum_t)


def _sb_t_kernel(q_ref, k_ref, v_ref, o_ref, vt_ref):
    i = pl.program_id(1)
    npair = N_HEADS // 2
    nkb = vt_ref.shape[0]

    @pl.when(i == 0)
    def _():
        for kb in range(nkb):
            for p in range(npair):
                vt_ref[kb, p * LANES:(p + 1) * LANES, :] = _transpose_bf16(v_ref[kb * KB:(kb + 1) * KB, p * LANES:(p + 1) * LANES])

    lane = _iota((QB, LANES), 1)
    qrow = i * QB + jnp.bitwise_and(_iota((1, 2 * QB), 1), QB - 1)
    krow = _iota((KB, 1), 0)
    later = jnp.where(_iota((KB, KB), 1) > _iota((KB, KB), 0), 1.0, 0.0).astype(BF16)
    q2 = [_stack_pair(q_ref, p, lane) for p in range(npair)]

    def step(kb, carry, masked):
        k0 = pl.multiple_of(kb * KB, KB)
        vis = (k0 + krow) < qrow
        zs = [_dot_nt(k_ref[pl.ds(k0, KB), p * LANES:(p + 1) * LANES], q2[p]) for p in range(npair)]
        sps = [_softplus(z) for z in zs]
        lks = [jnp.where(vis, -sp, 0.0) if masked else -sp for sp in sps]
        his = [lk.astype(BF16) for lk in lks]
        los = [(lk - hi.astype(F32)).astype(BF16) for lk, hi in zip(lks, his)]
        betweens = [_dot(later, hi) + _dot(later, lo) for hi, lo in zip(his, los)]
        weights = []
        for p in range(npair):
            a = jnp.exp(zs[p] - sps[p] + betweens[p] + carry[p][0])
            weights.append((jnp.where(vis, a, 0.0) if masked else a).astype(BF16))
        out = []
        for p in range(npair):
            r, acc = carry[p]
            acc = acc + _dot(vt_ref[kb, p * LANES:(p + 1) * LANES, :], weights[p])
            out.append((r + jnp.sum(lks[p], axis=0, keepdims=True), acc))
        return tuple(out)

    init = tuple((jnp.zeros((1, 2 * QB), F32), jnp.zeros((LANES, 2 * QB), F32)) for _ in range(npair))
    last = (i * QB) // KB
    carry = step(last, init, True)
    carry = lax.fori_loop(0, last, lambda n, c: step(last - 1 - n, c, False), carry)
    for p in range(npair):
        o_ref[:, p * LANES:(p + 1) * LANES] = _pair_out(carry[p][1])


def _sb_prompt_t(pb, nb, t):
    nq = t // QB
    return pl.pallas_call(
        _sb_t_kernel,
        grid=(nb, nq),
        in_specs=[pl.BlockSpec((QB, HW), lambda b, i: (b * nq + i, C_QC // HW)),
                  pl.BlockSpec((t, HW), lambda b, i: (b, C_KC // HW)),
                  pl.BlockSpec((t, HW), lambda b, i: (b, C_VC // HW))],
        out_specs=pl.BlockSpec((QB, HW), lambda b, i: (b * nq + i, 0)),
        out_shape=jax.ShapeDtypeStruct((nb * t, HW), BF16),
        scratch_shapes=[pltpu.VMEM((t // KB, HW, KB), BF16)],
        compiler_params=_params("arbitrary", "arbitrary"),
        name="sb_prompt",
    )(pb, pb, pb)


def _select_blocks_t(score, qrow, n_s, sc_ref):
    blk = _iota(score.shape, 0)
    blk_f = blk.astype(F32)
    valid = blk * SEL_BLOCK <= qrow
    forced = (blk == jnp.right_shift(qrow, 6)) | (blk == 0)
    sc = jnp.where(forced, FORCE_SCORE, jnp.where(valid, score, NEG_INF))
    sc = jnp.where(blk < n_s, sc, BELOW_ALL)
    rows = min(score.shape[0], -(-n_s // 8) * 8)
    sc, blk = sc[0:rows], _iota((rows, score.shape[1]), 0)
    sc_ref[0:rows, :] = sc
    rank = jnp.zeros(sc.shape, F32)
    for k in range(n_s):
        other = sc_ref[k:k + 1, :]
        rank = rank + jnp.where((other > sc) | ((other == sc) & (blk > k)), 1.0, 0.0)
    sel = jnp.where((rank < min(SEL_TOPK, n_s)) & (blk < n_s), 1.0, 0.0)
    if rows < score.shape[0]:
        sel = jnp.concatenate([sel, jnp.zeros((score.shape[0] - rows, score.shape[1]), F32)], axis=0)
    return sel


def _nsa_t_kernel(t5_ref, q_ref, ck_ref, cv_ref, sel_ref, win_ref, misc_ref, o_ref, svt_ref, wvt_ref, sc_ref, *, n_s):
    i = pl.program_id(1)
    q0 = i * QB
    nch = ck_ref.shape[2]
    nblk = svt_ref.shape[0]
    cols = N_HEADS * QB

    @pl.when(i == 0)
    def _():
        for kb in range(nblk):
            svt_ref[kb] = _transpose_bf16(sel_ref[kb * QB:(kb + 1) * QB, LANES:2 * LANES])
            wvt_ref[kb] = _transpose_bf16(win_ref[kb * QB:(kb + 1) * QB, LANES:2 * LANES])

    lane = _iota((QB, LANES), 1)
    qrow = q0 + _iota((1, QB), 1)

    def per_head(f):
        return jnp.concatenate([f(h) for h in range(N_HEADS)], axis=1)

    def per_group(x0, x1):
        return jnp.concatenate([x0] * NSA_HG + [x1] * NSA_HG, axis=1)

    def bias_of(rel):
        bk = _bucket(rel)
        return per_head(lambda h: _lut(jnp.broadcast_to(t5_ref[h:h + 1, :], (rel.shape[0], LANES)), bk))

    def q_head(head):
        g = head // NSA_HG
        x = q_ref[:, (head // 2) * LANES:(head // 2 + 1) * LANES].astype(F32) * SCALE
        if head % 2 != g:
            x = pltpu.roll(x, HEAD_DIM, axis=1)
        return jnp.where((lane >= HEAD_DIM) == (g == 1), x, 0.0).astype(BF16)

    q8 = jnp.concatenate([q_head(h) for h in range(N_HEADS)], axis=0)
    rel_d = _iota((QB, QB), 1) - _iota((QB, QB), 0)
    bias_d = bias_of(rel_d)
    bias_p = bias_of(rel_d + QB)
    bias_far = per_head(lambda h: jnp.broadcast_to(t5_ref[h:h + 1, N_BUCKETS - 1:N_BUCKETS], (1, QB)))

    nrow = _iota((nch, QB), 0)
    rel_c = qrow - (nrow * CMP_STRIDE + (CMP_BLOCK - 1))
    vis_c = jnp.where((rel_c >= 0) & (nrow < nch - 1), 1.0, 0.0)
    mask_c = per_group(vis_c, vis_c) > 0.5
    lc = jnp.where(mask_c, _dot_nt(ck_ref[0, 0], q8) + bias_of(rel_c), NEG_INF)
    ec = jnp.where(mask_c, jnp.exp(lc - jnp.max(lc, axis=0, keepdims=True)), 0.0)
    pc = ec / jnp.maximum(jnp.sum(ec, axis=0, keepdims=True), TINY)
    o_cmp = _dot(_transpose_bf16(cv_ref[0, 0]), pc.astype(BF16))
    psums = []
    for g in range(NSA_G):
        ps = pc[:, g * NSA_HG * QB:(g * NSA_HG + 1) * QB]
        for hg in range(1, NSA_HG):
            ps = ps + pc[:, (g * NSA_HG + hg) * QB:(g * NSA_HG + hg + 1) * QB]
        psums.append(ps)
    psum = jnp.concatenate(psums, axis=1)
    cs = _iota((LANES, nch), 1) * CMP_STRIDE
    ss = _iota((LANES, nch), 0) * SEL_BLOCK
    ov_t = (jnp.maximum(jnp.minimum(cs + CMP_BLOCK, ss + SEL_BLOCK) - jnp.maximum(cs, ss), 0).astype(F32)
            / CMP_BLOCK).astype(BF16)
    p_hi = psum.astype(BF16)
    p_lo = (psum - p_hi.astype(F32)).astype(BF16)
    selm = _select_blocks_t(_dot(ov_t, p_hi) + _dot(ov_t, p_lo), jnp.concatenate([qrow] * NSA_G, axis=1),
                            n_s, sc_ref).astype(BF16)

    def attend(steps):
        loaded = []
        for k_ref, vt_ref, k0, width, _, _, _, real in steps:
            k0 = pl.multiple_of(k0, QB)
            kk = k_ref[pl.ds(k0, width), 0:LANES]
            kb = jnp.right_shift(k0, 7)
            vt = vt_ref[kb] if width == QB else jnp.concatenate([vt_ref[kb], vt_ref[kb + 1]], axis=1)
            if real is not None:
                kk = jnp.where(real, kk, jnp.zeros_like(kk))
                vt = jnp.where(real, vt, jnp.zeros_like(vt))
            loaded.append((kk, vt))
        scores = [_dot_nt(kk, q8) for kk, _ in loaded]
        soft = []
        for (_, _, _, _, hide, bias, (m, l, _), _), s in zip(steps, scores):
            s = s + (bias + hide)
            m_new = jnp.maximum(m, jnp.max(s, axis=0, keepdims=True))
            pr = jnp.exp(s - m_new)
            alpha = jnp.exp(m - m_new)
            soft.append((m_new, alpha * l + jnp.sum(pr, axis=0, keepdims=True), alpha, pr.astype(BF16)))
        return [(m_new, l, alpha * step[6][2] + _dot(vt, pr))
                for step, (_, vt), (m_new, l, alpha, pr) in zip(steps, loaded, soft)]

    def init():
        return (jnp.full((1, cols), NEG_INF, F32), jnp.zeros((1, cols), F32), jnp.zeros((LANES, cols), F32))

    def win_step(dlt, carry):
        rel = qrow - ((i - dlt) * QB + _iota((QB, 1), 0))
        hide = jnp.where((rel >= 0) & (rel < WINDOW), 0.0, NEG_INF)
        bias = bias_d if dlt == 0 else (bias_p if dlt == 1 else bias_far)
        return (win_ref, wvt_ref, jnp.maximum(i - dlt, 0) * QB, QB, per_group(hide, hide), bias, carry,
                None if dlt == 0 else i - dlt >= 0)

    def sel_step(k0, width, bias, carry, causal=False, valid=None):
        blk = jnp.right_shift(k0 + _iota((width, LANES), 0), 6)
        expand = jnp.where(_iota((width, LANES), 1) == blk, 1.0, 0.0).astype(BF16)
        picked = _dot(expand, selm)
        if causal:
            picked = picked * jnp.concatenate([jnp.where((k0 + _iota((width, 1), 0)) <= qrow, 1.0, 0.0)] * NSA_G, axis=1)
        if valid is not None:
            picked = picked * jnp.where(valid, 1.0, 0.0)
        hide = jnp.where(picked > 0.5, 0.0, NEG_INF)
        return (sel_ref, svt_ref, k0, width, per_group(hide[:, 0:QB], hide[:, QB:2 * QB]), bias, carry, None)

    k_prev = jnp.maximum(q0 - QB, 0)
    k_odd = jnp.maximum(q0 - 2 * QB, 0)
    c_win, c_sel = attend([win_step(0, init()), sel_step(q0, QB, bias_d, init(), causal=True)])
    c_win, c_sel = attend([win_step(1, c_win), sel_step(k_prev, QB, bias_p, c_sel, valid=i >= 1)])
    c_win, c_sel = attend([win_step(2, c_win),
                           sel_step(k_odd, QB, bias_far, c_sel, valid=(i >= 2) & (jnp.bitwise_and(i, 1) == 0))])
    for dlt in range(3, WINDOW // QB + 1):
        (c_win,) = attend([win_step(dlt, c_win)])
    o_win = c_win[2] / jnp.maximum(c_win[1], TINY)
    c_sel = lax.fori_loop(0, jnp.right_shift(jnp.maximum(i - 1, 0), 1),
                          lambda kb, c: attend([sel_step(kb * KB, KB, bias_far, c)])[0], c_sel)
    o_sel = c_sel[2] / jnp.maximum(c_sel[1], TINY)

    sig_t = _sigmoid(misc_ref[:, 0:LANES]).T

    def gate(br):
        return per_head(lambda h: sig_t[8 + br * N_HEADS + h:9 + br * N_HEADS + h, :])

    o8 = gate(0) * o_cmp + gate(1) * o_sel + gate(2) * o_win
    for p in range(N_HEADS // 2):
        own = []
        for head in (2 * p, 2 * p + 1):
            g = head // NSA_HG
            own.append(o8[g * HEAD_DIM:(g + 1) * HEAD_DIM, head * QB:(head + 1) * QB])
        o_ref[:, p * LANES:(p + 1) * LANES] = jnp.concatenate(own, axis=0).T.astype(BF16)


def _nsa_prompt_t(pb, pf, comp, t5, nb, t):
    nq = t // QB
    nch = comp.shape[2]
    kernel = functools.partial(_nsa_t_kernel, n_s=-(-t // SEL_BLOCK))
    return pl.pallas_call(
        kernel,
        grid=(nb, nq),
        in_specs=[pl.BlockSpec((N_HEADS, LANES), lambda b, i: (0, 0)),
                  pl.BlockSpec((QB, HW), lambda b, i: (b * nq + i, C_QB // HW)),
                  pl.BlockSpec((1, 1, nch, LANES), lambda b, i: (0, b, 0, 0)),
                  pl.BlockSpec((1, 1, nch, LANES), lambda b, i: (1, b, 0, 0)),
                  pl.BlockSpec((t, 2 * LANES), lambda b, i: (b, C_SEL // (2 * LANES))),
                  pl.BlockSpec((t, 2 * LANES), lambda b, i: (b, C_WIN // (2 * LANES))),
                  pl.BlockSpec((QB, MISC_W), lambda b, i: (b * nq + i, C_MISC // MISC_W))],
        out_specs=pl.BlockSpec((QB, HW), lambda b, i: (b * nq + i, 0)),
        out_shape=jax.ShapeDtypeStruct((nb * t, HW), BF16),
        scratch_shapes=[pltpu.VMEM((nq, LANES, QB), BF16), pltpu.VMEM((nq, LANES, QB), BF16),
                        pltpu.VMEM((LANES, NSA_G * QB), F32)],
        compiler_params=_params("arbitrary", "arbitrary"),
        name="nsa_prompt",
    )(t5, pb, comp, comp, pb, pb, pf)


def _page_specs(npg, block, layer, tail):
    def spec(j):
        return pl.BlockSpec(block, lambda b, pt: (pt[b * npg + j], layer) + tail)
    return [spec(j) for j in range(npg)]


def _cum_sample_body(pages, m_ref, bf_ref, logf_ref):
    npg = len(pages)
    x = jnp.concatenate([pg[0, 0] for pg in pages], axis=0)
    lane = _iota(x.shape, 1)
    s = 1
    while s < PAGE:
        x = x + jnp.where(lane >= s, pltpu.roll(x, s, axis=1), 0.0)
        s *= 2
    off = jnp.zeros((N_HEADS, 1), F32)
    cum_pages = []
    for j in range(npg):
        blk = x[j * N_HEADS:(j + 1) * N_HEADS]
        cum_pages.append(blk + off)
        off = off + blk[:, PAGE - 1:PAGE]
    eye = _iota((N_HEADS, LANES), 0) == _iota((N_HEADS, LANES), 1)
    tot = jnp.sum(jnp.where(eye, off, 0.0), axis=0, keepdims=True)
    logf = _log_sigmoid(m_ref[:, 0:LANES] + bf_ref[...])
    logf_ref[...] = logf
    row = _iota(logf.shape, 0)
    c = logf
    s = 1
    while s < DEC_T:
        c = c + jnp.where(row >= s, pltpu.roll(c, s, axis=0), 0.0)
        s *= 2
    return cum_pages, c + tot


def _block_diag_q(q):
    qt = jnp.concatenate([q] * N_HEADS, axis=0)
    same = jnp.right_shift(_iota(qt.shape, 1), 6) == jnp.right_shift(_iota(qt.shape, 0), 3)
    return jnp.where(same, qt * SCALE, 0.0).astype(BF16)


def _rows_per_head(x):
    return jnp.concatenate([jnp.broadcast_to(x[h:h + 1], (DEC_T, x.shape[1])) for h in range(N_HEADS)], axis=0)


def _col_per_head(x):
    return jnp.concatenate([x[:, h:h + 1] for h in range(N_HEADS)], axis=0)


def _own_head_lanes(acc):
    lane_h = jnp.right_shift(_iota((DEC_T, HW), 1), 6)
    out = jnp.zeros((DEC_T, HW), F32)
    for h in range(N_HEADS):
        out = jnp.where(lane_h == h, acc[h * DEC_T:(h + 1) * DEC_T], out)
    return out


def _pad_rows(x, n):
    return jnp.concatenate([x, jnp.zeros((n - x.shape[0], x.shape[1]), x.dtype)], axis=0)


def _new_key_mask(strict):
    shape = (N_HEADS * DEC_T, PAGE)
    t_row = jnp.bitwise_and(_iota(shape, 0), DEC_T - 1)
    return (_iota(shape, 1) < t_row) if strict else (_iota(shape, 1) <= t_row)


def _fox_sample_body(pages, q_ref, k_ref, v_ref, cum_pages, cum_new, o_ref):
    npg = len(pages)
    qbd = _block_diag_q(q_ref[...])
    cq = _col_per_head(cum_new)
    cum_new_t = _pad_rows(cum_new, LANES).T[0:N_HEADS]
    raw = [_dot(qbd, pages[j][0, 0, 0].astype(BF16)) for j in range(npg)]
    s_pages = [raw[j] + cq - _rows_per_head(cum_pages[j]) for j in range(npg)]
    k_new = _pad_rows(k_ref[...], PAGE).astype(BF16)
    s_new = _dot_nt(qbd, k_new) + cq - _rows_per_head(cum_new_t)
    mask_new = _new_key_mask(strict=False)
    s_new = jnp.where(mask_new, s_new, NEG_INF)
    m = jnp.max(s_new, axis=-1, keepdims=True)
    for s in s_pages:
        m = jnp.maximum(m, jnp.max(s, axis=-1, keepdims=True))
    p_new = jnp.where(mask_new, jnp.exp(s_new - m), 0.0)
    l = jnp.sum(p_new, axis=-1, keepdims=True)
    acc = _dot(p_new.astype(BF16), _pad_rows(v_ref[...], PAGE).astype(BF16))
    for j in range(npg):
        pr = jnp.exp(s_pages[j] - m)
        l = l + jnp.sum(pr, axis=-1, keepdims=True)
        acc = acc + _dot_nt(pr.astype(BF16), pages[j][0, 0, 1].astype(BF16))
    o_ref[...] = _own_head_lanes(acc / jnp.maximum(l, TINY))


def _sb_sample_body(pages, q_ref, k_ref, v_ref, o_ref):
    npg = len(pages)
    qbd = _block_diag_q(q_ref[...])
    tri = _tri(PAGE)
    z = _dot_nt(qbd, _pad_rows(k_ref[...], PAGE).astype(BF16))
    mask = _new_key_mask(strict=True)
    sp = _softplus(z)
    lk = jnp.where(mask, -sp, 0.0)
    between = _split_dot(lk, tri)
    a = jnp.where(mask, jnp.exp(z - sp + between), 0.0)
    acc = _dot(a.astype(BF16), _pad_rows(v_ref[...], PAGE).astype(BF16))
    r = jnp.sum(lk, axis=-1, keepdims=True)
    rows = N_HEADS * DEC_T
    zs = [_dot(qbd, pages[j][0, 0, 0].astype(BF16)) for j in range(npg)]
    sps = [_softplus(z) for z in zs]
    within = _split_dot(jnp.concatenate([-sp for sp in sps], axis=0), tri)
    for j in reversed(range(npg)):
        a = jnp.exp(zs[j] - sps[j] + within[j * rows:(j + 1) * rows] + r)
        acc = acc + _dot_nt(a.astype(BF16), pages[j][0, 0, 1].astype(BF16))
        r = r - jnp.sum(sps[j], axis=-1, keepdims=True)
    o_ref[...] = _own_head_lanes(acc)


def _compress_sample_body(pages, w1_ref, pe_ref, w2_ref, o_ref, ch_ref, xp_ref):
    npg = len(pages)
    per_page = PAGE // CMP_STRIDE
    for j in range(npg):
        for c in range(2):
            xp_ref[2 * j + c] = pages[j][0, 0, 0, c].T
    for j in range(npg):
        for c in range(2):
            for r in range(CMP_STRIDE):
                ch_ref[c, j * per_page:(j + 1) * per_page, r * LANES:(r + 1) * LANES] = (
                    xp_ref[2 * j + c, pl.ds(r, per_page, stride=CMP_STRIDE), :])
    _compress_chunks(ch_ref, w1_ref, pe_ref, w2_ref, o_ref)


def _sample_fox_kernel(pt_ref, *refs, npg):
    logf_pages, fox_pages, cmp_pages = refs[0:npg], refs[npg:2 * npg], refs[2 * npg:3 * npg]
    (m_ref, bf_ref, q_ref, k_ref, v_ref, w1_ref, pe_ref, w2_ref,
     logf_ref, o_ref, comp_ref, ch_ref, xp_ref) = refs[3 * npg:]
    cum_pages, cum_new = _cum_sample_body(logf_pages, m_ref, bf_ref, logf_ref)
    _fox_sample_body(fox_pages, q_ref, k_ref, v_ref, cum_pages, cum_new, o_ref)
    _compress_sample_body(cmp_pages, w1_ref, pe_ref, w2_ref, comp_ref, ch_ref, xp_ref)


def _sample_fox(pt, logf_view, fox_view, nsa_view, pf, bf, w1big, pe_rows, w2p, layer, nbs, npg, row0):
    kernel = functools.partial(_sample_fox_kernel, npg=npg)
    nch = npg * PAGE // CMP_STRIDE
    rb = row0 // DEC_T
    grid_spec = pltpu.PrefetchScalarGridSpec(
        num_scalar_prefetch=1,
        grid=(nbs,),
        in_specs=_page_specs(npg, (1, 1, N_HEADS, PAGE), layer, (0, 0))
        + _page_specs(npg, (1, 1, 2, HW, PAGE), layer, (0, 0, 0))
        + _page_specs(npg, (1, 1, 1, 2, LANES, PAGE), layer, (0, 0, 0, 0))
        + [pl.BlockSpec((DEC_T, MISC_W), lambda b, pt: (rb + b, C_MISC // MISC_W)),
           pl.BlockSpec((1, LANES), lambda b, pt: (0, 0)),
           pl.BlockSpec((DEC_T, HW), lambda b, pt: (rb + b, C_QA // HW)),
           pl.BlockSpec((DEC_T, HW), lambda b, pt: (rb + b, C_KA // HW)),
           pl.BlockSpec((DEC_T, HW), lambda b, pt: (rb + b, C_VA // HW))]
        + _compress_weight_specs(layer, lambda idx: (lambda b, pt: idx)),
        out_specs=[pl.BlockSpec((DEC_T, LANES), lambda b, pt: (b, 0)),
                   pl.BlockSpec((DEC_T, HW), lambda b, pt: (b, 0)),
                   pl.BlockSpec((2, 1, nch, LANES), lambda b, pt: (0, b, 0, 0))],
        scratch_shapes=[pltpu.VMEM((2, nch, CHUNK_W), F32), pltpu.VMEM((2 * npg, PAGE, LANES), F32)],
    )
    return pl.pallas_call(
        kernel,
        grid_spec=grid_spec,
        out_shape=[jax.ShapeDtypeStruct((nbs * DEC_T, LANES), F32),
                   jax.ShapeDtypeStruct((nbs * DEC_T, HW), F32),
                   jax.ShapeDtypeStruct((2, nbs, nch, LANES), BF16)],
        compiler_params=_params("arbitrary"),
        name="sample_fox_compress",
    )(pt, *([logf_view] * npg), *([fox_view] * npg), *([nsa_view] * npg), pf, bf, pf, pf, pf, w1big, pe_rows, w2p)


def _nsa_sample_body(pages, q_ref, seln_ref, winn_ref, misc_ref, ck_ref, cv_ref, wst_ref, t5_ref, exp_ref, o_ref, n_s):
    npg = len(pages)
    past = npg * PAGE
    rows = N_HEADS * DEC_T
    nch = ck_ref.shape[2]
    wb = wst_ref.shape[4]
    lane8 = _iota((DEC_T, LANES), 1)
    t5c = t5_ref[...]

    def lut(rel):
        return _lut(t5c, _bucket(rel))

    def t_of(shape):
        return jnp.bitwise_and(_iota(shape, 0), DEC_T - 1)

    def g_rows(x):
        return jnp.concatenate([x] * NSA_HG, axis=0)

    q = q_ref[...]
    qrows = []
    for head in range(N_HEADS):
        g = head // NSA_HG
        x = q[:, (head // 2) * LANES:(head // 2 + 1) * LANES] * SCALE
        if head % 2 != g:
            x = pltpu.roll(x, HEAD_DIM, axis=1)
        qrows.append(jnp.where((lane8 >= HEAD_DIM) == (g == 1), x, 0.0))
    qbd = jnp.concatenate(qrows, axis=0).astype(BF16)
    bias_far = t5c[:, N_BUCKETS - 1:N_BUCKETS]

    raw_c = _dot_nt(qbd, ck_ref[0, 0])
    raw_pages = [_dot(qbd, pages[j][0, 0, 0, 0].astype(BF16)) for j in range(npg)]
    seln = _pad_rows(seln_ref[...], PAGE).astype(BF16)
    winn = _pad_rows(winn_ref[...], PAGE).astype(BF16)
    raw_new = _dot_nt(qbd, seln[:, 0:LANES])
    raw_w = _dot(qbd, wst_ref[0, 0, 0].astype(BF16))
    raw_wn = _dot_nt(qbd, winn[:, 0:LANES])

    qpos_c = past + t_of((rows, nch))
    rel_c = qpos_c - (_iota((rows, nch), 1) * CMP_STRIDE + (CMP_BLOCK - 1))
    mask_c = (rel_c >= 0) & (_iota((rows, nch), 1) < nch - 1)
    pc = _masked_softmax(raw_c + lut(rel_c), mask_c)
    psums = []
    for g in range(NSA_G):
        base = g * NSA_HG * DEC_T
        psum = pc[base:base + DEC_T]
        for hg in range(1, NSA_HG):
            psum = psum + pc[base + hg * DEC_T:base + (hg + 1) * DEC_T]
        psums.append(psum)

    rel_new = t_of((rows, PAGE)) - _iota((rows, PAGE), 1)
    in_new = _iota((rows, PAGE), 1) < DEC_T
    bias_new = lut(rel_new)
    rel_w = (wb + t_of((rows, wb))) - _iota((rows, wb), 1)
    mask_w = (rel_w >= 0) & (rel_w < WINDOW)
    s_w = jnp.where(mask_w, raw_w + lut(rel_w), NEG_INF)
    mask_wn = (rel_new >= 0) & (rel_new < WINDOW) & in_new
    s_wn = jnp.where(mask_wn, raw_wn + bias_new, NEG_INF)
    m = jnp.maximum(jnp.max(s_w, axis=-1, keepdims=True), jnp.max(s_wn, axis=-1, keepdims=True))
    p_w = jnp.where(mask_w, jnp.exp(s_w - m), 0.0)
    p_wn = jnp.where(mask_wn, jnp.exp(s_wn - m), 0.0)
    l_w = jnp.sum(p_w, axis=-1, keepdims=True) + jnp.sum(p_wn, axis=-1, keepdims=True)

    score = _split_dot(jnp.concatenate(psums, axis=0), _overlap(nch))
    o_cmp = _dot(pc.astype(BF16), cv_ref[0, 0])
    acc_w = _dot_nt(p_w.astype(BF16), wst_ref[0, 0, 1].astype(BF16)) + _dot(p_wn.astype(BF16), winn[:, LANES:2 * LANES])
    o_win = acc_w / jnp.maximum(l_w, TINY)

    qpos_g = past + jnp.bitwise_and(_iota((NSA_G * DEC_T, 1), 0), DEC_T - 1)
    picked = _select_blocks(score, qpos_g, n_s)
    picked = jnp.concatenate([g_rows(picked[g * DEC_T:(g + 1) * DEC_T]) for g in range(NSA_G)], axis=0)
    sel_past = _dot(picked.astype(BF16), exp_ref[...])
    last_blk = past // SEL_BLOCK
    sel_new = jnp.sum(jnp.where(_iota((rows, LANES), 1) == last_blk, picked, 0.0), axis=-1, keepdims=True) > 0.5
    s_pages, m_pages = [], []
    for j in range(npg):
        if past - (j + 1) * PAGE + 1 >= MAX_DISTANCE:
            s = raw_pages[j] + bias_far
        else:
            s = raw_pages[j] + lut(past + t_of((rows, PAGE)) - (j * PAGE + _iota((rows, PAGE), 1)))
        mk = sel_past[:, j * PAGE:(j + 1) * PAGE] > 0.5
        s_pages.append(jnp.where(mk, s, NEG_INF))
        m_pages.append(mk)
    mask_n = sel_new & (rel_new >= 0) & in_new
    s_new = jnp.where(mask_n, raw_new + bias_new, NEG_INF)
    m = jnp.max(s_new, axis=-1, keepdims=True)
    for s in s_pages:
        m = jnp.maximum(m, jnp.max(s, axis=-1, keepdims=True))
    p_new = jnp.where(mask_n, jnp.exp(s_new - m), 0.0)
    l = jnp.sum(p_new, axis=-1, keepdims=True)
    probs = []
    for j in range(npg):
        pr = jnp.where(m_pages[j], jnp.exp(s_pages[j] - m), 0.0)
        l = l + jnp.sum(pr, axis=-1, keepdims=True)
        probs.append(pr.astype(BF16))
    acc = _dot(p_new.astype(BF16), seln[:, LANES:2 * LANES])
    for j in range(npg):
        acc = acc + _dot_nt(probs[j], pages[j][0, 0, 0, 1].astype(BF16))
    o_sel = acc / jnp.maximum(l, TINY)

    sig = _sigmoid(misc_ref[:, 0:LANES])

    def gate(br):
        return jnp.concatenate(
            [sig[:, 8 + br * N_HEADS + h:9 + br * N_HEADS + h] for h in range(N_HEADS)], axis=0)

    o_all = gate(0) * o_cmp + gate(1) * o_sel + gate(2) * o_win
    pieces = []
    for head in range(N_HEADS):
        x = o_all[head * DEC_T:(head + 1) * DEC_T]
        if head % 2 != head // NSA_HG:
            x = pltpu.roll(x, HEAD_DIM, axis=1)
        pieces.append(x)
    for p in range(N_HEADS // 2):
        o_ref[:, p * LANES:(p + 1) * LANES] = jnp.where(lane8 < HEAD_DIM, pieces[2 * p], pieces[2 * p + 1])


def _sample_sb_nsa_kernel(pt_ref, *refs, npg, n_s):
    sb_pages, sel_pages = refs[0:npg], refs[npg:2 * npg]
    (qc_ref, kc_ref, vc_ref, q_ref, seln_ref, winn_ref, misc_ref, ck_ref, cv_ref, wst_ref, t5_ref, exp_ref,
     o_sb_ref, o_nsa_ref) = refs[2 * npg:]
    _sb_sample_body(sb_pages, qc_ref, kc_ref, vc_ref, o_sb_ref)
    _nsa_sample_body(sel_pages, q_ref, seln_ref, winn_ref, misc_ref, ck_ref, cv_ref, wst_ref, t5_ref, exp_ref,
                     o_nsa_ref, n_s)


def _sample_sb_nsa(pt, sb_view, nsa_view, pf, comp, win_view, t5col, expand, layer, nbs, npg, row0):
    nch = comp.shape[2]
    wb = win_view.shape[4]
    past = npg * PAGE
    kernel = functools.partial(_sample_sb_nsa_kernel, npg=npg, n_s=-(-(past + DEC_T) // SEL_BLOCK))
    rb = row0 // DEC_T
    grid_spec = pltpu.PrefetchScalarGridSpec(
        num_scalar_prefetch=1,
        grid=(nbs,),
        in_specs=_page_specs(npg, (1, 1, 2, HW, PAGE), layer, (0, 0, 0))
        + _page_specs(npg, (1, 1, 1, 2, LANES, PAGE), layer, (1, 0, 0, 0)) + [
            pl.BlockSpec((DEC_T, HW), lambda b, pt: (rb + b, C_QC // HW)),
            pl.BlockSpec((DEC_T, HW), lambda b, pt: (rb + b, C_KC // HW)),
            pl.BlockSpec((DEC_T, HW), lambda b, pt: (rb + b, C_VC // HW)),
            pl.BlockSpec((DEC_T, HW), lambda b, pt: (rb + b, C_QB // HW)),
            pl.BlockSpec((DEC_T, 2 * LANES), lambda b, pt: (rb + b, C_SEL // (2 * LANES))),
            pl.BlockSpec((DEC_T, 2 * LANES), lambda b, pt: (rb + b, C_WIN // (2 * LANES))),
            pl.BlockSpec((DEC_T, MISC_W), lambda b, pt: (rb + b, C_MISC // MISC_W)),
            pl.BlockSpec((1, 1, nch, LANES), lambda b, pt: (0, b, 0, 0)),
            pl.BlockSpec((1, 1, nch, LANES), lambda b, pt: (1, b, 0, 0)),
            pl.BlockSpec((1, 1, 2, LANES, wb), lambda b, pt: (b, layer, 0, 0, 0)),
            pl.BlockSpec((N_HEADS * DEC_T, LANES), lambda b, pt: (0, 0)),
            pl.BlockSpec((LANES, past), lambda b, pt: (0, 0))],
        out_specs=[pl.BlockSpec((DEC_T, HW), lambda b, pt: (b, 0)), pl.BlockSpec((DEC_T, HW), lambda b, pt: (b, 0))],
    )
    return pl.pallas_call(
        kernel,
        grid_spec=grid_spec,
        out_shape=[jax.ShapeDtypeStruct((nbs * DEC_T, HW), F32), jax.ShapeDtypeStruct((nbs * DEC_T, HW), F32)],
        compiler_params=_params("arbitrary"),
        name="sample_sb_nsa",
    )(pt, *([sb_view] * npg), *([nsa_view] * npg), pf, pf, pf, pf, pf, pf, pf, comp, comp, win_view, t5col, expand)


def _win_state_kernel(*refs):
    win_ref, new_refs, o_ref = refs[0], refs[1:-1], refs[-1]
    wb = win_ref.shape[4]
    lane = _iota((LANES, LANES), 1)
    for l, new_ref in enumerate(new_refs):
        for kv in range(2):
            shifted = pltpu.roll(win_ref[0, l, kv], wb - DEC_T, axis=1)
            new_t = _pad_rows(new_ref[:, kv * LANES:(kv + 1) * LANES], LANES).T
            tail = jnp.where(lane >= LANES - DEC_T, pltpu.roll(new_t, LANES - DEC_T, axis=1), shifted[:, wb - LANES:wb])
            o_ref[0, l, kv] = jnp.concatenate([shifted[:, 0:wb - LANES], tail], axis=1)


def _win_state(win_view, pfs, nbs, row0):
    _, depth, _, _, wb = win_view.shape
    rb = row0 // DEC_T
    blk = pl.BlockSpec((1, depth, 2, LANES, wb), lambda b: (b, 0, 0, 0, 0))
    return pl.pallas_call(
        _win_state_kernel,
        grid=(nbs,),
        in_specs=[blk] + [pl.BlockSpec((DEC_T, 2 * LANES), lambda b: (rb + b, C_WIN // (2 * LANES))) for _ in pfs],
        out_specs=blk,
        out_shape=jax.ShapeDtypeStruct(win_view.shape, F32),
        compiler_params=_params("arbitrary"),
        name="win_state",
    )(win_view, *pfs)


def _reorder_w_in(w_in, d):
    sizes = (HW, 2 * HW, N_HEADS, HW, 6 * NSA_G * HEAD_DIM, 3 * N_HEADS, HW, 2 * HW, 3 * d)
    offs = np.concatenate([[0], np.cumsum(sizes)])
    seg = [w_in[:, :, offs[i]:offs[i + 1]] for i in range(len(sizes))]
    q_a, kv_a, f_a, q_b, kv_b, g_b, q_c, kv_c, g_m = seg
    pad = jnp.zeros(w_in.shape[:2] + (MISC_W - N_HEADS - 3 * N_HEADS,), w_in.dtype)
    return jnp.concatenate([q_a, kv_a, q_b, q_c, kv_c, kv_b, f_a, g_b, pad, g_m], axis=-1).astype(BF16)


def kernel(x_prompt, x_sample, cache_fox_kv, cache_fox_logf, cache_nsa_kv, cache_sb_kv, state_nsa_win_kv, page_table, norm_mix_g, norm_ffn_g, norm_final_g, w_in, b_forget, t5_table, cmp_pe, cmp_w1, cmp_w2, w_out_a, w_out_b, w_out_c, w_out, router_group_w, router_group_b, router_expert_w, router_expert_b, expert_w_gate, expert_w_up, expert_w_down):
    nb, t, d = x_prompt.shape
    nbs, dec_t, _ = x_sample.shape
    depth = w_in.shape[0]
    npool = cache_fox_kv.shape[0]
    npg = page_table.shape[1]
    past = npg * PAGE
    wb = state_nsa_win_kv.shape[2]
    assert dec_t == DEC_T and t % KB == 0 and d % 128 == 0 and wb == WINDOW and past >= WINDOW
    n_p, n_s_rows = nb * t, nbs * DEC_T

    fox_view = jnp.transpose(cache_fox_kv, (0, 1, 3, 4, 5, 2)).reshape(npool, depth, 2, HW, PAGE)
    sb_view = jnp.transpose(cache_sb_kv, (0, 1, 3, 4, 5, 2)).reshape(npool, depth, 2, HW, PAGE)
    nsa_view = jnp.transpose(cache_nsa_kv, (0, 1, 3, 4, 5, 6, 2)).reshape(npool, depth, 2, 2, LANES, PAGE)
    logf_view = jnp.transpose(cache_fox_logf, (0, 1, 3, 2))
    win_view = jnp.transpose(state_nsa_win_kv, (0, 1, 3, 4, 5, 2)).reshape(nbs, depth, 2, LANES, wb)
    pt = page_table.reshape(-1).astype(jnp.int32)

    w_in_r = _reorder_w_in(w_in, d)
    bf_pad = jnp.pad(b_forget.astype(F32), ((0, 0), (0, LANES - N_HEADS))).reshape(depth, 1, LANES)
    t5 = jnp.pad(t5_table.astype(F32).T, ((0, 0), (0, LANES - N_BUCKETS)))
    t5col = jnp.repeat(t5, DEC_T, axis=0)
    expand = jnp.asarray(np.arange(LANES)[:, None] == (np.arange(past)[None, :] // SEL_BLOCK), BF16)
    w1r = cmp_w1.reshape(depth, 2, 2, CMP_STRIDE, HEAD_DIM, CMP_HIDDEN)
    w1big = jnp.einsum("zchldk,gG->zclgdhGk", w1r, jnp.eye(NSA_G, dtype=w1r.dtype)).reshape(
        depth, 2, CHUNK_W, 2 * NSA_G * CMP_HIDDEN).astype(BF16)
    pe_rows = jnp.broadcast_to(cmp_pe.reshape(depth, 2, 2, CMP_STRIDE, 1, HEAD_DIM),
                               (depth, 2, 2, CMP_STRIDE, NSA_G, HEAD_DIM)).reshape(depth, 2, 2, CHUNK_W)
    pe_rows = jnp.pad(pe_rows, ((0, 0), (0, 0), (0, 6), (0, 0))).astype(F32)
    w2p = jnp.stack([jnp.pad(cmp_w2, ((0, 0), (0, 0), (0, 0), (g * HEAD_DIM, LANES - (g + 1) * HEAD_DIM)))
                     for g in range(NSA_G)], axis=2).astype(BF16)
    w_router = jnp.pad(jnp.concatenate([router_group_w, router_expert_w], axis=-1),
                       ((0, 0), (0, 0), (0, LANES - N_GROUPS - N_EXPERTS))).astype(F32)
    b_router = jnp.pad(jnp.concatenate([router_group_b, router_expert_b], axis=-1),
                       ((0, 0), (0, LANES - N_GROUPS - N_EXPERTS))).astype(F32).reshape(depth, 1, LANES)
    wa, wb_, wc, wo = (w.astype(BF16) for w in (w_out_a, w_out_b, w_out_c, w_out))
    wg, wu, wd = (w.astype(BF16) for w in (expert_w_gate, expert_w_up, expert_w_down))

    h = jnp.concatenate([x_prompt.reshape(n_p, d), x_sample.reshape(n_s_rows, d)], axis=0)
    st_p = [[] for _ in range(5)]
    st_s = [[] for _ in range(5)]
    pfs = []
    for l in range(depth):
        pf, pb = _proj(h, norm_mix_g[l], w_in_r, l)

        logf_p, cum_p = _cum_prompt(pf, bf_pad[l], nb, t)
        cum_t = jnp.transpose(cum_p[:, :N_HEADS].reshape(nb, t, N_HEADS), (0, 2, 1)).reshape(nb, N_HEADS, t // QB, QB)
        o_a_p = _fox_prompt_t(pb, cum_p, cum_t, nb, t)
        o_c_p = _sb_prompt_t(pb, nb, t)
        comp_p = _compress_prompt(pf, w1big, pe_rows, w2p, l, nb, t)
        o_b_p = _nsa_prompt_t(pb, pf, comp_p, t5, nb, t)

        logf_s, o_a_s, comp_s = _sample_fox(pt, logf_view, fox_view, nsa_view, pf, bf_pad[l], w1big, pe_rows, w2p,
                                            l, nbs, npg, n_p)
        o_c_s, o_b_s = _sample_sb_nsa(pt, sb_view, nsa_view, pf, comp_s, win_view, t5col, expand, l, nbs, npg, n_p)

        o_a = jnp.concatenate([o_a_p, o_a_s.astype(BF16)], axis=0)
        o_b = jnp.concatenate([o_b_p, o_b_s.astype(BF16)], axis=0)
        o_c = jnp.concatenate([o_c_p, o_c_s.astype(BF16)], axis=0)
        mixed = _merge(o_a, o_b, o_c, pf, wa, wb_, wc, d, l)
        h = _mm_res(mixed, wo, h, l)
        xn, comb, mcol, mrow, cnt = _router(h, norm_ffn_g[l], w_router[l], b_router[l])
        counts = cnt[:, 0, :N_GROUPS].astype(jnp.int32).reshape(-1)
        h = _moe(xn, comb, mcol, mrow, counts, h, wg, wu, wd, l)

        def rows(c0, width, shape, lo, hi):
            return pf[lo:hi, c0:c0 + width].reshape(shape)

        win_new_p = rows(C_WIN, 2 * LANES, (nb, t, 2, NSA_G, HEAD_DIM), 0, n_p)
        pfs.append(pf)
        st_p[0].append(rows(C_KA, 2 * HW, (nb, t, 2, N_HEADS, HEAD_DIM), 0, n_p))
        st_p[1].append(logf_p[:, :N_HEADS].reshape(nb, t, N_HEADS))
        st_p[2].append(rows(C_CMP, 4 * LANES, (nb, t, 2, 2, NSA_G, HEAD_DIM), 0, n_p))
        st_p[3].append(rows(C_KC, 2 * HW, (nb, t, 2, N_HEADS, HEAD_DIM), 0, n_p))
        st_p[4].append(win_new_p[:, t - min(WINDOW, t):])
        st_s[0].append(rows(C_KA, 2 * HW, (nbs, DEC_T, 2, N_HEADS, HEAD_DIM), n_p, n_p + n_s_rows))
        st_s[1].append(logf_s[:, :N_HEADS].reshape(nbs, DEC_T, N_HEADS))
        st_s[2].append(rows(C_CMP, 4 * LANES, (nbs, DEC_T, 2, 2, NSA_G, HEAD_DIM), n_p, n_p + n_s_rows))
        st_s[3].append(rows(C_KC, 2 * HW, (nbs, DEC_T, 2, N_HEADS, HEAD_DIM), n_p, n_p + n_s_rows))

    y = _final_norm(h, norm_final_g)
    y_prompt = y[:n_p].reshape(nb, t, d)
    y_sample = y[n_p:].reshape(nbs, DEC_T, d)
    sp = [jnp.stack(s, axis=1) for s in st_p]
    ss = [jnp.stack(s, axis=1) for s in st_s[:4]]
    win_s = _win_state(win_view, pfs, nbs, n_p).reshape(nbs, depth, 2, NSA_G, HEAD_DIM, wb)
    win_s = jnp.transpose(win_s, (0, 1, 5, 2, 3, 4))
    return (y_prompt, y_sample, sp[0], ss[0], sp[1], ss[1], sp[2], ss[2], sp[3], ss[3], sp[4], win_s)
```

```python
import functools
import math

import numpy as np
import jax
import jax.numpy as jnp
from jax import lax
from jax.experimental import pallas as pl
from jax.experimental.pallas import tpu as pltpu

F32 = jnp.float32
BF16 = jnp.bfloat16

HEAD_DIM = 64
N_HEADS = 8
NSA_G = 2
NSA_HG = N_HEADS // NSA_G
HW = N_HEADS * HEAD_DIM
PAGE = 128
DEC_T = 8
CMP_BLOCK = 32
CMP_STRIDE = 16
CMP_HIDDEN = 128
SEL_BLOCK = 64
SEL_TOPK = 8
WINDOW = 512
N_BUCKETS = 32
MAX_DISTANCE = 128
N_GROUPS = 4
EXPERTS_PER_GROUP = 4
N_EXPERTS = N_GROUPS * EXPERTS_PER_GROUP
RMS_EPS = 1e-6
NEG_INF = -1e30
FORCE_SCORE = 1e9
BELOW_ALL = -3e38
SCALE = HEAD_DIM ** -0.5
TINY = float(np.finfo(np.float32).tiny)
QB = 128
KB = 256
MOE_TILE = 768
MOE_CHUNK = 256
LANES = 128
VMEM_LIMIT = 56 * 1024 * 1024

C_QA, C_KA, C_VA, C_QB, C_QC, C_KC, C_VC = 0, 512, 1024, 1536, 2048, 2560, 3072
C_CMP, C_SEL, C_WIN, C_MISC, C_GM = 3584, 3840, 4096, 4352, 4608
MISC_W = 256
CHUNK_W = CMP_STRIDE * LANES


def _params(*sem):
    return pltpu.CompilerParams(dimension_semantics=sem, vmem_limit_bytes=VMEM_LIMIT)


def _pick(n, cap, mult):
    t = (min(cap, n) // mult) * mult
    while t > 0 and n % t:
        t -= mult
    assert t > 0, (n, cap, mult)
    return t


def _iota(shape, axis):
    return lax.broadcasted_iota(jnp.int32, shape, axis)


def _dot(a, b):
    return jnp.dot(a, b, preferred_element_type=F32)


def _dot_nt(a, b):
    return lax.dot_general(a, b, (((1,), (1,)), ((), ())), preferred_element_type=F32)


def _softplus(x):
    return jnp.maximum(x, 0.0) + jnp.log(1.0 + jnp.exp(-jnp.abs(x)))


def _log_sigmoid(x):
    return jnp.minimum(x, 0.0) - jnp.log1p(jnp.exp(-jnp.abs(x)))


def _sigmoid(x):
    return 1.0 / (1.0 + jnp.exp(-x))


def _split_dot(x, w):
    hi = x.astype(BF16)
    lo = (x - hi.astype(F32)).astype(BF16)
    return _dot(hi, w) + _dot(lo, w)


def _tri(w):
    return jnp.where(_iota((w, w), 0) > _iota((w, w), 1), 1.0, 0.0).astype(BF16)


def _bucket(rel):
    n = jnp.maximum(rel, 0)
    exact = N_BUCKETS // 2
    far = jnp.log(jnp.maximum(n, exact).astype(F32) / exact) / math.log(MAX_DISTANCE / exact)
    far = exact + (far * (N_BUCKETS - exact)).astype(jnp.int32)
    return jnp.where(n < exact, n, jnp.minimum(far, N_BUCKETS - 1))


def _masked_softmax(logits, mask):
    l = jnp.where(mask, logits, NEG_INF)
    e = jnp.where(mask, jnp.exp(l - jnp.max(l, axis=-1, keepdims=True)), 0.0)
    return e / jnp.maximum(jnp.sum(e, axis=-1, keepdims=True), TINY)


def _overlap(nch):
    cs = _iota((nch, LANES), 0) * CMP_STRIDE
    ss = _iota((nch, LANES), 1) * SEL_BLOCK
    ov = jnp.maximum(jnp.minimum(cs + CMP_BLOCK, ss + SEL_BLOCK) - jnp.maximum(cs, ss), 0)
    return (ov.astype(F32) / CMP_BLOCK).astype(BF16)


def _select_blocks(score, qpos, n_s):
    lane = _iota(score.shape, 1)
    lane_f = lane.astype(F32)
    valid = lane * SEL_BLOCK <= qpos
    forced = (lane == jnp.right_shift(qpos, 6)) | (lane == 0)
    sc = jnp.where(forced, FORCE_SCORE, jnp.where(valid, score, NEG_INF))
    sc = jnp.where(lane < n_s, sc, BELOW_ALL)
    rank = jnp.zeros(score.shape, F32)
    for k in range(n_s):
        other = sc[:, k:k + 1]
        rank = rank + jnp.where((other > sc) | ((other == sc) & (lane > k)), 1.0, 0.0)
    return jnp.where((rank < min(SEL_TOPK, n_s)) & (lane < n_s), 1.0, 0.0)


def _proj_kernel(x_ref, g_ref, w_ref, of_ref, ob_ref, xn_ref):
    @pl.when(pl.program_id(1) == 0)
    def _():
        x = x_ref[...]
        ms = jnp.mean(x * x, axis=-1, keepdims=True)
        xn_ref[...] = (x * lax.rsqrt(ms + RMS_EPS) * g_ref[...]).astype(BF16)

    y = _dot(xn_ref[...], w_ref[...])
    of_ref[...] = y
    ob_ref[...] = y.astype(BF16)


def _proj(h, g, w, layer):
    n, d = h.shape
    wp = w.shape[2]
    tm = _pick(n, 1024, 16)
    tn = _pick(wp, 768, 128)
    return pl.pallas_call(
        _proj_kernel,
        grid=(n // tm, wp // tn),
        in_specs=[pl.BlockSpec((tm, d), lambda i, j: (i, 0)),
                  pl.BlockSpec((1, d), lambda i, j: (0, 0)),
                  pl.BlockSpec((None, d, tn), lambda i, j: (layer, 0, j))],
        out_specs=[pl.BlockSpec((tm, tn), lambda i, j: (i, j)),
                   pl.BlockSpec((tm, tn), lambda i, j: (i, j))],
        out_shape=[jax.ShapeDtypeStruct((n, wp), F32), jax.ShapeDtypeStruct((n, wp), BF16)],
        scratch_shapes=[pltpu.VMEM((tm, d), BF16)],
        compiler_params=_params("arbitrary", "arbitrary"),
        name="proj",
    )(h, g.reshape(1, d), w)


def _merge_kernel(oa_ref, ob_ref, oc_ref, wa_ref, wb_ref, wc_ref, g0_ref, g1_ref, g2_ref, o_ref):
    m = _sigmoid(g0_ref[...]) * _dot(oa_ref[...], wa_ref[...])
    m = m + _sigmoid(g1_ref[...]) * _dot(ob_ref[...], wb_ref[...])
    m = m + _sigmoid(g2_ref[...]) * _dot(oc_ref[...], wc_ref[...])
    o_ref[...] = m.astype(BF16)


def _merge(o_a, o_b, o_c, pf, wa, wb, wc, d, layer):
    n = o_a.shape[0]
    tm = _pick(n, 512, 16)
    tn = _pick(d, 512, 128)
    gm0 = C_GM // tn
    o_spec = pl.BlockSpec((tm, HW), lambda i, j: (i, 0))
    w_spec = pl.BlockSpec((None, HW, tn), lambda i, j: (layer, 0, j))

    def g_spec(k):
        return pl.BlockSpec((tm, tn), lambda i, j: (i, gm0 + k * (d // tn) + j))

    return pl.pallas_call(
        _merge_kernel,
        grid=(n // tm, d // tn),
        in_specs=[o_spec, o_spec, o_spec, w_spec, w_spec, w_spec, g_spec(0), g_spec(1), g_spec(2)],
        out_specs=pl.BlockSpec((tm, tn), lambda i, j: (i, j)),
        out_shape=jax.ShapeDtypeStruct((n, d), BF16),
        compiler_params=_params("arbitrary", "arbitrary"),
        name="merge",
    )(o_a, o_b, o_c, wa, wb, wc, pf, pf, pf)


def _mm_res_kernel(x_ref, w_ref, r_ref, o_ref):
    o_ref[...] = r_ref[...] + _dot(x_ref[...], w_ref[...])


def _mm_res(x, w, res, layer):
    n, k = x.shape
    d = w.shape[2]
    tm = _pick(n, 512, 16)
    tn = _pick(d, 512, 128)
    return pl.pallas_call(
        _mm_res_kernel,
        grid=(n // tm, d // tn),
        in_specs=[pl.BlockSpec((tm, k), lambda i, j: (i, 0)),
                  pl.BlockSpec((None, k, tn), lambda i, j: (layer, 0, j)),
                  pl.BlockSpec((tm, tn), lambda i, j: (i, j))],
        out_specs=pl.BlockSpec((tm, tn), lambda i, j: (i, j)),
        out_shape=jax.ShapeDtypeStruct((n, d), F32),
        compiler_params=_params("arbitrary", "arbitrary"),
        name="out_proj",
    )(x, w, res)


def _router_kernel(h_ref, g_ref, w_ref, b_ref, xn_ref, comb_ref, mcol_ref, mrow_ref, cnt_ref):
    x = h_ref[...]
    ms = jnp.mean(x * x, axis=-1, keepdims=True)
    xn = x * lax.rsqrt(ms + RMS_EPS) * g_ref[...]
    xn_ref[...] = xn.astype(BF16)
    logits = jnp.dot(xn, w_ref[...], precision=lax.Precision.HIGHEST, preferred_element_type=F32) + b_ref[...]
    lane = _iota(logits.shape, 1)
    lane_f = lane.astype(F32)
    is_grp = lane < N_GROUPS
    gl = jnp.where(is_grp, logits, BELOW_ALL)
    gmax = jnp.max(gl, axis=-1, keepdims=True)
    gsum = jnp.sum(jnp.where(is_grp, jnp.exp(gl - gmax), 0.0), axis=-1, keepdims=True)
    w_grp = 1.0 / gsum
    g_star = jnp.min(jnp.where(is_grp & (gl == gmax), lane_f, 1e9), axis=-1, keepdims=True)
    lo = N_GROUPS + g_star * EXPERTS_PER_GROUP
    in_grp = (lane_f >= lo) & (lane_f < lo + EXPERTS_PER_GROUP)
    el = jnp.where(in_grp, logits, BELOW_ALL)
    v1 = jnp.max(el, axis=-1, keepdims=True)
    i1 = jnp.min(jnp.where(el == v1, lane_f, 1e9), axis=-1, keepdims=True)
    el2 = jnp.where(lane_f == i1, BELOW_ALL, el)
    v2 = jnp.max(el2, axis=-1, keepdims=True)
    i2 = jnp.min(jnp.where(el2 == v2, lane_f, 1e9), axis=-1, keepdims=True)
    e2 = jnp.exp(v2 - v1)
    den = 1.0 + e2
    comb = jnp.where(lane_f == i1, w_grp / den, 0.0) + jnp.where(lane_f == i2, w_grp * e2 / den, 0.0)
    comb_ref[...] = comb
    member = jnp.where(lane_f == g_star, 1.0, 0.0)
    tm = member.shape[0]
    row = _iota(member.shape, 0)
    seen = member
    s = 1
    while s < tm:
        seen = seen + jnp.where(row >= s, pltpu.roll(seen, s, axis=0), 0.0)
        s *= 2
    rank = jnp.sum((seen - member) * member, axis=-1, keepdims=True)
    meta = jnp.where(lane == 0, g_star, jnp.where(lane == 1, rank, 0.0))
    mcol_ref[...] = meta
    mrow_ref[0] = meta.T[0:8]
    cnt_ref[0] = jnp.broadcast_to(seen[tm - 1:tm, :], (8, LANES))


def _router(h, g, w, b):
    n, d = h.shape
    tm = _pick(n, MOE_TILE, LANES)
    nt = n // tm
    return pl.pallas_call(
        _router_kernel,
        grid=(nt,),
        in_specs=[pl.BlockSpec((tm, d), lambda i: (i, 0)),
                  pl.BlockSpec((1, d), lambda i: (0, 0)),
                  pl.BlockSpec((d, LANES), lambda i: (0, 0)),
                  pl.BlockSpec((1, LANES), lambda i: (0, 0))],
        out_specs=[pl.BlockSpec((tm, d), lambda i: (i, 0)),
                   pl.BlockSpec((tm, LANES), lambda i: (i, 0)),
                   pl.BlockSpec((tm, LANES), lambda i: (i, 0)),
                   pl.BlockSpec((1, 8, tm), lambda i: (i, 0, 0)),
                   pl.BlockSpec((1, 8, LANES), lambda i: (i, 0, 0))],
        out_shape=[jax.ShapeDtypeStruct((n, d), BF16), jax.ShapeDtypeStruct((n, LANES), F32),
                   jax.ShapeDtypeStruct((n, LANES), F32), jax.ShapeDtypeStruct((nt, 8, tm), F32),
                   jax.ShapeDtypeStruct((nt, 8, LANES), F32)],
        compiler_params=_params("arbitrary"),
        name="router",
    )(h, g.reshape(1, d), w, b)


def _moe_kernel(cnt_ref, x_ref, c_ref, mcol_ref, mrow_ref, h_ref, wg_ref, wu_ref, wd_ref, o_ref, xc_ref, cc_ref, y_ref, *, ch):
    i, e = pl.program_id(0), pl.program_id(1)
    tm = x_ref.shape[0]
    grp = jnp.right_shift(e, 2)
    grp_f = grp.astype(F32)
    nchunk = (cnt_ref[i * N_GROUPS + grp] + ch - 1) // ch

    @pl.when(e == 0)
    def _():
        o_ref[...] = h_ref[...]

    @pl.when(jnp.bitwise_and(e, EXPERTS_PER_GROUP - 1) == 0)
    def _():
        grp_row, rank_row = mrow_ref[0, 0:1, :], mrow_ref[0, 1:2, :]
        comb = c_ref[...]
        c1 = comb.astype(BF16)
        rest = comb - c1.astype(F32)
        c2 = rest.astype(BF16)
        c3 = (rest - c2.astype(F32)).astype(BF16)

        def gather(c, _):
            base = pl.multiple_of(c * ch, ch)
            slot = (base + _iota((ch, tm), 0)).astype(F32)
            pick = jnp.where((grp_row == grp_f) & (rank_row == slot), 1.0, 0.0).astype(BF16)
            xc_ref[pl.ds(base, ch), :] = _dot(pick, x_ref[...]).astype(BF16)
            cc_ref[pl.ds(base, ch), :] = _dot(pick, c1) + _dot(pick, c2) + _dot(pick, c3)
            y_ref[pl.ds(base, ch), :] = jnp.zeros((ch, y_ref.shape[1]), F32)
            return 0

        lax.fori_loop(0, nchunk, gather, 0)

    def expert(c, _):
        base = pl.multiple_of(c * ch, ch)
        xc = xc_ref[pl.ds(base, ch), :]
        cc = cc_ref[pl.ds(base, ch), :]
        ce = jnp.sum(jnp.where(_iota(cc.shape, 1) == e + N_GROUPS, cc, 0.0), axis=-1, keepdims=True)
        gate = _dot(xc, wg_ref[0])
        up = _dot(xc, wu_ref[0])
        hh = gate * _sigmoid(gate) * up * ce
        y_ref[pl.ds(base, ch), :] += _dot(hh.astype(BF16), wd_ref[0])
        return 0

    lax.fori_loop(0, nchunk, expert, 0)

    @pl.when(jnp.bitwise_and(e, EXPERTS_PER_GROUP - 1) == EXPERTS_PER_GROUP - 1)
    def _():
        grp_col, rank_col = mcol_ref[:, 0:1], mcol_ref[:, 1:2]

        def scatter(c, _):
            base = pl.multiple_of(c * ch, ch)
            slot = (base + _iota((tm, ch), 1)).astype(F32)
            place = jnp.where((grp_col == grp_f) & (rank_col == slot), 1.0, 0.0).astype(BF16)
            y = y_ref[pl.ds(base, ch), :]
            hi = y.astype(BF16)
            lo = (y - hi.astype(F32)).astype(BF16)
            o_ref[...] += _dot(place, hi) + _dot(place, lo)
            return 0

        lax.fori_loop(0, nchunk, scatter, 0)


def _moe(xn, comb, mcol, mrow, counts, h, wg, wu, wd, layer):
    n, d = h.shape
    _, ne, _, ff = wg.shape
    nt, _, tm = mrow.shape
    ch = min(MOE_CHUNK, tm)
    cap = -(-tm // ch) * ch
    kernel = functools.partial(_moe_kernel, ch=ch)
    grid_spec = pltpu.PrefetchScalarGridSpec(
        num_scalar_prefetch=1,
        grid=(nt, ne),
        in_specs=[pl.BlockSpec((tm, d), lambda i, e, cnt: (i, 0)),
                  pl.BlockSpec((tm, LANES), lambda i, e, cnt: (i, 0)),
                  pl.BlockSpec((tm, LANES), lambda i, e, cnt: (i, 0)),
                  pl.BlockSpec((1, 8, tm), lambda i, e, cnt: (i, 0, 0)),
                  pl.BlockSpec((tm, d), lambda i, e, cnt: (i, 0), pipeline_mode=pl.Buffered(1)),
                  pl.BlockSpec((None, 1, d, ff), lambda i, e, cnt: (layer, e, 0, 0)),
                  pl.BlockSpec((None, 1, d, ff), lambda i, e, cnt: (layer, e, 0, 0)),
                  pl.BlockSpec((None, 1, ff, d), lambda i, e, cnt: (layer, e, 0, 0))],
        out_specs=pl.BlockSpec((tm, d), lambda i, e, cnt: (i, 0), pipeline_mode=pl.Buffered(1)),
        scratch_shapes=[pltpu.VMEM((cap, d), BF16), pltpu.VMEM((cap, LANES), F32), pltpu.VMEM((cap, d), F32)],
    )
    return pl.pallas_call(
        kernel,
        grid_spec=grid_spec,
        out_shape=jax.ShapeDtypeStruct((n, d), F32),
        compiler_params=_params("arbitrary", "arbitrary"),
        name="moe",
    )(counts, xn, comb, mcol, mrow, h, wg, wu, wd)


def _final_norm_kernel(h_ref, g_ref, o_ref):
    x = h_ref[...]
    ms = jnp.mean(x * x, axis=-1, keepdims=True)
    o_ref[...] = x * lax.rsqrt(ms + RMS_EPS) * g_ref[...]


def _final_norm(h, g):
    n, d = h.shape
    tm = _pick(n, 512, 8)
    return pl.pallas_call(
        _final_norm_kernel,
        grid=(n // tm,),
        in_specs=[pl.BlockSpec((tm, d), lambda i: (i, 0)), pl.BlockSpec((1, d), lambda i: (0, 0))],
        out_specs=pl.BlockSpec((tm, d), lambda i: (i, 0)),
        out_shape=jax.ShapeDtypeStruct((n, d), F32),
        compiler_params=_params("arbitrary"),
        name="final_norm",
    )(h, g.reshape(1, d))


def _cum_prompt_kernel(m_ref, bf_ref, logf_ref, cum_ref):
    logf = _log_sigmoid(m_ref[:, 0:LANES] + bf_ref[...])
    logf_ref[...] = logf
    t = logf.shape[0]
    row = _iota(logf.shape, 0)
    c = logf
    s = 1
    while s < t:
        c = c + jnp.where(row >= s, pltpu.roll(c, s, axis=0), 0.0)
        s *= 2
    cum_ref[...] = c


def _cum_prompt(pf, bf, nb, t):
    return pl.pallas_call(
        _cum_prompt_kernel,
        grid=(nb,),
        in_specs=[pl.BlockSpec((t, MISC_W), lambda b: (b, C_MISC // MISC_W)),
                  pl.BlockSpec((1, LANES), lambda b: (0, 0))],
        out_specs=[pl.BlockSpec((t, LANES), lambda b: (b, 0)), pl.BlockSpec((t, LANES), lambda b: (b, 0))],
        out_shape=[jax.ShapeDtypeStruct((nb * t, LANES), F32), jax.ShapeDtypeStruct((nb * t, LANES), F32)],
        compiler_params=_params("arbitrary"),
        name="cum_prompt",
    )(pf, bf)


def _stack_pair(q_ref, p, lane):
    qp = q_ref[:, p * LANES:(p + 1) * LANES] * jnp.asarray(SCALE, BF16)
    zero = jnp.zeros_like(qp)
    return jnp.concatenate([jnp.where(lane < HEAD_DIM, qp, zero), jnp.where(lane >= HEAD_DIM, qp, zero)], axis=0)


def _compress_chunks(ch_ref, w1_ref, pe_ref, w2_ref, o_ref):
    nch = ch_ref.shape[1]
    hw = NSA_G * CMP_HIDDEN
    for c in range(2):
        w1 = w1_ref[0, c]
        pe = _dot(pe_ref[0, c].astype(BF16), w1)
        y = _dot(ch_ref[c].astype(BF16), w1)
        out = None
        for g in range(NSA_G):
            lo = slice(g * CMP_HIDDEN, (g + 1) * CMP_HIDDEN)
            hi = slice(hw + g * CMP_HIDDEN, hw + (g + 1) * CMP_HIDDEN)
            pre = y[:, lo] + pltpu.roll(y[:, hi], nch - 1, axis=0) + pe[0:1, lo] + pe[1:2, hi]
            hid = 0.5 * pre * (1.0 + jnp.tanh(math.sqrt(2.0 / math.pi) * (pre + 0.044715 * pre * pre * pre)))
            cg = _dot(hid.astype(BF16), w2_ref[0, c, g])
            out = cg if out is None else out + cg
        o_ref[c, 0] = out.astype(BF16)


def _compress_weight_specs(layer, imap):
    return [pl.BlockSpec((1, 2, CHUNK_W, 2 * NSA_G * CMP_HIDDEN), imap((layer, 0, 0, 0))),
            pl.BlockSpec((1, 2, 8, CHUNK_W), imap((layer, 0, 0, 0))),
            pl.BlockSpec((1, 2, NSA_G, CMP_HIDDEN, LANES), imap((layer, 0, 0, 0, 0)))]


def _compress_prompt_kernel(xk_ref, xv_ref, w1_ref, pe_ref, w2_ref, o_ref, ch_ref):
    nch = ch_ref.shape[1]
    for c, x_ref in enumerate((xk_ref, xv_ref)):
        for r in range(CMP_STRIDE):
            ch_ref[c, :, r * LANES:(r + 1) * LANES] = x_ref[pl.ds(r, nch, stride=CMP_STRIDE), :]
    _compress_chunks(ch_ref, w1_ref, pe_ref, w2_ref, o_ref)


def _compress_prompt(pf, w1big, pe_rows, w2p, layer, nb, t):
    nch = t // CMP_STRIDE
    return pl.pallas_call(
        _compress_prompt_kernel,
        grid=(nb,),
        in_specs=[pl.BlockSpec((t, LANES), lambda b: (b, C_CMP // LANES)),
                  pl.BlockSpec((t, LANES), lambda b: (b, C_CMP // LANES + 1))]
        + _compress_weight_specs(layer, lambda idx: (lambda b: idx)),
        out_specs=pl.BlockSpec((2, 1, nch, LANES), lambda b: (0, b, 0, 0)),
        out_shape=jax.ShapeDtypeStruct((2, nb, nch, LANES), BF16),
        scratch_shapes=[pltpu.VMEM((2, nch, CHUNK_W), F32)],
        compiler_params=_params("arbitrary"),
        name="nsa_compress_prompt",
    )(pf, pf, w1big, pe_rows, w2p)


def _lut(tab, bk):
    parts = [jnp.take_along_axis(tab, bk[:, c:c + LANES], axis=1) for c in range(0, bk.shape[1], LANES)]
    return parts[0] if len(parts) == 1 else jnp.concatenate(parts, axis=1)


def _transpose_bf16(x):
    return x.astype(F32).T.astype(BF16)


def _pair_out(o_t, lane_dtype=BF16):
    own = jnp.concatenate([o_t[0:HEAD_DIM, 0:QB], o_t[HEAD_DIM:LANES, QB:2 * QB]], axis=0)
    return own.T.astype(lane_dtype)


def _fox_t_kernel(q_ref, k_ref, v_ref, cum_ref, cq_ref, o_ref, vt_ref, ckb_ref):
    i = pl.program_id(1)
    npair = N_HEADS // 2
    nkb = vt_ref.shape[0]

    @pl.when(i == 0)
    def _():
        for kb in range(nkb):
            for p in range(npair):
                vt_ref[kb, p * LANES:(p + 1) * LANES, :] = _transpose_bf16(v_ref[kb * KB:(kb + 1) * KB, p * LANES:(p + 1) * LANES])
        for h in range(N_HEADS):
            ckb_ref[h] = jnp.broadcast_to(cum_ref[:, h:h + 1], ckb_ref.shape[1:])

    lane = _iota((QB, LANES), 1)
    qrow = i * QB + jnp.bitwise_and(_iota((1, 2 * QB), 1), QB - 1)
    krow = _iota((KB, 1), 0)
    q2 = [_stack_pair(q_ref, p, lane) for p in range(npair)]
    cq2 = [jnp.concatenate([cq_ref[0, 2 * p, pl.ds(i, 1), :], cq_ref[0, 2 * p + 1, pl.ds(i, 1), :]], axis=1)
           for p in range(npair)]

    def step(kb, carry, masked):
        k0 = pl.multiple_of(kb * KB, KB)
        scores = [_dot_nt(k_ref[pl.ds(k0, KB), p * LANES:(p + 1) * LANES], q2[p]) for p in range(npair)]
        probs, stats = [], []
        for p in range(npair):
            m, l, _ = carry[p]
            ck = jnp.concatenate([ckb_ref[2 * p, pl.ds(k0, KB), :], ckb_ref[2 * p + 1, pl.ds(k0, KB), :]], axis=1)
            s = scores[p] + cq2[p] - ck
            if masked:
                s = jnp.where((k0 + krow) <= qrow, s, NEG_INF)
            m_new = jnp.maximum(m, jnp.max(s, axis=0, keepdims=True))
            pr = jnp.exp(s - m_new)
            alpha = jnp.exp(m - m_new)
            probs.append(pr.astype(BF16))
            stats.append((m_new, alpha * l + jnp.sum(pr, axis=0, keepdims=True), alpha))
        out = []
        for p in range(npair):
            m_new, l, alpha = stats[p]
            acc = alpha * carry[p][2] + _dot(vt_ref[kb, p * LANES:(p + 1) * LANES, :], probs[p])
            out.append((m_new, l, acc))
        return tuple(out)

    init = tuple((jnp.full((1, 2 * QB), NEG_INF, F32), jnp.zeros((1, 2 * QB), F32), jnp.zeros((LANES, 2 * QB), F32))
                 for _ in range(npair))
    last = (i * QB) // KB
    carry = lax.fori_loop(0, last, lambda kb, c: step(kb, c, False), init)
    carry = step(last, carry, True)
    for p in range(npair):
        m, l, acc = carry[p]
        o_ref[:, p * LANES:(p + 1) * LANES] = _pair_out(acc / jnp.maximum(l, TINY))


def _fox_prompt_t(pb, cum, cum_t, nb, t):
    nq = t // QB
    return pl.pallas_call(
        _fox_t_kernel,
        grid=(nb, nq),
        in_specs=[pl.BlockSpec((QB, HW), lambda b, i: (b * nq + i, C_QA // HW)),
                  pl.BlockSpec((t, HW), lambda b, i: (b, C_KA // HW)),
                  pl.BlockSpec((t, HW), lambda b, i: (b, C_VA // HW)),
                  pl.BlockSpec((t, LANES), lambda b, i: (b, 0)),
                  pl.BlockSpec((1, N_HEADS, nq, QB), lambda b, i: (b, 0, 0, 0))],
        out_specs=pl.BlockSpec((QB, HW), lambda b, i: (b * nq + i, 0)),
        out_shape=jax.ShapeDtypeStruct((nb * t, HW), BF16),
        scratch_shapes=[pltpu.VMEM((t // KB, HW, KB), BF16), pltpu.VMEM((N_HEADS, t, LANES), F32)],
        compiler_params=_params("arbitrary", "arbitrary"),
        name="fox_prompt",
    )(pb, pb, pb, cum, cum_t)


def _sb_t_kernel(q_ref, k_ref, v_ref, o_ref, vt_ref):
    i = pl.program_id(1)
    npair = N_HEADS // 2
    nkb = vt_ref.shape[0]

    @pl.when(i == 0)
    def _():
        for kb in range(nkb):
            for p in range(npair):
                vt_ref[kb, p * LANES:(p + 1) * LANES, :] = _transpose_bf16(v_ref[kb * KB:(kb + 1) * KB, p * LANES:(p + 1) * LANES])

    lane = _iota((QB, LANES), 1)
    qrow = i * QB + jnp.bitwise_and(_iota((1, 2 * QB), 1), QB - 1)
    krow = _iota((KB, 1), 0)
    later = jnp.where(_iota((KB, KB), 1) > _iota((KB, KB), 0), 1.0, 0.0).astype(BF16)
    q2 = [_stack_pair(q_ref, p, lane) for p in range(npair)]

    def step(kb, carry, masked):
        k0 = pl.multiple_of(kb * KB, KB)
        vis = (k0 + krow) < qrow
        zs = [_dot_nt(k_ref[pl.ds(k0, KB), p * LANES:(p + 1) * LANES], q2[p]) for p in range(npair)]
        sps = [_softplus(z) for z in zs]
        lks = [jnp.where(vis, -sp, 0.0) if masked else -sp for sp in sps]
        his = [lk.astype(BF16) for lk in lks]
        los = [(lk - hi.astype(F32)).astype(BF16) for lk, hi in zip(lks, his)]
        betweens = [_dot(later, hi) + _dot(later, lo) for hi, lo in zip(his, los)]
        weights = []
        for p in range(npair):
            a = jnp.exp(zs[p] - sps[p] + betweens[p] + carry[p][0])
            weights.append((jnp.where(vis, a, 0.0) if masked else a).astype(BF16))
        out = []
        for p in range(npair):
            r, acc = carry[p]
            acc = acc + _dot(vt_ref[kb, p * LANES:(p + 1) * LANES, :], weights[p])
            out.append((r + jnp.sum(lks[p], axis=0, keepdims=True), acc))
        return tuple(out)

    init = tuple((jnp.zeros((1, 2 * QB), F32), jnp.zeros((LANES, 2 * QB), F32)) for _ in range(npair))
    last = (i * QB) // KB
    carry = step(last, init, True)
    carry = lax.fori_loop(0, last, lambda n, c: step(last - 1 - n, c, False), carry)
    for p in range(npair):
        o_ref[:, p * LANES:(p + 1) * LANES] = _pair_out(carry[p][1])


def _sb_prompt_t(pb, nb, t):
    nq = t // QB
    return pl.pallas_call(
        _sb_t_kernel,
        grid=(nb, nq),
        in_specs=[pl.BlockSpec((QB, HW), lambda b, i: (b * nq + i, C_QC // HW)),
                  pl.BlockSpec((t, HW), lambda b, i: (b, C_KC // HW)),
                  pl.BlockSpec((t, HW), lambda b, i: (b, C_VC // HW))],
        out_specs=pl.BlockSpec((QB, HW), lambda b, i: (b * nq + i, 0)),
        out_shape=jax.ShapeDtypeStruct((nb * t, HW), BF16),
        scratch_shapes=[pltpu.VMEM((t // KB, HW, KB), BF16)],
        compiler_params=_params("arbitrary", "arbitrary"),
        name="sb_prompt",
    )(pb, pb, pb)


def _select_blocks_t(score, qrow, n_s, sc_ref):
    blk = _iota(score.shape, 0)
    blk_f = blk.astype(F32)
    valid = blk * SEL_BLOCK <= qrow
    forced = (blk == jnp.right_shift(qrow, 6)) | (blk == 0)
    sc = jnp.where(forced, FORCE_SCORE, jnp.where(valid, score, NEG_INF))
    sc = jnp.where(blk < n_s, sc, BELOW_ALL)
    rows = min(score.shape[0], -(-n_s // 8) * 8)
    sc, blk = sc[0:rows], _iota((rows, score.shape[1]), 0)
    sc_ref[0:rows, :] = sc
    rank = jnp.zeros(sc.shape, F32)
    for k in range(n_s):
        other = sc_ref[k:k + 1, :]
        rank = rank + jnp.where((other > sc) | ((other == sc) & (blk > k)), 1.0, 0.0)
    sel = jnp.where((rank < min(SEL_TOPK, n_s)) & (blk < n_s), 1.0, 0.0)
    if rows < score.shape[0]:
        sel = jnp.concatenate([sel, jnp.zeros((score.shape[0] - rows, score.shape[1]), F32)], axis=0)
    return sel


def _nsa_t_kernel(t5_ref, q_ref, ck_ref, cv_ref, sel_ref, win_ref, misc_ref, o_ref, svt_ref, wvt_ref, sc_ref, *, n_s):
    i = pl.program_id(1)
    q0 = i * QB
    nch = ck_ref.shape[2]
    nblk = svt_ref.shape[0]
    cols = N_HEADS * QB

    @pl.when(i == 0)
    def _():
        for kb in range(nblk):
            svt_ref[kb] = _transpose_bf16(sel_ref[kb * QB:(kb + 1) * QB, LANES:2 * LANES])
            wvt_ref[kb] = _transpose_bf16(win_ref[kb * QB:(kb + 1) * QB, LANES:2 * LANES])

    lane = _iota((QB, LANES), 1)
    qrow = q0 + _iota((1, QB), 1)

    def per_head(f):
        return jnp.concatenate([f(h) for h in range(N_HEADS)], axis=1)

    def per_group(x0, x1):
        return jnp.concatenate([x0] * NSA_HG + [x1] * NSA_HG, axis=1)

    def bias_of(rel):
        bk = _bucket(rel)
        return per_head(lambda h: _lut(jnp.broadcast_to(t5_ref[h:h + 1, :], (rel.shape[0], LANES)), bk))

    def q_head(head):
        g = head // NSA_HG
        x = q_ref[:, (head // 2) * LANES:(head // 2 + 1) * LANES].astype(F32) * SCALE
        if head % 2 != g:
            x = pltpu.roll(x, HEAD_DIM, axis=1)
        return jnp.where((lane >= HEAD_DIM) == (g == 1), x, 0.0).astype(BF16)

    q8 = jnp.concatenate([q_head(h) for h in range(N_HEADS)], axis=0)
    rel_d = _iota((QB, QB), 1) - _iota((QB, QB), 0)
    bias_d = bias_of(rel_d)
    bias_p = bias_of(rel_d + QB)
    bias_far = per_head(lambda h: jnp.broadcast_to(t5_ref[h:h + 1, N_BUCKETS - 1:N_BUCKETS], (1, QB)))

    nrow = _iota((nch, QB), 0)
    rel_c = qrow - (nrow * CMP_STRIDE + (CMP_BLOCK - 1))
    vis_c = jnp.where((rel_c >= 0) & (nrow < nch - 1), 1.0, 0.0)
    mask_c = per_group(vis_c, vis_c) > 0.5
    lc = jnp.where(mask_c, _dot_nt(ck_ref[0, 0], q8) + bias_of(rel_c), NEG_INF)
    ec = jnp.where(mask_c, jnp.exp(lc - jnp.max(lc, axis=0, keepdims=True)), 0.0)
    pc = ec / jnp.maximum(jnp.sum(ec, axis=0, keepdims=True), TINY)
    o_cmp = _dot(_transpose_bf16(cv_ref[0, 0]), pc.astype(BF16))
    psums = []
    for g in range(NSA_G):
        ps = pc[:, g * NSA_HG * QB:(g * NSA_HG + 1) * QB]
        for hg in range(1, NSA_HG):
            ps = ps + pc[:, (g * NSA_HG + hg) * QB:(g * NSA_HG + hg + 1) * QB]
        psums.append(ps)
    psum = jnp.concatenate(psums, axis=1)
    cs = _iota((LANES, nch), 1) * CMP_STRIDE
    ss = _iota((LANES, nch), 0) * SEL_BLOCK
    ov_t = (jnp.maximum(jnp.minimum(cs + CMP_BLOCK, ss + SEL_BLOCK) - jnp.maximum(cs, ss), 0).astype(F32)
            / CMP_BLOCK).astype(BF16)
    p_hi = psum.astype(BF16)
    p_lo = (psum - p_hi.astype(F32)).astype(BF16)
    selm = _select_blocks_t(_dot(ov_t, p_hi) + _dot(ov_t, p_lo), jnp.concatenate([qrow] * NSA_G, axis=1),
                            n_s, sc_ref).astype(BF16)

    def attend(steps):
        loaded = []
        for k_ref, vt_ref, k0, width, _, _, _, real in steps:
            k0 = pl.multiple_of(k0, QB)
            kk = k_ref[pl.ds(k0, width), 0:LANES]
            kb = jnp.right_shift(k0, 7)
            vt = vt_ref[kb] if width == QB else jnp.concatenate([vt_ref[kb], vt_ref[kb + 1]], axis=1)
            if real is not None:
                kk = jnp.where(real, kk, jnp.zeros_like(kk))
                vt = jnp.where(real, vt, jnp.zeros_like(vt))
            loaded.append((kk, vt))
        scores = [_dot_nt(kk, q8) for kk, _ in loaded]
        soft = []
        for (_, _, _, _, hide, bias, (m, l, _), _), s in zip(steps, scores):
            s = s + (bias + hide)
            m_new = jnp.maximum(m, jnp.max(s, axis=0, keepdims=True))
            pr = jnp.exp(s - m_new)
            alpha = jnp.exp(m - m_new)
            soft.append((m_new, alpha * l + jnp.sum(pr, axis=0, keepdims=True), alpha, pr.astype(BF16)))
        return [(m_new, l, alpha * step[6][2] + _dot(vt, pr))
                for step, (_, vt), (m_new, l, alpha, pr) in zip(steps, loaded, soft)]

    def init():
        return (jnp.full((1, cols), NEG_INF, F32), jnp.zeros((1, cols), F32), jnp.zeros((LANES, cols), F32))

    def win_step(dlt, carry):
        rel = qrow - ((i - dlt) * QB + _iota((QB, 1), 0))
        hide = jnp.where((rel >= 0) & (rel < WINDOW), 0.0, NEG_INF)
        bias = bias_d if dlt == 0 else (bias_p if dlt == 1 else bias_far)
        return (win_ref, wvt_ref, jnp.maximum(i - dlt, 0) * QB, QB, per_group(hide, hide), bias, carry,
                None if dlt == 0 else i - dlt >= 0)

    def sel_step(k0, width, bias, carry, causal=False, valid=None):
        blk = jnp.right_shift(k0 + _iota((width, LANES), 0), 6)
        expand = jnp.where(_iota((width, LANES), 1) == blk, 1.0, 0.0).astype(BF16)
        picked = _dot(expand, selm)
        if causal:
            picked = picked * jnp.concatenate([jnp.where((k0 + _iota((width, 1), 0)) <= qrow, 1.0, 0.0)] * NSA_G, axis=1)
        if valid is not None:
            picked = picked * jnp.where(valid, 1.0, 0.0)
        hide = jnp.where(picked > 0.5, 0.0, NEG_INF)
        return (sel_ref, svt_ref, k0, width, per_group(hide[:, 0:QB], hide[:, QB:2 * QB]), bias, carry, None)

    k_prev = jnp.maximum(q0 - QB, 0)
    k_odd = jnp.maximum(q0 - 2 * QB, 0)
    c_win, c_sel = attend([win_step(0, init()), sel_step(q0, QB, bias_d, init(), causal=True)])
    c_win, c_sel = attend([win_step(1, c_win), sel_step(k_prev, QB, bias_p, c_sel, valid=i >= 1)])
    c_win, c_sel = attend([win_step(2, c_win),
                           sel_step(k_odd, QB, bias_far, c_sel, valid=(i >= 2) & (jnp.bitwise_and(i, 1) == 0))])
    for dlt in range(3, WINDOW // QB + 1):
        (c_win,) = attend([win_step(dlt, c_win)])
    o_win = c_win[2] / jnp.maximum(c_win[1], TINY)
    c_sel = lax.fori_loop(0, jnp.right_shift(jnp.maximum(i - 1, 0), 1),
                          lambda kb, c: attend([sel_step(kb * KB, KB, bias_far, c)])[0], c_sel)
    o_sel = c_sel[2] / jnp.maximum(c_sel[1], TINY)

    sig_t = _sigmoid(misc_ref[:, 0:LANES]).T

    def gate(br):
        return per_head(lambda h: sig_t[8 + br * N_HEADS + h:9 + br * N_HEADS + h, :])

    o8 = gate(0) * o_cmp + gate(1) * o_sel + gate(2) * o_win
    for p in range(N_HEADS // 2):
        own = []
        for head in (2 * p, 2 * p + 1):
            g = head // NSA_HG
            own.append(o8[g * HEAD_DIM:(g + 1) * HEAD_DIM, head * QB:(head + 1) * QB])
        o_ref[:, p * LANES:(p + 1) * LANES] = jnp.concatenate(own, axis=0).T.astype(BF16)


def _nsa_prompt_t(pb, pf, comp, t5, nb, t):
    nq = t // QB
    nch = comp.shape[2]
    kernel = functools.partial(_nsa_t_kernel, n_s=-(-t // SEL_BLOCK))
    return pl.pallas_call(
        kernel,
        grid=(nb, nq),
        in_specs=[pl.BlockSpec((N_HEADS, LANES), lambda b, i: (0, 0)),
                  pl.BlockSpec((QB, HW), lambda b, i: (b * nq + i, C_QB // HW)),
                  pl.BlockSpec((1, 1, nch, LANES), lambda b, i: (0, b, 0, 0)),
                  pl.BlockSpec((1, 1, nch, LANES), lambda b, i: (1, b, 0, 0)),
                  pl.BlockSpec((t, 2 * LANES), lambda b, i: (b, C_SEL // (2 * LANES))),
                  pl.BlockSpec((t, 2 * LANES), lambda b, i: (b, C_WIN // (2 * LANES))),
                  pl.BlockSpec((QB, MISC_W), lambda b, i: (b * nq + i, C_MISC // MISC_W))],
        out_specs=pl.BlockSpec((QB, HW), lambda b, i: (b * nq + i, 0)),
        out_shape=jax.ShapeDtypeStruct((nb * t, HW), BF16),
        scratch_shapes=[pltpu.VMEM((nq, LANES, QB), BF16), pltpu.VMEM((nq, LANES, QB), BF16),
                        pltpu.VMEM((LANES, NSA_G * QB), F32)],
        compiler_params=_params("arbitrary", "arbitrary"),
        name="nsa_prompt",
    )(t5, pb, comp, comp, pb, pb, pf)


def _page_specs(npg, block, layer, tail):
    def spec(j):
        return pl.BlockSpec(block, lambda b, pt: (pt[b * npg + j], layer) + tail)
    return [spec(j) for j in range(npg)]


def _cum_sample_body(pages, m_ref, bf_ref, logf_ref):
    npg = len(pages)
    x = jnp.concatenate([pg[0, 0] for pg in pages], axis=0)
    lane = _iota(x.shape, 1)
    s = 1
    while s < PAGE:
        x = x + jnp.where(lane >= s, pltpu.roll(x, s, axis=1), 0.0)
        s *= 2
    off = jnp.zeros((N_HEADS, 1), F32)
    cum_pages = []
    for j in range(npg):
        blk = x[j * N_HEADS:(j + 1) * N_HEADS]
        cum_pages.append(blk + off)
        off = off + blk[:, PAGE - 1:PAGE]
    eye = _iota((N_HEADS, LANES), 0) == _iota((N_HEADS, LANES), 1)
    tot = jnp.sum(jnp.where(eye, off, 0.0), axis=0, keepdims=True)
    logf = _log_sigmoid(m_ref[:, 0:LANES] + bf_ref[...])
    logf_ref[...] = logf
    row = _iota(logf.shape, 0)
    c = logf
    s = 1
    while s < DEC_T:
        c = c + jnp.where(row >= s, pltpu.roll(c, s, axis=0), 0.0)
        s *= 2
    return cum_pages, c + tot


def _block_diag_q(q):
    qt = jnp.concatenate([q] * N_HEADS, axis=0)
    same = jnp.right_shift(_iota(qt.shape, 1), 6) == jnp.right_shift(_iota(qt.shape, 0), 3)
    return jnp.where(same, qt * SCALE, 0.0).astype(BF16)


def _rows_per_head(x):
    return jnp.concatenate([jnp.broadcast_to(x[h:h + 1], (DEC_T, x.shape[1])) for h in range(N_HEADS)], axis=0)


def _col_per_head(x):
    return jnp.concatenate([x[:, h:h + 1] for h in range(N_HEADS)], axis=0)


def _own_head_lanes(acc):
    lane_h = jnp.right_shift(_iota((DEC_T, HW), 1), 6)
    out = jnp.zeros((DEC_T, HW), F32)
    for h in range(N_HEADS):
        out = jnp.where(lane_h == h, acc[h * DEC_T:(h + 1) * DEC_T], out)
    return out


def _pad_rows(x, n):
    return jnp.concatenate([x, jnp.zeros((n - x.shape[0], x.shape[1]), x.dtype)], axis=0)


def _new_key_mask(strict):
    shape = (N_HEADS * DEC_T, PAGE)
    t_row = jnp.bitwise_and(_iota(shape, 0), DEC_T - 1)
    return (_iota(shape, 1) < t_row) if strict else (_iota(shape, 1) <= t_row)


def _fox_sample_body(pages, q_ref, k_ref, v_ref, cum_pages, cum_new, o_ref):
    npg = len(pages)
    qbd = _block_diag_q(q_ref[...])
    cq = _col_per_head(cum_new)
    cum_new_t = _pad_rows(cum_new, LANES).T[0:N_HEADS]
    raw = [_dot(qbd, pages[j][0, 0, 0].astype(BF16)) for j in range(npg)]
    s_pages = [raw[j] + cq - _rows_per_head(cum_pages[j]) for j in range(npg)]
    k_new = _pad_rows(k_ref[...], PAGE).astype(BF16)
    s_new = _dot_nt(qbd, k_new) + cq - _rows_per_head(cum_new_t)
    mask_new = _new_key_mask(strict=False)
    s_new = jnp.where(mask_new, s_new, NEG_INF)
    m = jnp.max(s_new, axis=-1, keepdims=True)
    for s in s_pages:
        m = jnp.maximum(m, jnp.max(s, axis=-1, keepdims=True))
    p_new = jnp.where(mask_new, jnp.exp(s_new - m), 0.0)
    l = jnp.sum(p_new, axis=-1, keepdims=True)
    acc = _dot(p_new.astype(BF16), _pad_rows(v_ref[...], PAGE).astype(BF16))
    for j in range(npg):
        pr = jnp.exp(s_pages[j] - m)
        l = l + jnp.sum(pr, axis=-1, keepdims=True)
        acc = acc + _dot_nt(pr.astype(BF16), pages[j][0, 0, 1].astype(BF16))
    o_ref[...] = _own_head_lanes(acc / jnp.maximum(l, TINY))


def _sb_sample_body(pages, q_ref, k_ref, v_ref, o_ref):
    npg = len(pages)
    qbd = _block_diag_q(q_ref[...])
    tri = _tri(PAGE)
    z = _dot_nt(qbd, _pad_rows(k_ref[...], PAGE).astype(BF16))
    mask = _new_key_mask(strict=True)
    sp = _softplus(z)
    lk = jnp.where(mask, -sp, 0.0)
    between = _split_dot(lk, tri)
    a = jnp.where(mask, jnp.exp(z - sp + between), 0.0)
    acc = _dot(a.astype(BF16), _pad_rows(v_ref[...], PAGE).astype(BF16))
    r = jnp.sum(lk, axis=-1, keepdims=True)
    rows = N_HEADS * DEC_T
    zs = [_dot(qbd, pages[j][0, 0, 0].astype(BF16)) for j in range(npg)]
    sps = [_softplus(z) for z in zs]
    within = _split_dot(jnp.concatenate([-sp for sp in sps], axis=0), tri)
    for j in reversed(range(npg)):
        a = jnp.exp(zs[j] - sps[j] + within[j * rows:(j + 1) * rows] + r)
        acc = acc + _dot_nt(a.astype(BF16), pages[j][0, 0, 1].astype(BF16))
        r = r - jnp.sum(sps[j], axis=-1, keepdims=True)
    o_ref[...] = _own_head_lanes(acc)


def _compress_sample_body(pages, w1_ref, pe_ref, w2_ref, o_ref, ch_ref, xp_ref):
    npg = len(pages)
    per_page = PAGE // CMP_STRIDE
    for j in range(npg):
        for c in range(2):
            xp_ref[2 * j + c] = pages[j][0, 0, 0, c].T
    for j in range(npg):
        for c in range(2):
            for r in range(CMP_STRIDE):
                ch_ref[c, j * per_page:(j + 1) * per_page, r * LANES:(r + 1) * LANES] = (
                    xp_ref[2 * j + c, pl.ds(r, per_page, stride=CMP_STRIDE), :])
    _compress_chunks(ch_ref, w1_ref, pe_ref, w2_ref, o_ref)


def _sample_fox_kernel(pt_ref, *refs, npg):
    logf_pages, fox_pages, cmp_pages = refs[0:npg], refs[npg:2 * npg], refs[2 * npg:3 * npg]
    (m_ref, bf_ref, q_ref, k_ref, v_ref, w1_ref, pe_ref, w2_ref,
     logf_ref, o_ref, comp_ref, ch_ref, xp_ref) = refs[3 * npg:]
    cum_pages, cum_new = _cum_sample_body(logf_pages, m_ref, bf_ref, logf_ref)
    _fox_sample_body(fox_pages, q_ref, k_ref, v_ref, cum_pages, cum_new, o_ref)
    _compress_sample_body(cmp_pages, w1_ref, pe_ref, w2_ref, comp_ref, ch_ref, xp_ref)


def _sample_fox(pt, logf_view, fox_view, nsa_view, pf, bf, w1big, pe_rows, w2p, layer, nbs, npg, row0):
    kernel = functools.partial(_sample_fox_kernel, npg=npg)
    nch = npg * PAGE // CMP_STRIDE
    rb = row0 // DEC_T
    grid_spec = pltpu.PrefetchScalarGridSpec(
        num_scalar_prefetch=1,
        grid=(nbs,),
        in_specs=_page_specs(npg, (1, 1, N_HEADS, PAGE), layer, (0, 0))
        + _page_specs(npg, (1, 1, 2, HW, PAGE), layer, (0, 0, 0))
        + _page_specs(npg, (1, 1, 1, 2, LANES, PAGE), layer, (0, 0, 0, 0))
        + [pl.BlockSpec((DEC_T, MISC_W), lambda b, pt: (rb + b, C_MISC // MISC_W)),
           pl.BlockSpec((1, LANES), lambda b, pt: (0, 0)),
           pl.BlockSpec((DEC_T, HW), lambda b, pt: (rb + b, C_QA // HW)),
           pl.BlockSpec((DEC_T, HW), lambda b, pt: (rb + b, C_KA // HW)),
           pl.BlockSpec((DEC_T, HW), lambda b, pt: (rb + b, C_VA // HW))]
        + _compress_weight_specs(layer, lambda idx: (lambda b, pt: idx)),
        out_specs=[pl.BlockSpec((DEC_T, LANES), lambda b, pt: (b, 0)),
                   pl.BlockSpec((DEC_T, HW), lambda b, pt: (b, 0)),
                   pl.BlockSpec((2, 1, nch, LANES), lambda b, pt: (0, b, 0, 0))],
        scratch_shapes=[pltpu.VMEM((2, nch, CHUNK_W), F32), pltpu.VMEM((2 * npg, PAGE, LANES), F32)],
    )
    return pl.pallas_call(
        kernel,
        grid_spec=grid_spec,
        out_shape=[jax.ShapeDtypeStruct((nbs * DEC_T, LANES), F32),
                   jax.ShapeDtypeStruct((nbs * DEC_T, HW), F32),
                   jax.ShapeDtypeStruct((2, nbs, nch, LANES), BF16)],
        compiler_params=_params("arbitrary"),
        name="sample_fox_compress",
    )(pt, *([logf_view] * npg), *([fox_view] * npg), *([nsa_view] * npg), pf, bf, pf, pf, pf, w1big, pe_rows, w2p)


def _nsa_sample_body(pages, q_ref, seln_ref, winn_ref, misc_ref, ck_ref, cv_ref, wst_ref, t5_ref, exp_ref, o_ref, n_s):
    npg = len(pages)
    past = npg * PAGE
    rows = N_HEADS * DEC_T
    nch = ck_ref.shape[2]
    wb = wst_ref.shape[4]
    lane8 = _iota((DEC_T, LANES), 1)
    t5c = t5_ref[...]

    def lut(rel):
        return _lut(t5c, _bucket(rel))

    def t_of(shape):
        return jnp.bitwise_and(_iota(shape, 0), DEC_T - 1)

    def g_rows(x):
        return jnp.concatenate([x] * NSA_HG, axis=0)

    q = q_ref[...]
    qrows = []
    for head in range(N_HEADS):
        g = head // NSA_HG
        x = q[:, (head // 2) * LANES:(head // 2 + 1) * LANES] * SCALE
        if head % 2 != g:
            x = pltpu.roll(x, HEAD_DIM, axis=1)
        qrows.append(jnp.where((lane8 >= HEAD_DIM) == (g == 1), x, 0.0))
    qbd = jnp.concatenate(qrows, axis=0).astype(BF16)
    bias_far = t5c[:, N_BUCKETS - 1:N_BUCKETS]

    raw_c = _dot_nt(qbd, ck_ref[0, 0])
    raw_pages = [_dot(qbd, pages[j][0, 0, 0, 0].astype(BF16)) for j in range(npg)]
    seln = _pad_rows(seln_ref[...], PAGE).astype(BF16)
    winn = _pad_rows(winn_ref[...], PAGE).astype(BF16)
    raw_new = _dot_nt(qbd, seln[:, 0:LANES])
    raw_w = _dot(qbd, wst_ref[0, 0, 0].astype(BF16))
    raw_wn = _dot_nt(qbd, winn[:, 0:LANES])

    qpos_c = past + t_of((rows, nch))
    rel_c = qpos_c - (_iota((rows, nch), 1) * CMP_STRIDE + (CMP_BLOCK - 1))
    mask_c = (rel_c >= 0) & (_iota((rows, nch), 1) < nch - 1)
    pc = _masked_softmax(raw_c + lut(rel_c), mask_c)
    psums = []
    for g in range(NSA_G):
        base = g * NSA_HG * DEC_T
        psum = pc[base:base + DEC_T]
        for hg in range(1, NSA_HG):
            psum = psum + pc[base + hg * DEC_T:base + (hg + 1) * DEC_T]
        psums.append(psum)

    rel_new = t_of((rows, PAGE)) - _iota((rows, PAGE), 1)
    in_new = _iota((rows, PAGE), 1) < DEC_T
    bias_new = lut(rel_new)
    rel_w = (wb + t_of((rows, wb))) - _iota((rows, wb), 1)
    mask_w = (rel_w >= 0) & (rel_w < WINDOW)
    s_w = jnp.where(mask_w, raw_w + lut(rel_w), NEG_INF)
    mask_wn = (rel_new >= 0) & (rel_new < WINDOW) & in_new
    s_wn = jnp.where(mask_wn, raw_wn + bias_new, NEG_INF)
    m = jnp.maximum(jnp.max(s_w, axis=-1, keepdims=True), jnp.max(s_wn, axis=-1, keepdims=True))
    p_w = jnp.where(mask_w, jnp.exp(s_w - m), 0.0)
    p_wn = jnp.where(mask_wn, jnp.exp(s_wn - m), 0.0)
    l_w = jnp.sum(p_w, axis=-1, keepdims=True) + jnp.sum(p_wn, axis=-1, keepdims=True)

    score = _split_dot(jnp.concatenate(psums, axis=0), _overlap(nch))
    o_cmp = _dot(pc.astype(BF16), cv_ref[0, 0])
    acc_w = _dot_nt(p_w.astype(BF16), wst_ref[0, 0, 1].astype(BF16)) + _dot(p_wn.astype(BF16), winn[:, LANES:2 * LANES])
    o_win = acc_w / jnp.maximum(l_w, TINY)

    qpos_g = past + jnp.bitwise_and(_iota((NSA_G * DEC_T, 1), 0), DEC_T - 1)
    picked = _select_blocks(score, qpos_g, n_s)
    picked = jnp.concatenate([g_rows(picked[g * DEC_T:(g + 1) * DEC_T]) for g in range(NSA_G)], axis=0)
    sel_past = _dot(picked.astype(BF16), exp_ref[...])
    last_blk = past // SEL_BLOCK
    sel_new = jnp.sum(jnp.where(_iota((rows, LANES), 1) == last_blk, picked, 0.0), axis=-1, keepdims=True) > 0.5
    s_pages, m_pages = [], []
    for j in range(npg):
        if past - (j + 1) * PAGE + 1 >= MAX_DISTANCE:
            s = raw_pages[j] + bias_far
        else:
            s = raw_pages[j] + lut(past + t_of((rows, PAGE)) - (j * PAGE + _iota((rows, PAGE), 1)))
        mk = sel_past[:, j * PAGE:(j + 1) * PAGE] > 0.5
        s_pages.append(jnp.where(mk, s, NEG_INF))
        m_pages.append(mk)
    mask_n = sel_new & (rel_new >= 0) & in_new
    s_new = jnp.where(mask_n, raw_new + bias_new, NEG_INF)
    m = jnp.max(s_new, axis=-1, keepdims=True)
    for s in s_pages:
        m = jnp.maximum(m, jnp.max(s, axis=-1, keepdims=True))
    p_new = jnp.where(mask_n, jnp.exp(s_new - m), 0.0)
    l = jnp.sum(p_new, axis=-1, keepdims=True)
    probs = []
    for j in range(npg):
        pr = jnp.where(m_pages[j], jnp.exp(s_pages[j] - m), 0.0)
        l = l + jnp.sum(pr, axis=-1, keepdims=True)
        probs.append(pr.astype(BF16))
    acc = _dot(p_new.astype(BF16), seln[:, LANES:2 * LANES])
    for j in range(npg):
        acc = acc + _dot_nt(probs[j], pages[j][0, 0, 0, 1].astype(BF16))
    o_sel = acc / jnp.maximum(l, TINY)

    sig = _sigmoid(misc_ref[:, 0:LANES])

    def gate(br):
        return jnp.concatenate(
            [sig[:, 8 + br * N_HEADS + h:9 + br * N_HEADS + h] for h in range(N_HEADS)], axis=0)

    o_all = gate(0) * o_cmp + gate(1) * o_sel + gate(2) * o_win
    pieces = []
    for head in range(N_HEADS):
        x = o_all[head * DEC_T:(head + 1) * DEC_T]
        if head % 2 != head // NSA_HG:
            x = pltpu.roll(x, HEAD_DIM, axis=1)
        pieces.append(x)
    for p in range(N_HEADS // 2):
        o_ref[:, p * LANES:(p + 1) * LANES] = jnp.where(lane8 < HEAD_DIM, pieces[2 * p], pieces[2 * p + 1])


def _sample_sb_nsa_kernel(pt_ref, *refs, npg, n_s):
    sb_pages, sel_pages = refs[0:npg], refs[npg:2 * npg]
    (qc_ref, kc_ref, vc_ref, q_ref, seln_ref, winn_ref, misc_ref, ck_ref, cv_ref, wst_ref, t5_ref, exp_ref,
     o_sb_ref, o_nsa_ref) = refs[2 * npg:]
    _sb_sample_body(sb_pages, qc_ref, kc_ref, vc_ref, o_sb_ref)
    _nsa_sample_body(sel_pages, q_ref, seln_ref, winn_ref, misc_ref, ck_ref, cv_ref, wst_ref, t5_ref, exp_ref,
                     o_nsa_ref, n_s)


def _sample_sb_nsa(pt, sb_view, nsa_view, pf, comp, win_view, t5col, expand, layer, nbs, npg, row0):
    nch = comp.shape[2]
    wb = win_view.shape[4]
    past = npg * PAGE
    kernel = functools.partial(_sample_sb_nsa_kernel, npg=npg, n_s=-(-(past + DEC_T) // SEL_BLOCK))
    rb = row0 // DEC_T
    grid_spec = pltpu.PrefetchScalarGridSpec(
        num_scalar_prefetch=1,
        grid=(nbs,),
        in_specs=_page_specs(npg, (1, 1, 2, HW, PAGE), layer, (0, 0, 0))
        + _page_specs(npg, (1, 1, 1, 2, LANES, PAGE), layer, (1, 0, 0, 0)) + [
            pl.BlockSpec((DEC_T, HW), lambda b, pt: (rb + b, C_QC // HW)),
            pl.BlockSpec((DEC_T, HW), lambda b, pt: (rb + b, C_KC // HW)),
            pl.BlockSpec((DEC_T, HW), lambda b, pt: (rb + b, C_VC // HW)),
            pl.BlockSpec((DEC_T, HW), lambda b, pt: (rb + b, C_QB // HW)),
            pl.BlockSpec((DEC_T, 2 * LANES), lambda b, pt: (rb + b, C_SEL // (2 * LANES))),
            pl.BlockSpec((DEC_T, 2 * LANES), lambda b, pt: (rb + b, C_WIN // (2 * LANES))),
            pl.BlockSpec((DEC_T, MISC_W), lambda b, pt: (rb + b, C_MISC // MISC_W)),
            pl.BlockSpec((1, 1, nch, LANES), lambda b, pt: (0, b, 0, 0)),
            pl.BlockSpec((1, 1, nch, LANES), lambda b, pt: (1, b, 0, 0)),
            pl.BlockSpec((1, 1, 2, LANES, wb), lambda b, pt: (b, layer, 0, 0, 0)),
            pl.BlockSpec((N_HEADS * DEC_T, LANES), lambda b, pt: (0, 0)),
            pl.BlockSpec((LANES, past), lambda b, pt: (0, 0))],
        out_specs=[pl.BlockSpec((DEC_T, HW), lambda b, pt: (b, 0)), pl.BlockSpec((DEC_T, HW), lambda b, pt: (b, 0))],
    )
    return pl.pallas_call(
        kernel,
        grid_spec=grid_spec,
        out_shape=[jax.ShapeDtypeStruct((nbs * DEC_T, HW), F32), jax.ShapeDtypeStruct((nbs * DEC_T, HW), F32)],
        compiler_params=_params("arbitrary"),
        name="sample_sb_nsa",
    )(pt, *([sb_view] * npg), *([nsa_view] * npg), pf, pf, pf, pf, pf, pf, pf, comp, comp, win_view, t5col, expand)


def _win_state_kernel(*refs):
    win_ref, new_refs, o_ref = refs[0], refs[1:-1], refs[-1]
    wb = win_ref.shape[4]
    lane = _iota((LANES, LANES), 1)
    for l, new_ref in enumerate(new_refs):
        for kv in range(2):
            shifted = pltpu.roll(win_ref[0, l, kv], wb - DEC_T, axis=1)
            new_t = _pad_rows(new_ref[:, kv * LANES:(kv + 1) * LANES], LANES).T
            tail = jnp.where(lane >= LANES - DEC_T, pltpu.roll(new_t, LANES - DEC_T, axis=1), shifted[:, wb - LANES:wb])
            o_ref[0, l, kv] = jnp.concatenate([shifted[:, 0:wb - LANES], tail], axis=1)


def _win_state(win_view, pfs, nbs, row0):
    _, depth, _, _, wb = win_view.shape
    rb = row0 // DEC_T
    blk = pl.BlockSpec((1, depth, 2, LANES, wb), lambda b: (b, 0, 0, 0, 0))
    return pl.pallas_call(
        _win_state_kernel,
        grid=(nbs,),
        in_specs=[blk] + [pl.BlockSpec((DEC_T, 2 * LANES), lambda b: (rb + b, C_WIN // (2 * LANES))) for _ in pfs],
        out_specs=blk,
        out_shape=jax.ShapeDtypeStruct(win_view.shape, F32),
        compiler_params=_params("arbitrary"),
        name="win_state",
    )(win_view, *pfs)


_STATE_SEGS = ((C_KA, 0, 0), (C_VA, 0, HW), (C_KC, 1, 0), (C_VC, 1, HW), (C_CMP, 2, 0))
_STATE_ROWS = (2 * HW, 2 * HW, HW)


def _prompt_states_kernel(*refs):
    nseg = len(_STATE_SEGS)
    depth = (len(refs) - len(_STATE_ROWS)) // nseg
    ins, outs = refs[:nseg * depth], refs[nseg * depth:]
    for l in range(depth):
        for src, (_, dst, row0) in zip(ins[nseg * l:nseg * (l + 1)], _STATE_SEGS):
            for c in range(HW // LANES):
                outs[dst][0, l, row0 + c * LANES:row0 + (c + 1) * LANES, :] = src[:, c * LANES:(c + 1) * LANES].T


def _prompt_states(pfs, nb, t):
    depth = len(pfs)
    tq = _pick(t, 512, LANES)
    nq = t // tq
    in_specs = [pl.BlockSpec((tq, HW), functools.partial(lambda b, i, c0: (b * nq + i, c0 // HW), c0=c0))
                for _ in range(depth) for c0, _, _ in _STATE_SEGS]
    return pl.pallas_call(
        _prompt_states_kernel,
        grid=(nb, nq),
        in_specs=in_specs,
        out_specs=[pl.BlockSpec((1, depth, w, tq), lambda b, i: (b, 0, 0, i)) for w in _STATE_ROWS],
        out_shape=[jax.ShapeDtypeStruct((nb, depth, w, t), F32) for w in _STATE_ROWS],
        compiler_params=_params("arbitrary", "arbitrary"),
        name="prompt_states",
    )(*[pf for pf in pfs for _ in _STATE_SEGS])


def _reorder_w_in(w_in, d):
    sizes = (HW, 2 * HW, N_HEADS, HW, 6 * NSA_G * HEAD_DIM, 3 * N_HEADS, HW, 2 * HW, 3 * d)
    offs = np.concatenate([[0], np.cumsum(sizes)])
    seg = [w_in[:, :, offs[i]:offs[i + 1]] for i in range(len(sizes))]
    q_a, kv_a, f_a, q_b, kv_b, g_b, q_c, kv_c, g_m = seg
    pad = jnp.zeros(w_in.shape[:2] + (MISC_W - N_HEADS - 3 * N_HEADS,), w_in.dtype)
    return jnp.concatenate([q_a, kv_a, q_b, q_c, kv_c, kv_b, f_a, g_b, pad, g_m], axis=-1).astype(BF16)


def kernel(x_prompt, x_sample, cache_fox_kv, cache_fox_logf, cache_nsa_kv, cache_sb_kv, state_nsa_win_kv, page_table, norm_mix_g, norm_ffn_g, norm_final_g, w_in, b_forget, t5_table, cmp_pe, cmp_w1, cmp_w2, w_out_a, w_out_b, w_out_c, w_out, router_group_w, router_group_b, router_expert_w, router_expert_b, expert_w_gate, expert_w_up, expert_w_down):
    nb, t, d = x_prompt.shape
    nbs, dec_t, _ = x_sample.shape
    depth = w_in.shape[0]
    npool = cache_fox_kv.shape[0]
    npg = page_table.shape[1]
    past = npg * PAGE
    wb = state_nsa_win_kv.shape[2]
    assert dec_t == DEC_T and t % KB == 0 and d % 128 == 0 and wb == WINDOW and past >= WINDOW
    n_p, n_s_rows = nb * t, nbs * DEC_T

    fox_view = jnp.transpose(cache_fox_kv, (0, 1, 3, 4, 5, 2)).reshape(npool, depth, 2, HW, PAGE)
    sb_view = jnp.transpose(cache_sb_kv, (0, 1, 3, 4, 5, 2)).reshape(npool, depth, 2, HW, PAGE)
    nsa_view = jnp.transpose(cache_nsa_kv, (0, 1, 3, 4, 5, 6, 2)).reshape(npool, depth, 2, 2, LANES, PAGE)
    logf_view = jnp.transpose(cache_fox_logf, (0, 1, 3, 2))
    win_view = jnp.transpose(state_nsa_win_kv, (0, 1, 3, 4, 5, 2)).reshape(nbs, depth, 2, LANES, wb)
    pt = page_table.reshape(-1).astype(jnp.int32)

    w_in_r = _reorder_w_in(w_in, d)
    bf_pad = jnp.pad(b_forget.astype(F32), ((0, 0), (0, LANES - N_HEADS))).reshape(depth, 1, LANES)
    t5 = jnp.pad(t5_table.astype(F32).T, ((0, 0), (0, LANES - N_BUCKETS)))
    t5col = jnp.repeat(t5, DEC_T, axis=0)
    expand = jnp.asarray(np.arange(LANES)[:, None] == (np.arange(past)[None, :] // SEL_BLOCK), BF16)
    w1r = cmp_w1.reshape(depth, 2, 2, CMP_STRIDE, HEAD_DIM, CMP_HIDDEN)
    w1big = jnp.einsum("zchldk,gG->zclgdhGk", w1r, jnp.eye(NSA_G, dtype=w1r.dtype)).reshape(
        depth, 2, CHUNK_W, 2 * NSA_G * CMP_HIDDEN).astype(BF16)
    pe_rows = jnp.broadcast_to(cmp_pe.reshape(depth, 2, 2, CMP_STRIDE, 1, HEAD_DIM),
                               (depth, 2, 2, CMP_STRIDE, NSA_G, HEAD_DIM)).reshape(depth, 2, 2, CHUNK_W)
    pe_rows = jnp.pad(pe_rows, ((0, 0), (0, 0), (0, 6), (0, 0))).astype(F32)
    w2p = jnp.stack([jnp.pad(cmp_w2, ((0, 0), (0, 0), (0, 0), (g * HEAD_DIM, LANES - (g + 1) * HEAD_DIM)))
                     for g in range(NSA_G)], axis=2).astype(BF16)
    w_router = jnp.pad(jnp.concatenate([router_group_w, router_expert_w], axis=-1),
                       ((0, 0), (0, 0), (0, LANES - N_GROUPS - N_EXPERTS))).astype(F32)
    b_router = jnp.pad(jnp.concatenate([router_group_b, router_expert_b], axis=-1),
                       ((0, 0), (0, LANES - N_GROUPS - N_EXPERTS))).astype(F32).reshape(depth, 1, LANES)
    wa, wb_, wc, wo = (w.astype(BF16) for w in (w_out_a, w_out_b, w_out_c, w_out))
    wg, wu, wd = (w.astype(BF16) for w in (expert_w_gate, expert_w_up, expert_w_down))

    h = jnp.concatenate([x_prompt.reshape(n_p, d), x_sample.reshape(n_s_rows, d)], axis=0)
    st_p = [[] for _ in range(5)]
    st_s = [[] for _ in range(5)]
    pfs = []
    for l in range(depth):
        pf, pb = _proj(h, norm_mix_g[l], w_in_r, l)

        logf_p, cum_p = _cum_prompt(pf, bf_pad[l], nb, t)
        cum_t = jnp.transpose(cum_p[:, :N_HEADS].reshape(nb, t, N_HEADS), (0, 2, 1)).reshape(nb, N_HEADS, t // QB, QB)
        o_a_p = _fox_prompt_t(pb, cum_p, cum_t, nb, t)
        o_c_p = _sb_prompt_t(pb, nb, t)
        comp_p = _compress_prompt(pf, w1big, pe_rows, w2p, l, nb, t)
        o_b_p = _nsa_prompt_t(pb, pf, comp_p, t5, nb, t)

        logf_s, o_a_s, comp_s = _sample_fox(pt, logf_view, fox_view, nsa_view, pf, bf_pad[l], w1big, pe_rows, w2p,
                                            l, nbs, npg, n_p)
        o_c_s, o_b_s = _sample_sb_nsa(pt, sb_view, nsa_view, pf, comp_s, win_view, t5col, expand, l, nbs, npg, n_p)

        o_a = jnp.concatenate([o_a_p, o_a_s.astype(BF16)], axis=0)
        o_b = jnp.concatenate([o_b_p, o_b_s.astype(BF16)], axis=0)
        o_c = jnp.concatenate([o_c_p, o_c_s.astype(BF16)], axis=0)
        mixed = _merge(o_a, o_b, o_c, pf, wa, wb_, wc, d, l)
        h = _mm_res(mixed, wo, h, l)
        xn, comb, mcol, mrow, cnt = _router(h, norm_ffn_g[l], w_router[l], b_router[l])
        counts = cnt[:, 0, :N_GROUPS].astype(jnp.int32).reshape(-1)
        h = _moe(xn, comb, mcol, mrow, counts, h, wg, wu, wd, l)

        def rows(c0, width, shape, lo, hi):
            return pf[lo:hi, c0:c0 + width].reshape(shape)

        win_new_p = rows(C_WIN, 2 * LANES, (nb, t, 2, NSA_G, HEAD_DIM), 0, n_p)
        pfs.append(pf)
        st_p[1].append(logf_p[:, :N_HEADS].reshape(nb, t, N_HEADS))
        st_p[4].append(win_new_p[:, t - min(WINDOW, t):])
        st_s[0].append(rows(C_KA, 2 * HW, (nbs, DEC_T, 2, N_HEADS, HEAD_DIM), n_p, n_p + n_s_rows))
        st_s[1].append(logf_s[:, :N_HEADS].reshape(nbs, DEC_T, N_HEADS))
        st_s[2].append(rows(C_CMP, 4 * LANES, (nbs, DEC_T, 2, 2, NSA_G, HEAD_DIM), n_p, n_p + n_s_rows))
        st_s[3].append(rows(C_KC, 2 * HW, (nbs, DEC_T, 2, N_HEADS, HEAD_DIM), n_p, n_p + n_s_rows))

    y = _final_norm(h, norm_final_g)
    y_prompt = y[:n_p].reshape(nb, t, d)
    y_sample = y[n_p:].reshape(nbs, DEC_T, d)
    ss = [jnp.stack(s, axis=1) for s in st_s[:4]]
    fox_t, sb_t, nsa_t = _prompt_states(pfs, nb, t)
    fox_p = jnp.transpose(fox_t.reshape(nb, depth, 2, N_HEADS, HEAD_DIM, t), (0, 1, 5, 2, 3, 4))
    sb_p = jnp.transpose(sb_t.reshape(nb, depth, 2, N_HEADS, HEAD_DIM, t), (0, 1, 5, 2, 3, 4))
    nsa_p = jnp.transpose(nsa_t.reshape(nb, depth, 2, 2, NSA_G, HEAD_DIM, t), (0, 1, 6, 2, 3, 4, 5))
    logf_p_all = jnp.stack(st_p[1], axis=1)
    win_p = jnp.stack(st_p[4], axis=1)
    win_s = _win_state(win_view, pfs, nbs, n_p).reshape(nbs, depth, 2, NSA_G, HEAD_DIM, wb)
    win_s = jnp.transpose(win_s, (0, 1, 5, 2, 3, 4))
    return (y_prompt, y_sample, fox_p, ss[0], logf_p_all, ss[1], nsa_p, ss[2], sb_p, ss[3], win_p, win_s)
```

```python
import functools
import math

import numpy as np
import jax
import jax.numpy as jnp
from jax import lax
from jax.experimental import pallas as pl
from jax.experimental.pallas import tpu as pltpu

F32 = jnp.float32
BF16 = jnp.bfloat16

HEAD_DIM = 64
N_HEADS = 8
NSA_G = 2
NSA_HG = N_HEADS // NSA_G
HW = N_HEADS * HEAD_DIM
PAGE = 128
DEC_T = 8
CMP_BLOCK = 32
CMP_STRIDE = 16
CMP_HIDDEN = 128
SEL_BLOCK = 64
SEL_TOPK = 8
WINDOW = 512
N_BUCKETS = 32
MAX_DISTANCE = 128
N_GROUPS = 4
EXPERTS_PER_GROUP = 4
N_EXPERTS = N_GROUPS * EXPERTS_PER_GROUP
RMS_EPS = 1e-6
NEG_INF = -1e30
FORCE_SCORE = 1e9
BELOW_ALL = -3e38
SCALE = HEAD_DIM ** -0.5
TINY = float(np.finfo(np.float32).tiny)
QB = 128
KB = 256
MOE_TILE = 768
MOE_CHUNK = 256
LANES = 128
VMEM_LIMIT = 56 * 1024 * 1024

C_QA, C_KA, C_VA, C_QB, C_QC, C_KC, C_VC = 0, 512, 1024, 1536, 2048, 2560, 3072
C_CMP, C_SEL, C_WIN, C_MISC, C_GM = 3584, 3840, 4096, 4352, 4608
MISC_W = 256
CHUNK_W = CMP_STRIDE * LANES


def _params(*sem):
    return pltpu.CompilerParams(dimension_semantics=sem, vmem_limit_bytes=VMEM_LIMIT)


def _pick(n, cap, mult):
    t = (min(cap, n) // mult) * mult
    while t > 0 and n % t:
        t -= mult
    assert t > 0, (n, cap, mult)
    return t


def _iota(shape, axis):
    return lax.broadcasted_iota(jnp.int32, shape, axis)


def _dot(a, b):
    return jnp.dot(a, b, preferred_element_type=F32)


def _dot_nt(a, b):
    return lax.dot_general(a, b, (((1,), (1,)), ((), ())), preferred_element_type=F32)


def _softplus(x):
    return jnp.maximum(x, 0.0) + jnp.log(1.0 + jnp.exp(-jnp.abs(x)))


def _log_sigmoid(x):
    return jnp.minimum(x, 0.0) - jnp.log1p(jnp.exp(-jnp.abs(x)))


def _sigmoid(x):
    return 1.0 / (1.0 + jnp.exp(-x))


def _split_dot(x, w):
    hi = x.astype(BF16)
    lo = (x - hi.astype(F32)).astype(BF16)
    return _dot(hi, w) + _dot(lo, w)


def _tri(w):
    return jnp.where(_iota((w, w), 0) > _iota((w, w), 1), 1.0, 0.0).astype(BF16)


def _bucket(rel):
    n = jnp.maximum(rel, 0)
    exact = N_BUCKETS // 2
    far = jnp.log(jnp.maximum(n, exact).astype(F32) / exact) / math.log(MAX_DISTANCE / exact)
    far = exact + (far * (N_BUCKETS - exact)).astype(jnp.int32)
    return jnp.where(n < exact, n, jnp.minimum(far, N_BUCKETS - 1))


def _masked_softmax(logits, mask):
    l = jnp.where(mask, logits, NEG_INF)
    e = jnp.where(mask, jnp.exp(l - jnp.max(l, axis=-1, keepdims=True)), 0.0)
    return e / jnp.maximum(jnp.sum(e, axis=-1, keepdims=True), TINY)


def _overlap(nch):
    cs = _iota((nch, LANES), 0) * CMP_STRIDE
    ss = _iota((nch, LANES), 1) * SEL_BLOCK
    ov = jnp.maximum(jnp.minimum(cs + CMP_BLOCK, ss + SEL_BLOCK) - jnp.maximum(cs, ss), 0)
    return (ov.astype(F32) / CMP_BLOCK).astype(BF16)


def _select_blocks(score, qpos, n_s):
    lane = _iota(score.shape, 1)
    lane_f = lane.astype(F32)
    valid = lane * SEL_BLOCK <= qpos
    forced = (lane == jnp.right_shift(qpos, 6)) | (lane == 0)
    sc = jnp.where(forced, FORCE_SCORE, jnp.where(valid, score, NEG_INF))
    sc = jnp.where(lane < n_s, sc, BELOW_ALL)
    rank = jnp.zeros(score.shape, F32)
    for k in range(n_s):
        other = sc[:, k:k + 1]
        rank = rank + jnp.where((other > sc) | ((other == sc) & (lane > k)), 1.0, 0.0)
    return jnp.where((rank < min(SEL_TOPK, n_s)) & (lane < n_s), 1.0, 0.0)


def _proj_kernel(x_ref, g_ref, w_ref, of_ref, ob_ref, xn_ref):
    @pl.when(pl.program_id(1) == 0)
    def _():
        x = x_ref[...]
        ms = jnp.mean(x * x, axis=-1, keepdims=True)
        xn_ref[...] = (x * lax.rsqrt(ms + RMS_EPS) * g_ref[...]).astype(BF16)

    y = _dot(xn_ref[...], w_ref[...])
    of_ref[...] = y
    ob_ref[...] = y.astype(BF16)


def _proj(h, g, w, layer):
    n, d = h.shape
    wp = w.shape[2]
    tm = _pick(n, 1024, 16)
    tn = _pick(wp, 768, 128)
    return pl.pallas_call(
        _proj_kernel,
        grid=(n // tm, wp // tn),
        in_specs=[pl.BlockSpec((tm, d), lambda i, j: (i, 0)),
                  pl.BlockSpec((1, d), lambda i, j: (0, 0)),
                  pl.BlockSpec((None, d, tn), lambda i, j: (layer, 0, j))],
        out_specs=[pl.BlockSpec((tm, tn), lambda i, j: (i, j)),
                   pl.BlockSpec((tm, tn), lambda i, j: (i, j))],
        out_shape=[jax.ShapeDtypeStruct((n, wp), F32), jax.ShapeDtypeStruct((n, wp), BF16)],
        scratch_shapes=[pltpu.VMEM((tm, d), BF16)],
        compiler_params=_params("arbitrary", "arbitrary"),
        name="proj",
    )(h, g.reshape(1, d), w)


def _merge_kernel(oa_ref, ob_ref, oc_ref, wa_ref, wb_ref, wc_ref, g0_ref, g1_ref, g2_ref, o_ref):
    m = _sigmoid(g0_ref[...]) * _dot(oa_ref[...], wa_ref[...])
    m = m + _sigmoid(g1_ref[...]) * _dot(ob_ref[...], wb_ref[...])
    m = m + _sigmoid(g2_ref[...]) * _dot(oc_ref[...], wc_ref[...])
    o_ref[...] = m.astype(BF16)


def _merge(o_a, o_b, o_c, pf, wa, wb, wc, d, layer):
    n = o_a.shape[0]
    tm = _pick(n, 1024, 16)
    tn = _pick(d, 512, 128)
    gm0 = C_GM // tn
    o_spec = pl.BlockSpec((tm, HW), lambda i, j: (i, 0))
    w_spec = pl.BlockSpec((None, HW, tn), lambda i, j: (layer, 0, j))

    def g_spec(k):
        return pl.BlockSpec((tm, tn), lambda i, j: (i, gm0 + k * (d // tn) + j))

    return pl.pallas_call(
        _merge_kernel,
        grid=(n // tm, d // tn),
        in_specs=[o_spec, o_spec, o_spec, w_spec, w_spec, w_spec, g_spec(0), g_spec(1), g_spec(2)],
        out_specs=pl.BlockSpec((tm, tn), lambda i, j: (i, j)),
        out_shape=jax.ShapeDtypeStruct((n, d), BF16),
        compiler_params=_params("arbitrary", "arbitrary"),
        name="merge",
    )(o_a, o_b, o_c, wa, wb, wc, pf, pf, pf)


def _mm_res_kernel(x_ref, w_ref, r_ref, o_ref):
    o_ref[...] = r_ref[...] + _dot(x_ref[...], w_ref[...])


def _mm_res(x, w, res, layer):
    n, k = x.shape
    d = w.shape[2]
    tm = _pick(n, 1024, 16)
    tn = _pick(d, 512, 128)
    return pl.pallas_call(
        _mm_res_kernel,
        grid=(n // tm, d // tn),
        in_specs=[pl.BlockSpec((tm, k), lambda i, j: (i, 0)),
                  pl.BlockSpec((None, k, tn), lambda i, j: (layer, 0, j)),
                  pl.BlockSpec((tm, tn), lambda i, j: (i, j))],
        out_specs=pl.BlockSpec((tm, tn), lambda i, j: (i, j)),
        out_shape=jax.ShapeDtypeStruct((n, d), F32),
        compiler_params=_params("arbitrary", "arbitrary"),
        name="out_proj",
    )(x, w, res)


def _router_kernel(h_ref, g_ref, w_ref, b_ref, xn_ref, comb_ref, mcol_ref, mrow_ref, cnt_ref):
    x = h_ref[...]
    ms = jnp.mean(x * x, axis=-1, keepdims=True)
    xn = x * lax.rsqrt(ms + RMS_EPS) * g_ref[...]
    xn_ref[...] = xn.astype(BF16)
    logits = jnp.dot(xn, w_ref[...], precision=lax.Precision.HIGHEST, preferred_element_type=F32) + b_ref[...]
    lane = _iota(logits.shape, 1)
    lane_f = lane.astype(F32)
    is_grp = lane < N_GROUPS
    gl = jnp.where(is_grp, logits, BELOW_ALL)
    gmax = jnp.max(gl, axis=-1, keepdims=True)
    gsum = jnp.sum(jnp.where(is_grp, jnp.exp(gl - gmax), 0.0), axis=-1, keepdims=True)
    w_grp = 1.0 / gsum
    g_star = jnp.min(jnp.where(is_grp & (gl == gmax), lane_f, 1e9), axis=-1, keepdims=True)
    lo = N_GROUPS + g_star * EXPERTS_PER_GROUP
    in_grp = (lane_f >= lo) & (lane_f < lo + EXPERTS_PER_GROUP)
    el = jnp.where(in_grp, logits, BELOW_ALL)
    v1 = jnp.max(el, axis=-1, keepdims=True)
    i1 = jnp.min(jnp.where(el == v1, lane_f, 1e9), axis=-1, keepdims=True)
    el2 = jnp.where(lane_f == i1, BELOW_ALL, el)
    v2 = jnp.max(el2, axis=-1, keepdims=True)
    i2 = jnp.min(jnp.where(el2 == v2, lane_f, 1e9), axis=-1, keepdims=True)
    e2 = jnp.exp(v2 - v1)
    den = 1.0 + e2
    comb = jnp.where(lane_f == i1, w_grp / den, 0.0) + jnp.where(lane_f == i2, w_grp * e2 / den, 0.0)
    comb_ref[...] = comb
    member = jnp.where(lane_f == g_star, 1.0, 0.0)
    tm = member.shape[0]
    row = _iota(member.shape, 0)
    seen = member
    s = 1
    while s < tm:
        seen = seen + jnp.where(row >= s, pltpu.roll(seen, s, axis=0), 0.0)
        s *= 2
    rank = jnp.sum((seen - member) * member, axis=-1, keepdims=True)
    meta = jnp.where(lane == 0, g_star, jnp.where(lane == 1, rank, 0.0))
    mcol_ref[...] = meta
    mrow_ref[0] = meta.T[0:8]
    cnt_ref[0] = jnp.broadcast_to(seen[tm - 1:tm, :], (8, LANES))


def _router(h, g, w, b):
    n, d = h.shape
    tm = _pick(n, MOE_TILE, LANES)
    nt = n // tm
    return pl.pallas_call(
        _router_kernel,
        grid=(nt,),
        in_specs=[pl.BlockSpec((tm, d), lambda i: (i, 0)),
                  pl.BlockSpec((1, d), lambda i: (0, 0)),
                  pl.BlockSpec((d, LANES), lambda i: (0, 0)),
                  pl.BlockSpec((1, LANES), lambda i: (0, 0))],
        out_specs=[pl.BlockSpec((tm, d), lambda i: (i, 0)),
                   pl.BlockSpec((tm, LANES), lambda i: (i, 0)),
                   pl.BlockSpec((tm, LANES), lambda i: (i, 0)),
                   pl.BlockSpec((1, 8, tm), lambda i: (i, 0, 0)),
                   pl.BlockSpec((1, 8, LANES), lambda i: (i, 0, 0))],
        out_shape=[jax.ShapeDtypeStruct((n, d), BF16), jax.ShapeDtypeStruct((n, LANES), F32),
                   jax.ShapeDtypeStruct((n, LANES), F32), jax.ShapeDtypeStruct((nt, 8, tm), F32),
                   jax.ShapeDtypeStruct((nt, 8, LANES), F32)],
        compiler_params=_params("arbitrary"),
        name="router",
    )(h, g.reshape(1, d), w, b)


def _moe_kernel(cnt_ref, x_ref, c_ref, mcol_ref, mrow_ref, h_ref, wg_ref, wu_ref, wd_ref, o_ref, xc_ref, cc_ref, y_ref, *, ch):
    i, e = pl.program_id(0), pl.program_id(1)
    tm = x_ref.shape[0]
    grp = jnp.right_shift(e, 2)
    grp_f = grp.astype(F32)
    nchunk = (cnt_ref[i * N_GROUPS + grp] + ch - 1) // ch

    @pl.when(e == 0)
    def _():
        o_ref[...] = h_ref[...]

    @pl.when(jnp.bitwise_and(e, EXPERTS_PER_GROUP - 1) == 0)
    def _():
        grp_row, rank_row = mrow_ref[0, 0:1, :], mrow_ref[0, 1:2, :]
        comb = c_ref[...]
        c1 = comb.astype(BF16)
        rest = comb - c1.astype(F32)
        c2 = rest.astype(BF16)
        c3 = (rest - c2.astype(F32)).astype(BF16)

        def gather(c, _):
            base = pl.multiple_of(c * ch, ch)
            slot = (base + _iota((ch, tm), 0)).astype(F32)
            pick = jnp.where((grp_row == grp_f) & (rank_row == slot), 1.0, 0.0).astype(BF16)
            xc_ref[pl.ds(base, ch), :] = _dot(pick, x_ref[...]).astype(BF16)
            cc_ref[pl.ds(base, ch), :] = _dot(pick, c1) + _dot(pick, c2) + _dot(pick, c3)
            y_ref[pl.ds(base, ch), :] = jnp.zeros((ch, y_ref.shape[1]), F32)
            return 0

        lax.fori_loop(0, nchunk, gather, 0)

    def expert(c, _):
        base = pl.multiple_of(c * ch, ch)
        xc = xc_ref[pl.ds(base, ch), :]
        cc = cc_ref[pl.ds(base, ch), :]
        ce = jnp.sum(jnp.where(_iota(cc.shape, 1) == e + N_GROUPS, cc, 0.0), axis=-1, keepdims=True)
        gate = _dot(xc, wg_ref[0])
        up = _dot(xc, wu_ref[0])
        hh = gate * _sigmoid(gate) * up * ce
        y_ref[pl.ds(base, ch), :] += _dot(hh.astype(BF16), wd_ref[0])
        return 0

    lax.fori_loop(0, nchunk, expert, 0)

    @pl.when(jnp.bitwise_and(e, EXPERTS_PER_GROUP - 1) == EXPERTS_PER_GROUP - 1)
    def _():
        grp_col, rank_col = mcol_ref[:, 0:1], mcol_ref[:, 1:2]

        def scatter(c, _):
            base = pl.multiple_of(c * ch, ch)
            slot = (base + _iota((tm, ch), 1)).astype(F32)
            place = jnp.where((grp_col == grp_f) & (rank_col == slot), 1.0, 0.0).astype(BF16)
            y = y_ref[pl.ds(base, ch), :]
            hi = y.astype(BF16)
            lo = (y - hi.astype(F32)).astype(BF16)
            o_ref[...] += _dot(place, hi) + _dot(place, lo)
            return 0

        lax.fori_loop(0, nchunk, scatter, 0)


def _moe(xn, comb, mcol, mrow, counts, h, wg, wu, wd, layer):
    n, d = h.shape
    _, ne, _, ff = wg.shape
    nt, _, tm = mrow.shape
    ch = min(MOE_CHUNK, tm)
    cap = -(-tm // ch) * ch
    kernel = functools.partial(_moe_kernel, ch=ch)
    grid_spec = pltpu.PrefetchScalarGridSpec(
        num_scalar_prefetch=1,
        grid=(nt, ne),
        in_specs=[pl.BlockSpec((tm, d), lambda i, e, cnt: (i, 0)),
                  pl.BlockSpec((tm, LANES), lambda i, e, cnt: (i, 0)),
                  pl.BlockSpec((tm, LANES), lambda i, e, cnt: (i, 0)),
                  pl.BlockSpec((1, 8, tm), lambda i, e, cnt: (i, 0, 0)),
                  pl.BlockSpec((tm, d), lambda i, e, cnt: (i, 0), pipeline_mode=pl.Buffered(1)),
                  pl.BlockSpec((None, 1, d, ff), lambda i, e, cnt: (layer, e, 0, 0)),
                  pl.BlockSpec((None, 1, d, ff), lambda i, e, cnt: (layer, e, 0, 0)),
                  pl.BlockSpec((None, 1, ff, d), lambda i, e, cnt: (layer, e, 0, 0))],
        out_specs=pl.BlockSpec((tm, d), lambda i, e, cnt: (i, 0), pipeline_mode=pl.Buffered(1)),
        scratch_shapes=[pltpu.VMEM((cap, d), BF16), pltpu.VMEM((cap, LANES), F32), pltpu.VMEM((cap, d), F32)],
    )
    return pl.pallas_call(
        kernel,
        grid_spec=grid_spec,
        out_shape=jax.ShapeDtypeStruct((n, d), F32),
        compiler_params=_params("arbitrary", "arbitrary"),
        name="moe",
    )(counts, xn, comb, mcol, mrow, h, wg, wu, wd)


def _final_norm_kernel(h_ref, g_ref, o_ref):
    x = h_ref[...]
    ms = jnp.mean(x * x, axis=-1, keepdims=True)
    o_ref[...] = x * lax.rsqrt(ms + RMS_EPS) * g_ref[...]


def _final_norm(h, g, row0, rows):
    d = h.shape[1]
    tm = math.gcd(_pick(rows, 512, 8), row0) if row0 else _pick(rows, 512, 8)
    return pl.pallas_call(
        _final_norm_kernel,
        grid=(rows // tm,),
        in_specs=[pl.BlockSpec((tm, d), lambda i: (row0 // tm + i, 0)), pl.BlockSpec((1, d), lambda i: (0, 0))],
        out_specs=pl.BlockSpec((tm, d), lambda i: (i, 0)),
        out_shape=jax.ShapeDtypeStruct((rows, d), F32),
        compiler_params=_params("arbitrary"),
        name="final_norm",
    )(h, g.reshape(1, d))


def _cum_prompt_kernel(m_ref, bf_ref, logf_ref, cum_ref):
    logf = _log_sigmoid(m_ref[:, 0:LANES] + bf_ref[...])
    logf_ref[...] = logf
    t = logf.shape[0]
    row = _iota(logf.shape, 0)
    c = logf
    s = 1
    while s < t:
        c = c + jnp.where(row >= s, pltpu.roll(c, s, axis=0), 0.0)
        s *= 2
    cum_ref[...] = c


def _cum_prompt(pf, bf, nb, t):
    return pl.pallas_call(
        _cum_prompt_kernel,
        grid=(nb,),
        in_specs=[pl.BlockSpec((t, MISC_W), lambda b: (b, C_MISC // MISC_W)),
                  pl.BlockSpec((1, LANES), lambda b: (0, 0))],
        out_specs=[pl.BlockSpec((t, LANES), lambda b: (b, 0)), pl.BlockSpec((t, LANES), lambda b: (b, 0))],
        out_shape=[jax.ShapeDtypeStruct((nb * t, LANES), F32), jax.ShapeDtypeStruct((nb * t, LANES), F32)],
        compiler_params=_params("arbitrary"),
        name="cum_prompt",
    )(pf, bf)


def _stack_pair(q_ref, p, lane):
    qp = q_ref[:, p * LANES:(p + 1) * LANES] * jnp.asarray(SCALE, BF16)
    zero = jnp.zeros_like(qp)
    return jnp.concatenate([jnp.where(lane < HEAD_DIM, qp, zero), jnp.where(lane >= HEAD_DIM, qp, zero)], axis=0)


def _compress_chunks(ch_ref, w1_ref, pe_ref, w2_ref, o_ref):
    nch = ch_ref.shape[1]
    hw = NSA_G * CMP_HIDDEN
    for c in range(2):
        w1 = w1_ref[0, c]
        pe = _dot(pe_ref[0, c].astype(BF16), w1)
        y = _dot(ch_ref[c].astype(BF16), w1)
        out = None
        for g in range(NSA_G):
            lo = slice(g * CMP_HIDDEN, (g + 1) * CMP_HIDDEN)
            hi = slice(hw + g * CMP_HIDDEN, hw + (g + 1) * CMP_HIDDEN)
            pre = y[:, lo] + pltpu.roll(y[:, hi], nch - 1, axis=0) + pe[0:1, lo] + pe[1:2, hi]
            hid = 0.5 * pre * (1.0 + jnp.tanh(math.sqrt(2.0 / math.pi) * (pre + 0.044715 * pre * pre * pre)))
            cg = _dot(hid.astype(BF16), w2_ref[0, c, g])
            out = cg if out is None else out + cg
        o_ref[c, 0] = out.astype(BF16)


def _compress_weight_specs(layer, imap):
    return [pl.BlockSpec((1, 2, CHUNK_W, 2 * NSA_G * CMP_HIDDEN), imap((layer, 0, 0, 0))),
            pl.BlockSpec((1, 2, 8, CHUNK_W), imap((layer, 0, 0, 0))),
            pl.BlockSpec((1, 2, NSA_G, CMP_HIDDEN, LANES), imap((layer, 0, 0, 0, 0)))]


def _compress_prompt_kernel(xk_ref, xv_ref, w1_ref, pe_ref, w2_ref, o_ref, ch_ref):
    nch = ch_ref.shape[1]
    for c, x_ref in enumerate((xk_ref, xv_ref)):
        for r in range(CMP_STRIDE):
            ch_ref[c, :, r * LANES:(r + 1) * LANES] = x_ref[pl.ds(r, nch, stride=CMP_STRIDE), :]
    _compress_chunks(ch_ref, w1_ref, pe_ref, w2_ref, o_ref)


def _compress_prompt(pf, w1big, pe_rows, w2p, layer, nb, t):
    nch = t // CMP_STRIDE
    return pl.pallas_call(
        _compress_prompt_kernel,
        grid=(nb,),
        in_specs=[pl.BlockSpec((t, LANES), lambda b: (b, C_CMP // LANES)),
                  pl.BlockSpec((t, LANES), lambda b: (b, C_CMP // LANES + 1))]
        + _compress_weight_specs(layer, lambda idx: (lambda b: idx)),
        out_specs=pl.BlockSpec((2, 1, nch, LANES), lambda b: (0, b, 0, 0)),
        out_shape=jax.ShapeDtypeStruct((2, nb, nch, LANES), BF16),
        scratch_shapes=[pltpu.VMEM((2, nch, CHUNK_W), F32)],
        compiler_params=_params("arbitrary"),
        name="nsa_compress_prompt",
    )(pf, pf, w1big, pe_rows, w2p)


def _lut(tab, bk):
    parts = [jnp.take_along_axis(tab, bk[:, c:c + LANES], axis=1) for c in range(0, bk.shape[1], LANES)]
    return parts[0] if len(parts) == 1 else jnp.concatenate(parts, axis=1)


def _transpose_bf16(x):
    return x.astype(F32).T.astype(BF16)


def _pair_out(o_t, lane_dtype=BF16):
    own = jnp.concatenate([o_t[0:HEAD_DIM, 0:QB], o_t[HEAD_DIM:LANES, QB:2 * QB]], axis=0)
    return own.T.astype(lane_dtype)


def _fox_t_kernel(q_ref, k_ref, v_ref, cum_ref, cq_ref, o_ref, vt_ref, ckb_ref):
    i = pl.program_id(1)
    npair = N_HEADS // 2
    nkb = vt_ref.shape[0]

    @pl.when(i == 0)
    def _():
        for kb in range(nkb):
            for p in range(npair):
                vt_ref[kb, p * LANES:(p + 1) * LANES, :] = _transpose_bf16(v_ref[kb * KB:(kb + 1) * KB, p * LANES:(p + 1) * LANES])
        for h in range(N_HEADS):
            ckb_ref[h] = jnp.broadcast_to(cum_ref[:, h:h + 1], ckb_ref.shape[1:])

    lane = _iota((QB, LANES), 1)
    qrow = i * QB + jnp.bitwise_and(_iota((1, 2 * QB), 1), QB - 1)
    krow = _iota((KB, 1), 0)
    q2 = [_stack_pair(q_ref, p, lane) for p in range(npair)]
    cq2 = [jnp.concatenate([cq_ref[0, 2 * p, pl.ds(i, 1), :], cq_ref[0, 2 * p + 1, pl.ds(i, 1), :]], axis=1)
           for p in range(npair)]

    def step(kb, carry, masked):
        k0 = pl.multiple_of(kb * KB, KB)
        scores = [_dot_nt(k_ref[pl.ds(k0, KB), p * LANES:(p + 1) * LANES], q2[p]) for p in range(npair)]
        probs, stats = [], []
        for p in range(npair):
            m, l, _ = carry[p]
            ck = jnp.concatenate([ckb_ref[2 * p, pl.ds(k0, KB), :], ckb_ref[2 * p + 1, pl.ds(k0, KB), :]], axis=1)
            s = scores[p] + cq2[p] - ck
            if masked:
                s = jnp.where((k0 + krow) <= qrow, s, NEG_INF)
            m_new = jnp.maximum(m, jnp.max(s, axis=0, keepdims=True))
            pr = jnp.exp(s - m_new)
            alpha = jnp.exp(m - m_new)
            probs.append(pr.astype(BF16))
            stats.append((m_new, alpha * l + jnp.sum(pr, axis=0, keepdims=True), alpha))
        out = []
        for p in range(npair):
            m_new, l, alpha = stats[p]
            acc = alpha * carry[p][2] + _dot(vt_ref[kb, p * LANES:(p + 1) * LANES, :], probs[p])
            out.append((m_new, l, acc))
        return tuple(out)

    init = tuple((jnp.full((1, 2 * QB), NEG_INF, F32), jnp.zeros((1, 2 * QB), F32), jnp.zeros((LANES, 2 * QB), F32))
                 for _ in range(npair))
    last = (i * QB) // KB
    carry = lax.fori_loop(0, last, lambda kb, c: step(kb, c, False), init)
    carry = step(last, carry, True)
    for p in range(npair):
        m, l, acc = carry[p]
        o_ref[:, p * LANES:(p + 1) * LANES] = _pair_out(acc / jnp.maximum(l, TINY))


def _fox_prompt_t(pb, cum, cum_t, nb, t):
    nq = t // QB
    return pl.pallas_call(
        _fox_t_kernel,
        grid=(nb, nq),
        in_specs=[pl.BlockSpec((QB, HW), lambda b, i: (b * nq + i, C_QA // HW)),
                  pl.BlockSpec((t, HW), lambda b, i: (b, C_KA // HW)),
                  pl.BlockSpec((t, HW), lambda b, i: (b, C_VA // HW)),
                  pl.BlockSpec((t, LANES), lambda b, i: (b, 0)),
                  pl.BlockSpec((1, N_HEADS, nq, QB), lambda b, i: (b, 0, 0, 0))],
        out_specs=pl.BlockSpec((QB, HW), lambda b, i: (b * nq + i, 0)),
        out_shape=jax.ShapeDtypeStruct((nb * t, HW), BF16),
        scratch_shapes=[pltpu.VMEM((t // KB, HW, KB), BF16), pltpu.VMEM((N_HEADS, t, LANES), F32)],
        compiler_params=_params("arbitrary", "arbitrary"),
        name="fox_prompt",
    )(pb, pb, pb, cum, cum_t)


def _sb_t_kernel(q_ref, k_ref, v_ref, o_ref, vt_ref):
    i = pl.program_id(1)
    npair = N_HEADS // 2
    nkb = vt_ref.shape[0]

    @pl.when(i == 0)
    def _():
        for kb in range(nkb):
            for p in range(npair):
                vt_ref[kb, p * LANES:(p + 1) * LANES, :] = _transpose_bf16(v_ref[kb * KB:(kb + 1) * KB, p * LANES:(p + 1) * LANES])

    lane = _iota((QB, LANES), 1)
    qrow = i * QB + jnp.bitwise_and(_iota((1, 2 * QB), 1), QB - 1)
    krow = _iota((KB, 1), 0)
    later = jnp.where(_iota((KB, KB), 1) > _iota((KB, KB), 0), 1.0, 0.0).astype(BF16)
    q2 = [_stack_pair(q_ref, p, lane) for p in range(npair)]

    def step(kb, carry, masked):
        k0 = pl.multiple_of(kb * KB, KB)
        vis = (k0 + krow) < qrow
        zs = [_dot_nt(k_ref[pl.ds(k0, KB), p * LANES:(p + 1) * LANES], q2[p]) for p in range(npair)]
        sps = [_softplus(z) for z in zs]
        lks = [jnp.where(vis, -sp, 0.0) if masked else -sp for sp in sps]
        his = [lk.astype(BF16) for lk in lks]
        los = [(lk - hi.astype(F32)).astype(BF16) for lk, hi in zip(lks, his)]
        betweens = [_dot(later, hi) + _dot(later, lo) for hi, lo in zip(his, los)]
        weights = []
        for p in range(npair):
            a = jnp.exp(zs[p] - sps[p] + betweens[p] + carry[p][0])
            weights.append((jnp.where(vis, a, 0.0) if masked else a).astype(BF16))
        out = []
        for p in range(npair):
            r, acc = carry[p]
            acc = acc + _dot(vt_ref[kb, p * LANES:(p + 1) * LANES, :], weights[p])
            out.append((r + jnp.sum(lks[p], axis=0, keepdims=True), acc))
        return tuple(out)

    init = tuple((jnp.zeros((1, 2 * QB), F32), jnp.zeros((LANES, 2 * QB), F32)) for _ in range(npair))
    last = (i * QB) // KB
    carry = step(last, init, True)
    carry = lax.fori_loop(0, last, lambda n, c: step(last - 1 - n, c, False), carry)
    for p in range(npair):
        o_ref[:, p * LANES:(p + 1) * LANES] = _pair_out(carry[p][1])


def _sb_prompt_t(pb, nb, t):
    nq = t // QB
    return pl.pallas_call(
        _sb_t_kernel,
        grid=(nb, nq),
        in_specs=[pl.BlockSpec((QB, HW), lambda b, i: (b * nq + i, C_QC // HW)),
                  pl.BlockSpec((t, HW), lambda b, i: (b, C_KC // HW)),
                  pl.BlockSpec((t, HW), lambda b, i: (b, C_VC // HW))],
        out_specs=pl.BlockSpec((QB, HW), lambda b, i: (b * nq + i, 0)),
        out_shape=jax.ShapeDtypeStruct((nb * t, HW), BF16),
        scratch_shapes=[pltpu.VMEM((t // KB, HW, KB), BF16)],
        compiler_params=_params("arbitrary", "arbitrary"),
        name="sb_prompt",
    )(pb, pb, pb)


def _select_blocks_t(score, qrow, n_s, sc_ref):
    blk = _iota(score.shape, 0)
    blk_f = blk.astype(F32)
    valid = blk * SEL_BLOCK <= qrow
    forced = (blk == jnp.right_shift(qrow, 6)) | (blk == 0)
    sc = jnp.where(forced, FORCE_SCORE, jnp.where(valid, score, NEG_INF))
    sc = jnp.where(blk < n_s, sc, BELOW_ALL)
    rows = min(score.shape[0], -(-n_s // 8) * 8)
    sc, blk = sc[0:rows], _iota((rows, score.shape[1]), 0)
    sc_ref[0:rows, :] = sc
    rank = jnp.zeros(sc.shape, F32)
    for k in range(n_s):
        other = sc_ref[k:k + 1, :]
        rank = rank + jnp.where((other > sc) | ((other == sc) & (blk > k)), 1.0, 0.0)
    sel = jnp.where((rank < min(SEL_TOPK, n_s)) & (blk < n_s), 1.0, 0.0)
    if rows < score.shape[0]:
        sel = jnp.concatenate([sel, jnp.zeros((score.shape[0] - rows, score.shape[1]), F32)], axis=0)
    return sel


def _nsa_t_kernel(t5_ref, q_ref, ck_ref, cv_ref, sel_ref, win_ref, misc_ref, o_ref, svt_ref, wvt_ref, sc_ref, *, n_s):
    i = pl.program_id(1)
    q0 = i * QB
    nch = ck_ref.shape[2]
    nblk = svt_ref.shape[0]
    cols = N_HEADS * QB

    @pl.when(i == 0)
    def _():
        for kb in range(nblk):
            svt_ref[kb] = _transpose_bf16(sel_ref[kb * QB:(kb + 1) * QB, LANES:2 * LANES])
            wvt_ref[kb] = _transpose_bf16(win_ref[kb * QB:(kb + 1) * QB, LANES:2 * LANES])

    lane = _iota((QB, LANES), 1)
    qrow = q0 + _iota((1, QB), 1)

    def per_head(f):
        return jnp.concatenate([f(h) for h in range(N_HEADS)], axis=1)

    def per_group(x0, x1):
        return jnp.concatenate([x0] * NSA_HG + [x1] * NSA_HG, axis=1)

    def bias_of(rel):
        bk = _bucket(rel)
        return per_head(lambda h: _lut(jnp.broadcast_to(t5_ref[h:h + 1, :], (rel.shape[0], LANES)), bk))

    def q_head(head):
        g = head // NSA_HG
        x = q_ref[:, (head // 2) * LANES:(head // 2 + 1) * LANES].astype(F32) * SCALE
        if head % 2 != g:
            x = pltpu.roll(x, HEAD_DIM, axis=1)
        return jnp.where((lane >= HEAD_DIM) == (g == 1), x, 0.0).astype(BF16)

    q8 = jnp.concatenate([q_head(h) for h in range(N_HEADS)], axis=0)
    rel_d = _iota((QB, QB), 1) - _iota((QB, QB), 0)
    bias_d = bias_of(rel_d)
    bias_p = bias_of(rel_d + QB)
    bias_far = per_head(lambda h: jnp.broadcast_to(t5_ref[h:h + 1, N_BUCKETS - 1:N_BUCKETS], (1, QB)))

    nrow = _iota((nch, QB), 0)
    rel_c = qrow - (nrow * CMP_STRIDE + (CMP_BLOCK - 1))
    vis_c = jnp.where((rel_c >= 0) & (nrow < nch - 1), 1.0, 0.0)
    mask_c = per_group(vis_c, vis_c) > 0.5
    lc = jnp.where(mask_c, _dot_nt(ck_ref[0, 0], q8) + bias_of(rel_c), NEG_INF)
    ec = jnp.where(mask_c, jnp.exp(lc - jnp.max(lc, axis=0, keepdims=True)), 0.0)
    pc = ec / jnp.maximum(jnp.sum(ec, axis=0, keepdims=True), TINY)
    o_cmp = _dot(_transpose_bf16(cv_ref[0, 0]), pc.astype(BF16))
    psums = []
    for g in range(NSA_G):
        ps = pc[:, g * NSA_HG * QB:(g * NSA_HG + 1) * QB]
        for hg in range(1, NSA_HG):
            ps = ps + pc[:, (g * NSA_HG + hg) * QB:(g * NSA_HG + hg + 1) * QB]
        psums.append(ps)
    psum = jnp.concatenate(psums, axis=1)
    cs = _iota((LANES, nch), 1) * CMP_STRIDE
    ss = _iota((LANES, nch), 0) * SEL_BLOCK
    ov_t = (jnp.maximum(jnp.minimum(cs + CMP_BLOCK, ss + SEL_BLOCK) - jnp.maximum(cs, ss), 0).astype(F32)
            / CMP_BLOCK).astype(BF16)
    p_hi = psum.astype(BF16)
    p_lo = (psum - p_hi.astype(F32)).astype(BF16)
    selm = _select_blocks_t(_dot(ov_t, p_hi) + _dot(ov_t, p_lo), jnp.concatenate([qrow] * NSA_G, axis=1),
                            n_s, sc_ref).astype(BF16)

    def attend(steps):
        loaded = []
        for k_ref, vt_ref, k0, width, _, _, _, real in steps:
            k0 = pl.multiple_of(k0, QB)
            kk = k_ref[pl.ds(k0, width), 0:LANES]
            kb = jnp.right_shift(k0, 7)
            vt = vt_ref[kb] if width == QB else jnp.concatenate([vt_ref[kb], vt_ref[kb + 1]], axis=1)
            if real is not None:
                kk = jnp.where(real, kk, jnp.zeros_like(kk))
                vt = jnp.where(real, vt, jnp.zeros_like(vt))
            loaded.append((kk, vt))
        scores = [_dot_nt(kk, q8) for kk, _ in loaded]
        soft = []
        for (_, _, _, _, hide, bias, (m, l, _), _), s in zip(steps, scores):
            s = s + (bias + hide)
            m_new = jnp.maximum(m, jnp.max(s, axis=0, keepdims=True))
            pr = jnp.exp(s - m_new)
            alpha = jnp.exp(m - m_new)
            soft.append((m_new, alpha * l + jnp.sum(pr, axis=0, keepdims=True), alpha, pr.astype(BF16)))
        return [(m_new, l, alpha * step[6][2] + _dot(vt, pr))
                for step, (_, vt), (m_new, l, alpha, pr) in zip(steps, loaded, soft)]

    def init():
        return (jnp.full((1, cols), NEG_INF, F32), jnp.zeros((1, cols), F32), jnp.zeros((LANES, cols), F32))

    def win_step(dlt, carry):
        rel = qrow - ((i - dlt) * QB + _iota((QB, 1), 0))
        hide = jnp.where((rel >= 0) & (rel < WINDOW), 0.0, NEG_INF)
        bias = bias_d if dlt == 0 else (bias_p if dlt == 1 else bias_far)
        return (win_ref, wvt_ref, jnp.maximum(i - dlt, 0) * QB, QB, per_group(hide, hide), bias, carry,
                None if dlt == 0 else i - dlt >= 0)

    def sel_step(k0, width, bias, carry, causal=False, valid=None):
        blk = jnp.right_shift(k0 + _iota((width, LANES), 0), 6)
        expand = jnp.where(_iota((width, LANES), 1) == blk, 1.0, 0.0).astype(BF16)
        picked = _dot(expand, selm)
        if causal:
            picked = picked * jnp.concatenate([jnp.where((k0 + _iota((width, 1), 0)) <= qrow, 1.0, 0.0)] * NSA_G, axis=1)
        if valid is not None:
            picked = picked * jnp.where(valid, 1.0, 0.0)
        hide = jnp.where(picked > 0.5, 0.0, NEG_INF)
        return (sel_ref, svt_ref, k0, width, per_group(hide[:, 0:QB], hide[:, QB:2 * QB]), bias, carry, None)

    k_prev = jnp.maximum(q0 - QB, 0)
    k_odd = jnp.maximum(q0 - 2 * QB, 0)
    c_win, c_sel = attend([win_step(0, init()), sel_step(q0, QB, bias_d, init(), causal=True)])
    c_win, c_sel = attend([win_step(1, c_win), sel_step(k_prev, QB, bias_p, c_sel, valid=i >= 1)])
    c_win, c_sel = attend([win_step(2, c_win),
                           sel_step(k_odd, QB, bias_far, c_sel, valid=(i >= 2) & (jnp.bitwise_and(i, 1) == 0))])
    for dlt in range(3, WINDOW // QB + 1):
        (c_win,) = attend([win_step(dlt, c_win)])
    o_win = c_win[2] / jnp.maximum(c_win[1], TINY)
    c_sel = lax.fori_loop(0, jnp.right_shift(jnp.maximum(i - 1, 0), 1),
                          lambda kb, c: attend([sel_step(kb * KB, KB, bias_far, c)])[0], c_sel)
    o_sel = c_sel[2] / jnp.maximum(c_sel[1], TINY)

    sig_t = _sigmoid(misc_ref[:, 0:LANES]).T

    def gate(br):
        return per_head(lambda h: sig_t[8 + br * N_HEADS + h:9 + br * N_HEADS + h, :])

    o8 = gate(0) * o_cmp + gate(1) * o_sel + gate(2) * o_win
    for p in range(N_HEADS // 2):
        own = []
        for head in (2 * p, 2 * p + 1):
            g = head // NSA_HG
            own.append(o8[g * HEAD_DIM:(g + 1) * HEAD_DIM, head * QB:(head + 1) * QB])
        o_ref[:, p * LANES:(p + 1) * LANES] = jnp.concatenate(own, axis=0).T.astype(BF16)


def _nsa_prompt_t(pb, pf, comp, t5, nb, t):
    nq = t // QB
    nch = comp.shape[2]
    kernel = functools.partial(_nsa_t_kernel, n_s=-(-t // SEL_BLOCK))
    return pl.pallas_call(
        kernel,
        grid=(nb, nq),
        in_specs=[pl.BlockSpec((N_HEADS, LANES), lambda b, i: (0, 0)),
                  pl.BlockSpec((QB, HW), lambda b, i: (b * nq + i, C_QB // HW)),
                  pl.BlockSpec((1, 1, nch, LANES), lambda b, i: (0, b, 0, 0)),
                  pl.BlockSpec((1, 1, nch, LANES), lambda b, i: (1, b, 0, 0)),
                  pl.BlockSpec((t, 2 * LANES), lambda b, i: (b, C_SEL // (2 * LANES))),
                  pl.BlockSpec((t, 2 * LANES), lambda b, i: (b, C_WIN // (2 * LANES))),
                  pl.BlockSpec((QB, MISC_W), lambda b, i: (b * nq + i, C_MISC // MISC_W))],
        out_specs=pl.BlockSpec((QB, HW), lambda b, i: (b * nq + i, 0)),
        out_shape=jax.ShapeDtypeStruct((nb * t, HW), BF16),
        scratch_shapes=[pltpu.VMEM((nq, LANES, QB), BF16), pltpu.VMEM((nq, LANES, QB), BF16),
                        pltpu.VMEM((LANES, NSA_G * QB), F32)],
        compiler_params=_params("arbitrary", "arbitrary"),
        name="nsa_prompt",
    )(t5, pb, comp, comp, pb, pb, pf)


def _page_specs(npg, block, layer, tail):
    def spec(j):
        return pl.BlockSpec(block, lambda b, pt: (pt[b * npg + j], layer) + tail)
    return [spec(j) for j in range(npg)]


def _cum_sample_body(pages, m_ref, bf_ref, logf_ref):
    npg = len(pages)
    x = jnp.concatenate([pg[0, 0] for pg in pages], axis=0)
    lane = _iota(x.shape, 1)
    s = 1
    while s < PAGE:
        x = x + jnp.where(lane >= s, pltpu.roll(x, s, axis=1), 0.0)
        s *= 2
    off = jnp.zeros((N_HEADS, 1), F32)
    cum_pages = []
    for j in range(npg):
        blk = x[j * N_HEADS:(j + 1) * N_HEADS]
        cum_pages.append(blk + off)
        off = off + blk[:, PAGE - 1:PAGE]
    eye = _iota((N_HEADS, LANES), 0) == _iota((N_HEADS, LANES), 1)
    tot = jnp.sum(jnp.where(eye, off, 0.0), axis=0, keepdims=True)
    logf = _log_sigmoid(m_ref[:, 0:LANES] + bf_ref[...])
    logf_ref[...] = logf
    row = _iota(logf.shape, 0)
    c = logf
    s = 1
    while s < DEC_T:
        c = c + jnp.where(row >= s, pltpu.roll(c, s, axis=0), 0.0)
        s *= 2
    return cum_pages, c + tot


def _block_diag_q(q):
    qt = jnp.concatenate([q] * N_HEADS, axis=0)
    same = jnp.right_shift(_iota(qt.shape, 1), 6) == jnp.right_shift(_iota(qt.shape, 0), 3)
    return jnp.where(same, qt * SCALE, 0.0).astype(BF16)


def _rows_per_head(x):
    return jnp.concatenate([jnp.broadcast_to(x[h:h + 1], (DEC_T, x.shape[1])) for h in range(N_HEADS)], axis=0)


def _col_per_head(x):
    return jnp.concatenate([x[:, h:h + 1] for h in range(N_HEADS)], axis=0)


def _own_head_lanes(acc):
    lane_h = jnp.right_shift(_iota((DEC_T, HW), 1), 6)
    out = jnp.zeros((DEC_T, HW), F32)
    for h in range(N_HEADS):
        out = jnp.where(lane_h == h, acc[h * DEC_T:(h + 1) * DEC_T], out)
    return out


def _pad_rows(x, n):
    return jnp.concatenate([x, jnp.zeros((n - x.shape[0], x.shape[1]), x.dtype)], axis=0)


def _new_key_mask(strict):
    shape = (N_HEADS * DEC_T, PAGE)
    t_row = jnp.bitwise_and(_iota(shape, 0), DEC_T - 1)
    return (_iota(shape, 1) < t_row) if strict else (_iota(shape, 1) <= t_row)


def _fox_sample_body(pages, q_ref, k_ref, v_ref, cum_pages, cum_new, o_ref):
    npg = len(pages)
    qbd = _block_diag_q(q_ref[...])
    cq = _col_per_head(cum_new)
    cum_new_t = _pad_rows(cum_new, LANES).T[0:N_HEADS]
    raw = [_dot(qbd, pages[j][0, 0, 0].astype(BF16)) for j in range(npg)]
    s_pages = [raw[j] + cq - _rows_per_head(cum_pages[j]) for j in range(npg)]
    k_new = _pad_rows(k_ref[...], PAGE).astype(BF16)
    s_new = _dot_nt(qbd, k_new) + cq - _rows_per_head(cum_new_t)
    mask_new = _new_key_mask(strict=False)
    s_new = jnp.where(mask_new, s_new, NEG_INF)
    m = jnp.max(s_new, axis=-1, keepdims=True)
    for s in s_pages:
        m = jnp.maximum(m, jnp.max(s, axis=-1, keepdims=True))
    p_new = jnp.where(mask_new, jnp.exp(s_new - m), 0.0)
    l = jnp.sum(p_new, axis=-1, keepdims=True)
    acc = _dot(p_new.astype(BF16), _pad_rows(v_ref[...], PAGE).astype(BF16))
    for j in range(npg):
        pr = jnp.exp(s_pages[j] - m)
        l = l + jnp.sum(pr, axis=-1, keepdims=True)
        acc = acc + _dot_nt(pr.astype(BF16), pages[j][0, 0, 1].astype(BF16))
    o_ref[...] = _own_head_lanes(acc / jnp.maximum(l, TINY))


def _sb_sample_body(pages, q_ref, k_ref, v_ref, o_ref):
    npg = len(pages)
    qbd = _block_diag_q(q_ref[...])
    tri = _tri(PAGE)
    z = _dot_nt(qbd, _pad_rows(k_ref[...], PAGE).astype(BF16))
    mask = _new_key_mask(strict=True)
    sp = _softplus(z)
    lk = jnp.where(mask, -sp, 0.0)
    between = _split_dot(lk, tri)
    a = jnp.where(mask, jnp.exp(z - sp + between), 0.0)
    acc = _dot(a.astype(BF16), _pad_rows(v_ref[...], PAGE).astype(BF16))
    r = jnp.sum(lk, axis=-1, keepdims=True)
    rows = N_HEADS * DEC_T
    zs = [_dot(qbd, pages[j][0, 0, 0].astype(BF16)) for j in range(npg)]
    sps = [_softplus(z) for z in zs]
    within = _split_dot(jnp.concatenate([-sp for sp in sps], axis=0), tri)
    for j in reversed(range(npg)):
        a = jnp.exp(zs[j] - sps[j] + within[j * rows:(j + 1) * rows] + r)
        acc = acc + _dot_nt(a.astype(BF16), pages[j][0, 0, 1].astype(BF16))
        r = r - jnp.sum(sps[j], axis=-1, keepdims=True)
    o_ref[...] = _own_head_lanes(acc)


def _compress_sample_body(pages, w1_ref, pe_ref, w2_ref, o_ref, ch_ref, xp_ref):
    npg = len(pages)
    per_page = PAGE // CMP_STRIDE
    for j in range(npg):
        for c in range(2):
            xp_ref[2 * j + c] = pages[j][0, 0, 0, c].T
    for j in range(npg):
        for c in range(2):
            for r in range(CMP_STRIDE):
                ch_ref[c, j * per_page:(j + 1) * per_page, r * LANES:(r + 1) * LANES] = (
                    xp_ref[2 * j + c, pl.ds(r, per_page, stride=CMP_STRIDE), :])
    _compress_chunks(ch_ref, w1_ref, pe_ref, w2_ref, o_ref)


def _sample_fox_kernel(pt_ref, *refs, npg):
    logf_pages, fox_pages, cmp_pages = refs[0:npg], refs[npg:2 * npg], refs[2 * npg:3 * npg]
    (m_ref, bf_ref, q_ref, k_ref, v_ref, w1_ref, pe_ref, w2_ref,
     logf_ref, o_ref, comp_ref, ch_ref, xp_ref) = refs[3 * npg:]
    cum_pages, cum_new = _cum_sample_body(logf_pages, m_ref, bf_ref, logf_ref)
    _fox_sample_body(fox_pages, q_ref, k_ref, v_ref, cum_pages, cum_new, o_ref)
    _compress_sample_body(cmp_pages, w1_ref, pe_ref, w2_ref, comp_ref, ch_ref, xp_ref)


def _sample_fox(pt, logf_view, fox_view, nsa_view, pf, bf, w1big, pe_rows, w2p, layer, nbs, npg, row0):
    kernel = functools.partial(_sample_fox_kernel, npg=npg)
    nch = npg * PAGE // CMP_STRIDE
    rb = row0 // DEC_T
    grid_spec = pltpu.PrefetchScalarGridSpec(
        num_scalar_prefetch=1,
        grid=(nbs,),
        in_specs=_page_specs(npg, (1, 1, N_HEADS, PAGE), layer, (0, 0))
        + _page_specs(npg, (1, 1, 2, HW, PAGE), layer, (0, 0, 0))
        + _page_specs(npg, (1, 1, 1, 2, LANES, PAGE), layer, (0, 0, 0, 0))
        + [pl.BlockSpec((DEC_T, MISC_W), lambda b, pt: (rb + b, C_MISC // MISC_W)),
           pl.BlockSpec((1, LANES), lambda b, pt: (0, 0)),
           pl.BlockSpec((DEC_T, HW), lambda b, pt: (rb + b, C_QA // HW)),
           pl.BlockSpec((DEC_T, HW), lambda b, pt: (rb + b, C_KA // HW)),
           pl.BlockSpec((DEC_T, HW), lambda b, pt: (rb + b, C_VA // HW))]
        + _compress_weight_specs(layer, lambda idx: (lambda b, pt: idx)),
        out_specs=[pl.BlockSpec((DEC_T, LANES), lambda b, pt: (b, 0)),
                   pl.BlockSpec((DEC_T, HW), lambda b, pt: (b, 0)),
                   pl.BlockSpec((2, 1, nch, LANES), lambda b, pt: (0, b, 0, 0))],
        scratch_shapes=[pltpu.VMEM((2, nch, CHUNK_W), F32), pltpu.VMEM((2 * npg, PAGE, LANES), F32)],
    )
    return pl.pallas_call(
        kernel,
        grid_spec=grid_spec,
        out_shape=[jax.ShapeDtypeStruct((nbs * DEC_T, LANES), F32),
                   jax.ShapeDtypeStruct((nbs * DEC_T, HW), F32),
                   jax.ShapeDtypeStruct((2, nbs, nch, LANES), BF16)],
        compiler_params=_params("arbitrary"),
        name="sample_fox_compress",
    )(pt, *([logf_view] * npg), *([fox_view] * npg), *([nsa_view] * npg), pf, bf, pf, pf, pf, w1big, pe_rows, w2p)


def _nsa_sample_body(pages, q_ref, seln_ref, winn_ref, misc_ref, ck_ref, cv_ref, wst_ref, t5_ref, exp_ref, o_ref, n_s):
    npg = len(pages)
    past = npg * PAGE
    rows = N_HEADS * DEC_T
    nch = ck_ref.shape[2]
    wb = wst_ref.shape[4]
    lane8 = _iota((DEC_T, LANES), 1)
    t5c = t5_ref[...]

    def lut(rel):
        return _lut(t5c, _bucket(rel))

    def t_of(shape):
        return jnp.bitwise_and(_iota(shape, 0), DEC_T - 1)

    def g_rows(x):
        return jnp.concatenate([x] * NSA_HG, axis=0)

    q = q_ref[...]
    qrows = []
    for head in range(N_HEADS):
        g = head // NSA_HG
        x = q[:, (head // 2) * LANES:(head // 2 + 1) * LANES] * SCALE
        if head % 2 != g:
            x = pltpu.roll(x, HEAD_DIM, axis=1)
        qrows.append(jnp.where((lane8 >= HEAD_DIM) == (g == 1), x, 0.0))
    qbd = jnp.concatenate(qrows, axis=0).astype(BF16)
    bias_far = t5c[:, N_BUCKETS - 1:N_BUCKETS]

    raw_c = _dot_nt(qbd, ck_ref[0, 0])
    raw_pages = [_dot(qbd, pages[j][0, 0, 0, 0].astype(BF16)) for j in range(npg)]
    seln = _pad_rows(seln_ref[...], PAGE).astype(BF16)
    winn = _pad_rows(winn_ref[...], PAGE).astype(BF16)
    raw_new = _dot_nt(qbd, seln[:, 0:LANES])
    raw_w = _dot(qbd, wst_ref[0, 0, 0].astype(BF16))
    raw_wn = _dot_nt(qbd, winn[:, 0:LANES])

    qpos_c = past + t_of((rows, nch))
    rel_c = qpos_c - (_iota((rows, nch), 1) * CMP_STRIDE + (CMP_BLOCK - 1))
    mask_c = (rel_c >= 0) & (_iota((rows, nch), 1) < nch - 1)
    pc = _masked_softmax(raw_c + lut(rel_c), mask_c)
    psums = []
    for g in range(NSA_G):
        base = g * NSA_HG * DEC_T
        psum = pc[base:base + DEC_T]
        for hg in range(1, NSA_HG):
            psum = psum + pc[base + hg * DEC_T:base + (hg + 1) * DEC_T]
        psums.append(psum)

    rel_new = t_of((rows, PAGE)) - _iota((rows, PAGE), 1)
    in_new = _iota((rows, PAGE), 1) < DEC_T
    bias_new = lut(rel_new)
    rel_w = (wb + t_of((rows, wb))) - _iota((rows, wb), 1)
    mask_w = (rel_w >= 0) & (rel_w < WINDOW)
    s_w = jnp.where(mask_w, raw_w + lut(rel_w), NEG_INF)
    mask_wn = (rel_new >= 0) & (rel_new < WINDOW) & in_new
    s_wn = jnp.where(mask_wn, raw_wn + bias_new, NEG_INF)
    m = jnp.maximum(jnp.max(s_w, axis=-1, keepdims=True), jnp.max(s_wn, axis=-1, keepdims=True))
    p_w = jnp.where(mask_w, jnp.exp(s_w - m), 0.0)
    p_wn = jnp.where(mask_wn, jnp.exp(s_wn - m), 0.0)
    l_w = jnp.sum(p_w, axis=-1, keepdims=True) + jnp.sum(p_wn, axis=-1, keepdims=True)

    score = _split_dot(jnp.concatenate(psums, axis=0), _overlap(nch))
    o_cmp = _dot(pc.astype(BF16), cv_ref[0, 0])
    acc_w = _dot_nt(p_w.astype(BF16), wst_ref[0, 0, 1].astype(BF16)) + _dot(p_wn.astype(BF16), winn[:, LANES:2 * LANES])
    o_win = acc_w / jnp.maximum(l_w, TINY)

    qpos_g = past + jnp.bitwise_and(_iota((NSA_G * DEC_T, 1), 0), DEC_T - 1)
    picked = _select_blocks(score, qpos_g, n_s)
    picked = jnp.concatenate([g_rows(picked[g * DEC_T:(g + 1) * DEC_T]) for g in range(NSA_G)], axis=0)
    sel_past = _dot(picked.astype(BF16), exp_ref[...])
    last_blk = past // SEL_BLOCK
    sel_new = jnp.sum(jnp.where(_iota((rows, LANES), 1) == last_blk, picked, 0.0), axis=-1, keepdims=True) > 0.5
    s_pages, m_pages = [], []
    for j in range(npg):
        if past - (j + 1) * PAGE + 1 >= MAX_DISTANCE:
            s = raw_pages[j] + bias_far
        else:
            s = raw_pages[j] + lut(past + t_of((rows, PAGE)) - (j * PAGE + _iota((rows, PAGE), 1)))
        mk = sel_past[:, j * PAGE:(j + 1) * PAGE] > 0.5
        s_pages.append(jnp.where(mk, s, NEG_INF))
        m_pages.append(mk)
    mask_n = sel_new & (rel_new >= 0) & in_new
    s_new = jnp.where(mask_n, raw_new + bias_new, NEG_INF)
    m = jnp.max(s_new, axis=-1, keepdims=True)
    for s in s_pages:
        m = jnp.maximum(m, jnp.max(s, axis=-1, keepdims=True))
    p_new = jnp.where(mask_n, jnp.exp(s_new - m), 0.0)
    l = jnp.sum(p_new, axis=-1, keepdims=True)
    probs = []
    for j in range(npg):
        pr = jnp.where(m_pages[j], jnp.exp(s_pages[j] - m), 0.0)
        l = l + jnp.sum(pr, axis=-1, keepdims=True)
        probs.append(pr.astype(BF16))
    acc = _dot(p_new.astype(BF16), seln[:, LANES:2 * LANES])
    for j in range(npg):
        acc = acc + _dot_nt(probs[j], pages[j][0, 0, 0, 1].astype(BF16))
    o_sel = acc / jnp.maximum(l, TINY)

    sig = _sigmoid(misc_ref[:, 0:LANES])

    def gate(br):
        return jnp.concatenate(
            [sig[:, 8 + br * N_HEADS + h:9 + br * N_HEADS + h] for h in range(N_HEADS)], axis=0)

    o_all = gate(0) * o_cmp + gate(1) * o_sel + gate(2) * o_win
    pieces = []
    for head in range(N_HEADS):
        x = o_all[head * DEC_T:(head + 1) * DEC_T]
        if head % 2 != head // NSA_HG:
            x = pltpu.roll(x, HEAD_DIM, axis=1)
        pieces.append(x)
    for p in range(N_HEADS // 2):
        o_ref[:, p * LANES:(p + 1) * LANES] = jnp.where(lane8 < HEAD_DIM, pieces[2 * p], pieces[2 * p + 1])


def _sample_sb_nsa_kernel(pt_ref, *refs, npg, n_s):
    sb_pages, sel_pages = refs[0:npg], refs[npg:2 * npg]
    (qc_ref, kc_ref, vc_ref, q_ref, seln_ref, winn_ref, misc_ref, ck_ref, cv_ref, wst_ref, t5_ref, exp_ref,
     o_sb_ref, o_nsa_ref) = refs[2 * npg:]
    _sb_sample_body(sb_pages, qc_ref, kc_ref, vc_ref, o_sb_ref)
    _nsa_sample_body(sel_pages, q_ref, seln_ref, winn_ref, misc_ref, ck_ref, cv_ref, wst_ref, t5_ref, exp_ref,
                     o_nsa_ref, n_s)


def _sample_sb_nsa(pt, sb_view, nsa_view, pf, comp, win_view, t5col, expand, layer, nbs, npg, row0):
    nch = comp.shape[2]
    wb = win_view.shape[4]
    past = npg * PAGE
    kernel = functools.partial(_sample_sb_nsa_kernel, npg=npg, n_s=-(-(past + DEC_T) // SEL_BLOCK))
    rb = row0 // DEC_T
    grid_spec = pltpu.PrefetchScalarGridSpec(
        num_scalar_prefetch=1,
        grid=(nbs,),
        in_specs=_page_specs(npg, (1, 1, 2, HW, PAGE), layer, (0, 0, 0))
        + _page_specs(npg, (1, 1, 1, 2, LANES, PAGE), layer, (1, 0, 0, 0)) + [
            pl.BlockSpec((DEC_T, HW), lambda b, pt: (rb + b, C_QC // HW)),
            pl.BlockSpec((DEC_T, HW), lambda b, pt: (rb + b, C_KC // HW)),
            pl.BlockSpec((DEC_T, HW), lambda b, pt: (rb + b, C_VC // HW)),
            pl.BlockSpec((DEC_T, HW), lambda b, pt: (rb + b, C_QB // HW)),
            pl.BlockSpec((DEC_T, 2 * LANES), lambda b, pt: (rb + b, C_SEL // (2 * LANES))),
            pl.BlockSpec((DEC_T, 2 * LANES), lambda b, pt: (rb + b, C_WIN // (2 * LANES))),
            pl.BlockSpec((DEC_T, MISC_W), lambda b, pt: (rb + b, C_MISC // MISC_W)),
            pl.BlockSpec((1, 1, nch, LANES), lambda b, pt: (0, b, 0, 0)),
            pl.BlockSpec((1, 1, nch, LANES), lambda b, pt: (1, b, 0, 0)),
            pl.BlockSpec((1, 1, 2, LANES, wb), lambda b, pt: (b, layer, 0, 0, 0)),
            pl.BlockSpec((N_HEADS * DEC_T, LANES), lambda b, pt: (0, 0)),
            pl.BlockSpec((LANES, past), lambda b, pt: (0, 0))],
        out_specs=[pl.BlockSpec((DEC_T, HW), lambda b, pt: (b, 0)), pl.BlockSpec((DEC_T, HW), lambda b, pt: (b, 0))],
    )
    return pl.pallas_call(
        kernel,
        grid_spec=grid_spec,
        out_shape=[jax.ShapeDtypeStruct((nbs * DEC_T, HW), F32), jax.ShapeDtypeStruct((nbs * DEC_T, HW), F32)],
        compiler_params=_params("arbitrary"),
        name="sample_sb_nsa",
    )(pt, *([sb_view] * npg), *([nsa_view] * npg), pf, pf, pf, pf, pf, pf, pf, comp, comp, win_view, t5col, expand)


def _win_state_kernel(*refs):
    win_ref, new_refs, o_ref = refs[0], refs[1:-1], refs[-1]
    wb = win_ref.shape[4]
    lane = _iota((LANES, LANES), 1)
    for l, new_ref in enumerate(new_refs):
        for kv in range(2):
            shifted = pltpu.roll(win_ref[0, l, kv], wb - DEC_T, axis=1)
            new_t = _pad_rows(new_ref[:, kv * LANES:(kv + 1) * LANES], LANES).T
            tail = jnp.where(lane >= LANES - DEC_T, pltpu.roll(new_t, LANES - DEC_T, axis=1), shifted[:, wb - LANES:wb])
            o_ref[0, l, kv] = jnp.concatenate([shifted[:, 0:wb - LANES], tail], axis=1)


def _win_state(win_view, pfs, nbs, row0):
    _, depth, _, _, wb = win_view.shape
    rb = row0 // DEC_T
    blk = pl.BlockSpec((1, depth, 2, LANES, wb), lambda b: (b, 0, 0, 0, 0))
    return pl.pallas_call(
        _win_state_kernel,
        grid=(nbs,),
        in_specs=[blk] + [pl.BlockSpec((DEC_T, 2 * LANES), lambda b: (rb + b, C_WIN // (2 * LANES))) for _ in pfs],
        out_specs=blk,
        out_shape=jax.ShapeDtypeStruct(win_view.shape, F32),
        compiler_params=_params("arbitrary"),
        name="win_state",
    )(win_view, *pfs)


_STATE_SEGS = ((C_KA, 0, 0), (C_VA, 0, HW), (C_KC, 1, 0), (C_VC, 1, HW), (C_CMP, 2, 0))
_STATE_ROWS = (2 * HW, 2 * HW, HW)


def _prompt_states_kernel(*refs):
    nseg = len(_STATE_SEGS)
    depth = (len(refs) - len(_STATE_ROWS)) // nseg
    ins, outs = refs[:nseg * depth], refs[nseg * depth:]
    for l in range(depth):
        for src, (_, dst, row0) in zip(ins[nseg * l:nseg * (l + 1)], _STATE_SEGS):
            for c in range(HW // LANES):
                outs[dst][0, l, row0 + c * LANES:row0 + (c + 1) * LANES, :] = src[:, c * LANES:(c + 1) * LANES].T


def _prompt_states(pfs, nb, t):
    depth = len(pfs)
    tq = _pick(t, 512, LANES)
    nq = t // tq
    in_specs = [pl.BlockSpec((tq, HW), functools.partial(lambda b, i, c0: (b * nq + i, c0 // HW), c0=c0))
                for _ in range(depth) for c0, _, _ in _STATE_SEGS]
    return pl.pallas_call(
        _prompt_states_kernel,
        grid=(nb, nq),
        in_specs=in_specs,
        out_specs=[pl.BlockSpec((1, depth, w, tq), lambda b, i: (b, 0, 0, i)) for w in _STATE_ROWS],
        out_shape=[jax.ShapeDtypeStruct((nb, depth, w, t), F32) for w in _STATE_ROWS],
        compiler_params=_params("arbitrary", "arbitrary"),
        name="prompt_states",
    )(*[pf for pf in pfs for _ in _STATE_SEGS])


def _reorder_w_in(w_in, d):
    sizes = (HW, 2 * HW, N_HEADS, HW, 6 * NSA_G * HEAD_DIM, 3 * N_HEADS, HW, 2 * HW, 3 * d)
    offs = np.concatenate([[0], np.cumsum(sizes)])
    seg = [w_in[:, :, offs[i]:offs[i + 1]] for i in range(len(sizes))]
    q_a, kv_a, f_a, q_b, kv_b, g_b, q_c, kv_c, g_m = seg
    pad = jnp.zeros(w_in.shape[:2] + (MISC_W - N_HEADS - 3 * N_HEADS,), w_in.dtype)
    return jnp.concatenate([s.astype(BF16) for s in (q_a, kv_a, q_b, q_c, kv_c, kv_b, f_a, g_b, pad, g_m)], axis=-1)


def kernel(x_prompt, x_sample, cache_fox_kv, cache_fox_logf, cache_nsa_kv, cache_sb_kv, state_nsa_win_kv, page_table, norm_mix_g, norm_ffn_g, norm_final_g, w_in, b_forget, t5_table, cmp_pe, cmp_w1, cmp_w2, w_out_a, w_out_b, w_out_c, w_out, router_group_w, router_group_b, router_expert_w, router_expert_b, expert_w_gate, expert_w_up, expert_w_down):
    nb, t, d = x_prompt.shape
    nbs, dec_t, _ = x_sample.shape
    depth = w_in.shape[0]
    npool = cache_fox_kv.shape[0]
    npg = page_table.shape[1]
    past = npg * PAGE
    wb = state_nsa_win_kv.shape[2]
    assert dec_t == DEC_T and t % KB == 0 and d % 128 == 0 and wb == WINDOW and past >= WINDOW
    n_p, n_s_rows = nb * t, nbs * DEC_T

    fox_view = jnp.transpose(cache_fox_kv, (0, 1, 3, 4, 5, 2)).reshape(npool, depth, 2, HW, PAGE)
    sb_view = jnp.transpose(cache_sb_kv, (0, 1, 3, 4, 5, 2)).reshape(npool, depth, 2, HW, PAGE)
    nsa_view = jnp.transpose(cache_nsa_kv, (0, 1, 3, 4, 5, 6, 2)).reshape(npool, depth, 2, 2, LANES, PAGE)
    logf_view = jnp.transpose(cache_fox_logf, (0, 1, 3, 2))
    win_view = jnp.transpose(state_nsa_win_kv, (0, 1, 3, 4, 5, 2)).reshape(nbs, depth, 2, LANES, wb)
    pt = page_table.reshape(-1).astype(jnp.int32)

    w_in_r = _reorder_w_in(w_in, d)
    bf_pad = jnp.pad(b_forget.astype(F32), ((0, 0), (0, LANES - N_HEADS))).reshape(depth, 1, LANES)
    t5 = jnp.pad(t5_table.astype(F32).T, ((0, 0), (0, LANES - N_BUCKETS)))
    t5col = jnp.repeat(t5, DEC_T, axis=0)
    expand = jnp.asarray(np.arange(LANES)[:, None] == (np.arange(past)[None, :] // SEL_BLOCK), BF16)
    w1r = cmp_w1.reshape(depth, 2, 2, CMP_STRIDE, HEAD_DIM, CMP_HIDDEN)
    w1big = jnp.einsum("zchldk,gG->zclgdhGk", w1r, jnp.eye(NSA_G, dtype=w1r.dtype)).reshape(
        depth, 2, CHUNK_W, 2 * NSA_G * CMP_HIDDEN).astype(BF16)
    pe_rows = jnp.broadcast_to(cmp_pe.reshape(depth, 2, 2, CMP_STRIDE, 1, HEAD_DIM),
                               (depth, 2, 2, CMP_STRIDE, NSA_G, HEAD_DIM)).reshape(depth, 2, 2, CHUNK_W)
    pe_rows = jnp.pad(pe_rows, ((0, 0), (0, 0), (0, 6), (0, 0))).astype(F32)
    w2p = jnp.stack([jnp.pad(cmp_w2, ((0, 0), (0, 0), (0, 0), (g * HEAD_DIM, LANES - (g + 1) * HEAD_DIM)))
                     for g in range(NSA_G)], axis=2).astype(BF16)
    w_router = jnp.pad(jnp.concatenate([router_group_w, router_expert_w], axis=-1),
                       ((0, 0), (0, 0), (0, LANES - N_GROUPS - N_EXPERTS))).astype(F32)
    b_router = jnp.pad(jnp.concatenate([router_group_b, router_expert_b], axis=-1),
                       ((0, 0), (0, LANES - N_GROUPS - N_EXPERTS))).astype(F32).reshape(depth, 1, LANES)
    wa, wb_, wc, wo = (w.astype(BF16) for w in (w_out_a, w_out_b, w_out_c, w_out))
    wg, wu, wd = (w.astype(BF16) for w in (expert_w_gate, expert_w_up, expert_w_down))

    h = jnp.concatenate([x_prompt.reshape(n_p, d), x_sample.reshape(n_s_rows, d)], axis=0)
    st_p = [[] for _ in range(5)]
    st_s = [[] for _ in range(5)]
    pfs = []
    for l in range(depth):
        pf, pb = _proj(h, norm_mix_g[l], w_in_r, l)

        logf_p, cum_p = _cum_prompt(pf, bf_pad[l], nb, t)
        cum_t = jnp.transpose(cum_p[:, :N_HEADS].reshape(nb, t, N_HEADS), (0, 2, 1)).reshape(nb, N_HEADS, t // QB, QB)
        o_a_p = _fox_prompt_t(pb, cum_p, cum_t, nb, t)
        o_c_p = _sb_prompt_t(pb, nb, t)
        comp_p = _compress_prompt(pf, w1big, pe_rows, w2p, l, nb, t)
        o_b_p = _nsa_prompt_t(pb, pf, comp_p, t5, nb, t)

        logf_s, o_a_s, comp_s = _sample_fox(pt, logf_view, fox_view, nsa_view, pf, bf_pad[l], w1big, pe_rows, w2p,
                                            l, nbs, npg, n_p)
        o_c_s, o_b_s = _sample_sb_nsa(pt, sb_view, nsa_view, pf, comp_s, win_view, t5col, expand, l, nbs, npg, n_p)

        o_a = jnp.concatenate([o_a_p, o_a_s.astype(BF16)], axis=0)
        o_b = jnp.concatenate([o_b_p, o_b_s.astype(BF16)], axis=0)
        o_c = jnp.concatenate([o_c_p, o_c_s.astype(BF16)], axis=0)
        mixed = _merge(o_a, o_b, o_c, pf, wa, wb_, wc, d, l)
        h = _mm_res(mixed, wo, h, l)
        xn, comb, mcol, mrow, cnt = _router(h, norm_ffn_g[l], w_router[l], b_router[l])
        counts = cnt[:, 0, :N_GROUPS].astype(jnp.int32).reshape(-1)
        h = _moe(xn, comb, mcol, mrow, counts, h, wg, wu, wd, l)

        def rows(c0, width, shape, lo, hi):
            return pf[lo:hi, c0:c0 + width].reshape(shape)

        win_new_p = rows(C_WIN, 2 * LANES, (nb, t, 2, NSA_G, HEAD_DIM), 0, n_p)
        pfs.append(pf)
        st_p[1].append(logf_p[:, :N_HEADS].reshape(nb, t, N_HEADS))
        st_p[4].append(win_new_p[:, t - min(WINDOW, t):])
        st_s[0].append(rows(C_KA, 2 * HW, (nbs, DEC_T, 2, N_HEADS, HEAD_DIM), n_p, n_p + n_s_rows))
        st_s[1].append(logf_s[:, :N_HEADS].reshape(nbs, DEC_T, N_HEADS))
        st_s[2].append(rows(C_CMP, 4 * LANES, (nbs, DEC_T, 2, 2, NSA_G, HEAD_DIM), n_p, n_p + n_s_rows))
        st_s[3].append(rows(C_KC, 2 * HW, (nbs, DEC_T, 2, N_HEADS, HEAD_DIM), n_p, n_p + n_s_rows))

    y_prompt = _final_norm(h, norm_final_g, 0, n_p).reshape(nb, t, d)
    y_sample = _final_norm(h, norm_final_g, n_p, n_s_rows).reshape(nbs, DEC_T, d)
    ss = [jnp.stack(s, axis=1) for s in st_s[:4]]
    fox_t, sb_t, nsa_t = _prompt_states(pfs, nb, t)
    fox_p = jnp.transpose(fox_t.reshape(nb, depth, 2, N_HEADS, HEAD_DIM, t), (0, 1, 5, 2, 3, 4))
    sb_p = jnp.transpose(sb_t.reshape(nb, depth, 2, N_HEADS, HEAD_DIM, t), (0, 1, 5, 2, 3, 4))
    nsa_p = jnp.transpose(nsa_t.reshape(nb, depth, 2, 2, NSA_G, HEAD_DIM, t), (0, 1, 6, 2, 3, 4, 5))
    logf_p_all = jnp.stack(st_p[1], axis=1)
    win_p = jnp.stack(st_p[4], axis=1)
    win_s = _win_state(win_view, pfs, nbs, n_p).reshape(nbs, depth, 2, NSA_G, HEAD_DIM, wb)
    win_s = jnp.transpose(win_s, (0, 1, 5, 2, 3, 4))
    return (y_prompt, y_sample, fox_p, ss[0], logf_p_all, ss[1], nsa_p, ss[2], sb_p, ss[3], win_p, win_s)
```

```python
import functools
import math

import numpy as np
import jax
import jax.numpy as jnp
from jax import lax
from jax.experimental import pallas as pl
from jax.experimental.pallas import tpu as pltpu

F32 = jnp.float32
BF16 = jnp.bfloat16

HEAD_DIM = 64
N_HEADS = 8
NSA_G = 2
NSA_HG = N_HEADS // NSA_G
HW = N_HEADS * HEAD_DIM
PAGE = 128
DEC_T = 8
CMP_BLOCK = 32
CMP_STRIDE = 16
CMP_HIDDEN = 128
SEL_BLOCK = 64
SEL_TOPK = 8
WINDOW = 512
N_BUCKETS = 32
MAX_DISTANCE = 128
N_GROUPS = 4
EXPERTS_PER_GROUP = 4
N_EXPERTS = N_GROUPS * EXPERTS_PER_GROUP
RMS_EPS = 1e-6
NEG_INF = -1e30
FORCE_SCORE = 1e9
BELOW_ALL = -3e38
SCALE = HEAD_DIM ** -0.5
TINY = float(np.finfo(np.float32).tiny)
QB = 128
KB = 256
MOE_TILE = 768
MOE_CHUNK = 256
LANES = 128
VMEM_LIMIT = 56 * 1024 * 1024

C_QA, C_KA, C_VA, C_QB, C_QC, C_KC, C_VC = 0, 512, 1024, 1536, 2048, 2560, 3072
C_CMP, C_SEL, C_WIN, C_MISC, C_GM = 3584, 3840, 4096, 4352, 4608
MISC_W = 256
CHUNK_W = CMP_STRIDE * LANES


def _params(*sem):
    return pltpu.CompilerParams(dimension_semantics=sem, vmem_limit_bytes=VMEM_LIMIT)


def _pick(n, cap, mult):
    t = (min(cap, n) // mult) * mult
    while t > 0 and n % t:
        t -= mult
    assert t > 0, (n, cap, mult)
    return t


def _iota(shape, axis):
    return lax.broadcasted_iota(jnp.int32, shape, axis)


def _dot(a, b):
    return jnp.dot(a, b, preferred_element_type=F32)


def _dot_nt(a, b):
    return lax.dot_general(a, b, (((1,), (1,)), ((), ())), preferred_element_type=F32)


def _softplus(x):
    return jnp.maximum(x, 0.0) + jnp.log(1.0 + jnp.exp(-jnp.abs(x)))


def _log_sigmoid(x):
    return jnp.minimum(x, 0.0) - jnp.log1p(jnp.exp(-jnp.abs(x)))


def _sigmoid(x):
    return 1.0 / (1.0 + jnp.exp(-x))


def _split_dot(x, w):
    hi = x.astype(BF16)
    lo = (x - hi.astype(F32)).astype(BF16)
    return _dot(hi, w) + _dot(lo, w)


def _tri(w):
    return jnp.where(_iota((w, w), 0) > _iota((w, w), 1), 1.0, 0.0).astype(BF16)


def _bucket(rel):
    n = jnp.maximum(rel, 0)
    exact = N_BUCKETS // 2
    far = jnp.log(jnp.maximum(n, exact).astype(F32) / exact) / math.log(MAX_DISTANCE / exact)
    far = exact + (far * (N_BUCKETS - exact)).astype(jnp.int32)
    return jnp.where(n < exact, n, jnp.minimum(far, N_BUCKETS - 1))


def _masked_softmax(logits, mask):
    l = jnp.where(mask, logits, NEG_INF)
    e = jnp.where(mask, jnp.exp(l - jnp.max(l, axis=-1, keepdims=True)), 0.0)
    return e / jnp.maximum(jnp.sum(e, axis=-1, keepdims=True), TINY)


def _overlap(nch):
    cs = _iota((nch, LANES), 0) * CMP_STRIDE
    ss = _iota((nch, LANES), 1) * SEL_BLOCK
    ov = jnp.maximum(jnp.minimum(cs + CMP_BLOCK, ss + SEL_BLOCK) - jnp.maximum(cs, ss), 0)
    return (ov.astype(F32) / CMP_BLOCK).astype(BF16)


def _select_blocks(score, qpos, n_s):
    lane = _iota(score.shape, 1)
    lane_f = lane.astype(F32)
    valid = lane * SEL_BLOCK <= qpos
    forced = (lane == jnp.right_shift(qpos, 6)) | (lane == 0)
    sc = jnp.where(forced, FORCE_SCORE, jnp.where(valid, score, NEG_INF))
    sc = jnp.where(lane < n_s, sc, BELOW_ALL)
    rank = jnp.zeros(score.shape, F32)
    for k in range(n_s):
        other = sc[:, k:k + 1]
        rank = rank + jnp.where((other > sc) | ((other == sc) & (lane > k)), 1.0, 0.0)
    return jnp.where((rank < min(SEL_TOPK, n_s)) & (lane < n_s), 1.0, 0.0)


def _proj_kernel(x_ref, g_ref, w_ref, of_ref, ob_ref, xn_ref, *, nb16):
    @pl.when(pl.program_id(1) == 0)
    def _():
        x = x_ref[...]
        ms = jnp.mean(x * x, axis=-1, keepdims=True)
        xn_ref[...] = (x * lax.rsqrt(ms + RMS_EPS) * g_ref[...]).astype(BF16)

    y = _dot(xn_ref[...], w_ref[...])
    of_ref[...] = y

    @pl.when(pl.program_id(1) < nb16)
    def _():
        ob_ref[...] = y.astype(BF16)


def _proj(h, g, w, layer):
    n, d = h.shape
    wp = w.shape[2]
    tm = _pick(n, 1024, 16)
    tn = _pick(wp, 768, 128)
    nb16 = -(-C_GM // tn)
    return pl.pallas_call(
        functools.partial(_proj_kernel, nb16=nb16),
        grid=(n // tm, wp // tn),
        in_specs=[pl.BlockSpec((tm, d), lambda i, j: (i, 0)),
                  pl.BlockSpec((1, d), lambda i, j: (0, 0)),
                  pl.BlockSpec((None, d, tn), lambda i, j: (layer, 0, j))],
        out_specs=[pl.BlockSpec((tm, tn), lambda i, j: (i, j)),
                   pl.BlockSpec((tm, tn), lambda i, j: (i, jnp.minimum(j, nb16 - 1)))],
        out_shape=[jax.ShapeDtypeStruct((n, wp), F32), jax.ShapeDtypeStruct((n, nb16 * tn), BF16)],
        scratch_shapes=[pltpu.VMEM((tm, d), BF16)],
        compiler_params=_params("arbitrary", "arbitrary"),
        name="proj",
    )(h, g.reshape(1, d), w)


def _merge_kernel(oa_ref, ob_ref, oc_ref, wa_ref, wb_ref, wc_ref, g0_ref, g1_ref, g2_ref, o_ref):
    m = _sigmoid(g0_ref[...]) * _dot(oa_ref[...], wa_ref[...])
    m = m + _sigmoid(g1_ref[...]) * _dot(ob_ref[...], wb_ref[...])
    m = m + _sigmoid(g2_ref[...]) * _dot(oc_ref[...], wc_ref[...])
    o_ref[...] = m.astype(BF16)


def _merge(o_a, o_b, o_c, pf, wa, wb, wc, d, layer):
    n = o_a.shape[0]
    tm = _pick(n, 1024, 16)
    tn = _pick(d, 512, 128)
    gm0 = C_GM // tn
    o_spec = pl.BlockSpec((tm, HW), lambda i, j: (i, 0))
    w_spec = pl.BlockSpec((None, HW, tn), lambda i, j: (layer, 0, j))

    def g_spec(k):
        return pl.BlockSpec((tm, tn), lambda i, j: (i, gm0 + k * (d // tn) + j))

    return pl.pallas_call(
        _merge_kernel,
        grid=(n // tm, d // tn),
        in_specs=[o_spec, o_spec, o_spec, w_spec, w_spec, w_spec, g_spec(0), g_spec(1), g_spec(2)],
        out_specs=pl.BlockSpec((tm, tn), lambda i, j: (i, j)),
        out_shape=jax.ShapeDtypeStruct((n, d), BF16),
        compiler_params=_params("arbitrary", "arbitrary"),
        name="merge",
    )(o_a, o_b, o_c, wa, wb, wc, pf, pf, pf)


def _mm_res_kernel(x_ref, w_ref, r_ref, o_ref):
    o_ref[...] = r_ref[...] + _dot(x_ref[...], w_ref[...])


def _mm_res(x, w, res, layer):
    n, k = x.shape
    d = w.shape[2]
    tm = _pick(n, 1024, 16)
    tn = _pick(d, 512, 128)
    return pl.pallas_call(
        _mm_res_kernel,
        grid=(n // tm, d // tn),
        in_specs=[pl.BlockSpec((tm, k), lambda i, j: (i, 0)),
                  pl.BlockSpec((None, k, tn), lambda i, j: (layer, 0, j)),
                  pl.BlockSpec((tm, tn), lambda i, j: (i, j))],
        out_specs=pl.BlockSpec((tm, tn), lambda i, j: (i, j)),
        out_shape=jax.ShapeDtypeStruct((n, d), F32),
        compiler_params=_params("arbitrary", "arbitrary"),
        name="out_proj",
    )(x, w, res)


def _router_kernel(h_ref, g_ref, w_ref, b_ref, xn_ref, comb_ref, mcol_ref, mrow_ref, cnt_ref):
    x = h_ref[...]
    ms = jnp.mean(x * x, axis=-1, keepdims=True)
    xn = x * lax.rsqrt(ms + RMS_EPS) * g_ref[...]
    xn_ref[...] = xn.astype(BF16)
    logits = jnp.dot(xn, w_ref[...], precision=lax.Precision.HIGHEST, preferred_element_type=F32) + b_ref[...]
    lane = _iota(logits.shape, 1)
    lane_f = lane.astype(F32)
    is_grp = lane < N_GROUPS
    gl = jnp.where(is_grp, logits, BELOW_ALL)
    gmax = jnp.max(gl, axis=-1, keepdims=True)
    gsum = jnp.sum(jnp.where(is_grp, jnp.exp(gl - gmax), 0.0), axis=-1, keepdims=True)
    w_grp = 1.0 / gsum
    g_star = jnp.min(jnp.where(is_grp & (gl == gmax), lane_f, 1e9), axis=-1, keepdims=True)
    lo = N_GROUPS + g_star * EXPERTS_PER_GROUP
    in_grp = (lane_f >= lo) & (lane_f < lo + EXPERTS_PER_GROUP)
    el = jnp.where(in_grp, logits, BELOW_ALL)
    v1 = jnp.max(el, axis=-1, keepdims=True)
    i1 = jnp.min(jnp.where(el == v1, lane_f, 1e9), axis=-1, keepdims=True)
    el2 = jnp.where(lane_f == i1, BELOW_ALL, el)
    v2 = jnp.max(el2, axis=-1, keepdims=True)
    i2 = jnp.min(jnp.where(el2 == v2, lane_f, 1e9), axis=-1, keepdims=True)
    e2 = jnp.exp(v2 - v1)
    den = 1.0 + e2
    comb = jnp.where(lane_f == i1, w_grp / den, 0.0) + jnp.where(lane_f == i2, w_grp * e2 / den, 0.0)
    comb_ref[...] = comb
    member = jnp.where(lane_f == g_star, 1.0, 0.0)
    tm = member.shape[0]
    row = _iota(member.shape, 0)
    seen = member
    s = 1
    while s < tm:
        seen = seen + jnp.where(row >= s, pltpu.roll(seen, s, axis=0), 0.0)
        s *= 2
    rank = jnp.sum((seen - member) * member, axis=-1, keepdims=True)
    meta = jnp.where(lane == 0, g_star, jnp.where(lane == 1, rank, 0.0))
    mcol_ref[...] = meta
    mrow_ref[0] = meta.T[0:8]
    cnt_ref[0] = jnp.broadcast_to(seen[tm - 1:tm, :], (8, LANES))


def _router(h, g, w, b):
    n, d = h.shape
    tm = _pick(n, MOE_TILE, LANES)
    nt = n // tm
    return pl.pallas_call(
        _router_kernel,
        grid=(nt,),
        in_specs=[pl.BlockSpec((tm, d), lambda i: (i, 0)),
                  pl.BlockSpec((1, d), lambda i: (0, 0)),
                  pl.BlockSpec((d, LANES), lambda i: (0, 0)),
                  pl.BlockSpec((1, LANES), lambda i: (0, 0))],
        out_specs=[pl.BlockSpec((tm, d), lambda i: (i, 0)),
                   pl.BlockSpec((tm, LANES), lambda i: (i, 0)),
                   pl.BlockSpec((tm, LANES), lambda i: (i, 0)),
                   pl.BlockSpec((1, 8, tm), lambda i: (i, 0, 0)),
                   pl.BlockSpec((1, 8, LANES), lambda i: (i, 0, 0))],
        out_shape=[jax.ShapeDtypeStruct((n, d), BF16), jax.ShapeDtypeStruct((n, LANES), F32),
                   jax.ShapeDtypeStruct((n, LANES), F32), jax.ShapeDtypeStruct((nt, 8, tm), F32),
                   jax.ShapeDtypeStruct((nt, 8, LANES), F32)],
        compiler_params=_params("arbitrary"),
        name="router",
    )(h, g.reshape(1, d), w, b)


def _moe_kernel(cnt_ref, x_ref, c_ref, mcol_ref, mrow_ref, h_ref, wg_ref, wu_ref, wd_ref, o_ref, xc_ref, cc_ref, y_ref, *, ch):
    i, e = pl.program_id(0), pl.program_id(1)
    tm = x_ref.shape[0]
    grp = jnp.right_shift(e, 2)
    grp_f = grp.astype(F32)
    nchunk = (cnt_ref[i * N_GROUPS + grp] + ch - 1) // ch

    @pl.when(e == 0)
    def _():
        o_ref[...] = h_ref[...]

    @pl.when(jnp.bitwise_and(e, EXPERTS_PER_GROUP - 1) == 0)
    def _():
        grp_row, rank_row = mrow_ref[0, 0:1, :], mrow_ref[0, 1:2, :]
        comb = c_ref[...]
        c1 = comb.astype(BF16)
        rest = comb - c1.astype(F32)
        c2 = rest.astype(BF16)
        c3 = (rest - c2.astype(F32)).astype(BF16)

        def gather(c, _):
            base = pl.multiple_of(c * ch, ch)
            slot = (base + _iota((ch, tm), 0)).astype(F32)
            pick = jnp.where((grp_row == grp_f) & (rank_row == slot), 1.0, 0.0).astype(BF16)
            xc_ref[pl.ds(base, ch), :] = _dot(pick, x_ref[...]).astype(BF16)
            cc_ref[pl.ds(base, ch), :] = _dot(pick, c1) + _dot(pick, c2) + _dot(pick, c3)
            y_ref[pl.ds(base, ch), :] = jnp.zeros((ch, y_ref.shape[1]), F32)
            return 0

        lax.fori_loop(0, nchunk, gather, 0)

    def expert(c, _):
        base = pl.multiple_of(c * ch, ch)
        xc = xc_ref[pl.ds(base, ch), :]
        cc = cc_ref[pl.ds(base, ch), :]
        ce = jnp.sum(jnp.where(_iota(cc.shape, 1) == e + N_GROUPS, cc, 0.0), axis=-1, keepdims=True)
        gate = _dot(xc, wg_ref[0])
        up = _dot(xc, wu_ref[0])
        hh = gate * _sigmoid(gate) * up * ce
        y_ref[pl.ds(base, ch), :] += _dot(hh.astype(BF16), wd_ref[0])
        return 0

    lax.fori_loop(0, nchunk, expert, 0)

    @pl.when(jnp.bitwise_and(e, EXPERTS_PER_GROUP - 1) == EXPERTS_PER_GROUP - 1)
    def _():
        grp_col, rank_col = mcol_ref[:, 0:1], mcol_ref[:, 1:2]

        def scatter(c, _):
            base = pl.multiple_of(c * ch, ch)
            slot = (base + _iota((tm, ch), 1)).astype(F32)
            place = jnp.where((grp_col == grp_f) & (rank_col == slot), 1.0, 0.0).astype(BF16)
            y = y_ref[pl.ds(base, ch), :]
            hi = y.astype(BF16)
            lo = (y - hi.astype(F32)).astype(BF16)
            o_ref[...] += _dot(place, hi) + _dot(place, lo)
            return 0

        lax.fori_loop(0, nchunk, scatter, 0)


def _moe(xn, comb, mcol, mrow, counts, h, wg, wu, wd, layer):
    n, d = h.shape
    _, ne, _, ff = wg.shape
    nt, _, tm = mrow.shape
    ch = min(MOE_CHUNK, tm)
    cap = -(-tm // ch) * ch
    kernel = functools.partial(_moe_kernel, ch=ch)
    grid_spec = pltpu.PrefetchScalarGridSpec(
        num_scalar_prefetch=1,
        grid=(nt, ne),
        in_specs=[pl.BlockSpec((tm, d), lambda i, e, cnt: (i, 0)),
                  pl.BlockSpec((tm, LANES), lambda i, e, cnt: (i, 0)),
                  pl.BlockSpec((tm, LANES), lambda i, e, cnt: (i, 0)),
                  pl.BlockSpec((1, 8, tm), lambda i, e, cnt: (i, 0, 0)),
                  pl.BlockSpec((tm, d), lambda i, e, cnt: (i, 0), pipeline_mode=pl.Buffered(1)),
                  pl.BlockSpec((None, 1, d, ff), lambda i, e, cnt: (layer, e, 0, 0)),
                  pl.BlockSpec((None, 1, d, ff), lambda i, e, cnt: (layer, e, 0, 0)),
                  pl.BlockSpec((None, 1, ff, d), lambda i, e, cnt: (layer, e, 0, 0))],
        out_specs=pl.BlockSpec((tm, d), lambda i, e, cnt: (i, 0), pipeline_mode=pl.Buffered(1)),
        scratch_shapes=[pltpu.VMEM((cap, d), BF16), pltpu.VMEM((cap, LANES), F32), pltpu.VMEM((cap, d), F32)],
    )
    return pl.pallas_call(
        kernel,
        grid_spec=grid_spec,
        out_shape=jax.ShapeDtypeStruct((n, d), F32),
        compiler_params=_params("arbitrary", "arbitrary"),
        name="moe",
    )(counts, xn, comb, mcol, mrow, h, wg, wu, wd)


def _final_norm_kernel(h_ref, g_ref, o_ref):
    x = h_ref[...]
    ms = jnp.mean(x * x, axis=-1, keepdims=True)
    o_ref[...] = x * lax.rsqrt(ms + RMS_EPS) * g_ref[...]


def _final_norm(h, g, row0, rows):
    d = h.shape[1]
    tm = math.gcd(_pick(rows, 512, 8), row0) if row0 else _pick(rows, 512, 8)
    return pl.pallas_call(
        _final_norm_kernel,
        grid=(rows // tm,),
        in_specs=[pl.BlockSpec((tm, d), lambda i: (row0 // tm + i, 0)), pl.BlockSpec((1, d), lambda i: (0, 0))],
        out_specs=pl.BlockSpec((tm, d), lambda i: (i, 0)),
        out_shape=jax.ShapeDtypeStruct((rows, d), F32),
        compiler_params=_params("arbitrary"),
        name="final_norm",
    )(h, g.reshape(1, d))


def _cum_prompt_kernel(m_ref, bf_ref, logf_ref, cum_ref):
    logf = _log_sigmoid(m_ref[:, 0:LANES] + bf_ref[...])
    logf_ref[...] = logf
    t = logf.shape[0]
    row = _iota(logf.shape, 0)
    c = logf
    s = 1
    while s < t:
        c = c + jnp.where(row >= s, pltpu.roll(c, s, axis=0), 0.0)
        s *= 2
    cum_ref[...] = c


def _cum_prompt(pf, bf, nb, t):
    return pl.pallas_call(
        _cum_prompt_kernel,
        grid=(nb,),
        in_specs=[pl.BlockSpec((t, MISC_W), lambda b: (b, C_MISC // MISC_W)),
                  pl.BlockSpec((1, LANES), lambda b: (0, 0))],
        out_specs=[pl.BlockSpec((t, LANES), lambda b: (b, 0)), pl.BlockSpec((t, LANES), lambda b: (b, 0))],
        out_shape=[jax.ShapeDtypeStruct((nb * t, LANES), F32), jax.ShapeDtypeStruct((nb * t, LANES), F32)],
        compiler_params=_params("arbitrary"),
        name="cum_prompt",
    )(pf, bf)


def _stack_pair(q_ref, p, lane):
    qp = q_ref[:, p * LANES:(p + 1) * LANES] * jnp.asarray(SCALE, BF16)
    zero = jnp.zeros_like(qp)
    return jnp.concatenate([jnp.where(lane < HEAD_DIM, qp, zero), jnp.where(lane >= HEAD_DIM, qp, zero)], axis=0)


def _compress_chunks(ch_ref, w1_ref, pe_ref, w2_ref, o_ref):
    nch = ch_ref.shape[1]
    hw = NSA_G * CMP_HIDDEN
    for c in range(2):
        w1 = w1_ref[0, c]
        pe = _dot(pe_ref[0, c].astype(BF16), w1)
        y = _dot(ch_ref[c].astype(BF16), w1)
        out = None
        for g in range(NSA_G):
            lo = slice(g * CMP_HIDDEN, (g + 1) * CMP_HIDDEN)
            hi = slice(hw + g * CMP_HIDDEN, hw + (g + 1) * CMP_HIDDEN)
            pre = y[:, lo] + pltpu.roll(y[:, hi], nch - 1, axis=0) + pe[0:1, lo] + pe[1:2, hi]
            hid = 0.5 * pre * (1.0 + jnp.tanh(math.sqrt(2.0 / math.pi) * (pre + 0.044715 * pre * pre * pre)))
            cg = _dot(hid.astype(BF16), w2_ref[0, c, g])
            out = cg if out is None else out + cg
        o_ref[c, 0] = out.astype(BF16)


def _compress_weight_specs(layer, imap):
    return [pl.BlockSpec((1, 2, CHUNK_W, 2 * NSA_G * CMP_HIDDEN), imap((layer, 0, 0, 0))),
            pl.BlockSpec((1, 2, 8, CHUNK_W), imap((layer, 0, 0, 0))),
            pl.BlockSpec((1, 2, NSA_G, CMP_HIDDEN, LANES), imap((layer, 0, 0, 0, 0)))]


def _compress_prompt_kernel(xk_ref, xv_ref, w1_ref, pe_ref, w2_ref, o_ref, ch_ref):
    nch = ch_ref.shape[1]
    for c, x_ref in enumerate((xk_ref, xv_ref)):
        for r in range(CMP_STRIDE):
            ch_ref[c, :, r * LANES:(r + 1) * LANES] = x_ref[pl.ds(r, nch, stride=CMP_STRIDE), :]
    _compress_chunks(ch_ref, w1_ref, pe_ref, w2_ref, o_ref)


def _compress_prompt(pf, w1big, pe_rows, w2p, layer, nb, t):
    nch = t // CMP_STRIDE
    return pl.pallas_call(
        _compress_prompt_kernel,
        grid=(nb,),
        in_specs=[pl.BlockSpec((t, LANES), lambda b: (b, C_CMP // LANES)),
                  pl.BlockSpec((t, LANES), lambda b: (b, C_CMP // LANES + 1))]
        + _compress_weight_specs(layer, lambda idx: (lambda b: idx)),
        out_specs=pl.BlockSpec((2, 1, nch, LANES), lambda b: (0, b, 0, 0)),
        out_shape=jax.ShapeDtypeStruct((2, nb, nch, LANES), BF16),
        scratch_shapes=[pltpu.VMEM((2, nch, CHUNK_W), F32)],
        compiler_params=_params("arbitrary"),
        name="nsa_compress_prompt",
    )(pf, pf, w1big, pe_rows, w2p)


def _lut(tab, bk):
    parts = [jnp.take_along_axis(tab, bk[:, c:c + LANES], axis=1) for c in range(0, bk.shape[1], LANES)]
    return parts[0] if len(parts) == 1 else jnp.concatenate(parts, axis=1)


def _transpose_bf16(x):
    return x.astype(F32).T.astype(BF16)


def _pair_out(o_t, lane_dtype=BF16):
    own = jnp.concatenate([o_t[0:HEAD_DIM, 0:QB], o_t[HEAD_DIM:LANES, QB:2 * QB]], axis=0)
    return own.T.astype(lane_dtype)


def _fox_t_kernel(q_ref, k_ref, v_ref, cum_ref, cq_ref, o_ref, vt_ref, ckb_ref):
    i = pl.program_id(1)
    npair = N_HEADS // 2
    nkb = vt_ref.shape[0]

    @pl.when(i == 0)
    def _():
        for kb in range(nkb):
            for p in range(npair):
                vt_ref[kb, p * LANES:(p + 1) * LANES, :] = _transpose_bf16(v_ref[kb * KB:(kb + 1) * KB, p * LANES:(p + 1) * LANES])
        for h in range(N_HEADS):
            ckb_ref[h] = jnp.broadcast_to(cum_ref[:, h:h + 1], ckb_ref.shape[1:])

    lane = _iota((QB, LANES), 1)
    qrow = i * QB + jnp.bitwise_and(_iota((1, 2 * QB), 1), QB - 1)
    krow = _iota((KB, 1), 0)
    q2 = [_stack_pair(q_ref, p, lane) for p in range(npair)]
    cq2 = [jnp.concatenate([cq_ref[0, 2 * p, pl.ds(i, 1), :], cq_ref[0, 2 * p + 1, pl.ds(i, 1), :]], axis=1)
           for p in range(npair)]

    def step(kb, carry, masked):
        k0 = pl.multiple_of(kb * KB, KB)
        scores = [_dot_nt(k_ref[pl.ds(k0, KB), p * LANES:(p + 1) * LANES], q2[p]) for p in range(npair)]
        probs, stats = [], []
        for p in range(npair):
            m, l, _ = carry[p]
            ck = jnp.concatenate([ckb_ref[2 * p, pl.ds(k0, KB), :], ckb_ref[2 * p + 1, pl.ds(k0, KB), :]], axis=1)
            s = scores[p] + cq2[p] - ck
            if masked:
                s = jnp.where((k0 + krow) <= qrow, s, NEG_INF)
            m_new = jnp.maximum(m, jnp.max(s, axis=0, keepdims=True))
            pr = jnp.exp(s - m_new)
            alpha = jnp.exp(m - m_new)
            probs.append(pr.astype(BF16))
            stats.append((m_new, alpha * l + jnp.sum(pr, axis=0, keepdims=True), alpha))
        out = []
        for p in range(npair):
            m_new, l, alpha = stats[p]
            acc = alpha * carry[p][2] + _dot(vt_ref[kb, p * LANES:(p + 1) * LANES, :], probs[p])
            out.append((m_new, l, acc))
        return tuple(out)

    init = tuple((jnp.full((1, 2 * QB), NEG_INF, F32), jnp.zeros((1, 2 * QB), F32), jnp.zeros((LANES, 2 * QB), F32))
                 for _ in range(npair))
    last = (i * QB) // KB
    carry = lax.fori_loop(0, last, lambda kb, c: step(kb, c, False), init)
    carry = step(last, carry, True)
    for p in range(npair):
        m, l, acc = carry[p]
        o_ref[:, p * LANES:(p + 1) * LANES] = _pair_out(acc / jnp.maximum(l, TINY))


def _fox_prompt_t(pb, cum, cum_t, nb, t):
    nq = t // QB
    return pl.pallas_call(
        _fox_t_kernel,
        grid=(nb, nq),
        in_specs=[pl.BlockSpec((QB, HW), lambda b, i: (b * nq + i, C_QA // HW)),
                  pl.BlockSpec((t, HW), lambda b, i: (b, C_KA // HW)),
                  pl.BlockSpec((t, HW), lambda b, i: (b, C_VA // HW)),
                  pl.BlockSpec((t, LANES), lambda b, i: (b, 0)),
                  pl.BlockSpec((1, N_HEADS, nq, QB), lambda b, i: (b, 0, 0, 0))],
        out_specs=pl.BlockSpec((QB, HW), lambda b, i: (b * nq + i, 0)),
        out_shape=jax.ShapeDtypeStruct((nb * t, HW), BF16),
        scratch_shapes=[pltpu.VMEM((t // KB, HW, KB), BF16), pltpu.VMEM((N_HEADS, t, LANES), F32)],
        compiler_params=_params("arbitrary", "arbitrary"),
        name="fox_prompt",
    )(pb, pb, pb, cum, cum_t)


def _sb_t_kernel(q_ref, k_ref, v_ref, o_ref, vt_ref):
    i = pl.program_id(1)
    npair = N_HEADS // 2
    nkb = vt_ref.shape[0]

    @pl.when(i == 0)
    def _():
        for kb in range(nkb):
            for p in range(npair):
                vt_ref[kb, p * LANES:(p + 1) * LANES, :] = _transpose_bf16(v_ref[kb * KB:(kb + 1) * KB, p * LANES:(p + 1) * LANES])

    lane = _iota((QB, LANES), 1)
    qrow = i * QB + jnp.bitwise_and(_iota((1, 2 * QB), 1), QB - 1)
    krow = _iota((KB, 1), 0)
    later = jnp.where(_iota((KB, KB), 1) > _iota((KB, KB), 0), 1.0, 0.0).astype(BF16)
    q2 = [_stack_pair(q_ref, p, lane) for p in range(npair)]

    def step(kb, carry, masked):
        k0 = pl.multiple_of(kb * KB, KB)
        vis = (k0 + krow) < qrow
        zs = [_dot_nt(k_ref[pl.ds(k0, KB), p * LANES:(p + 1) * LANES], q2[p]) for p in range(npair)]
        sps = [_softplus(z) for z in zs]
        lks = [jnp.where(vis, -sp, 0.0) if masked else -sp for sp in sps]
        his = [lk.astype(BF16) for lk in lks]
        los = [(lk - hi.astype(F32)).astype(BF16) for lk, hi in zip(lks, his)]
        betweens = [_dot(later, hi) + _dot(later, lo) for hi, lo in zip(his, los)]
        weights = []
        for p in range(npair):
            a = jnp.exp(zs[p] - sps[p] + betweens[p] + carry[p][0])
            weights.append((jnp.where(vis, a, 0.0) if masked else a).astype(BF16))
        out = []
        for p in range(npair):
            r, acc = carry[p]
            acc = acc + _dot(vt_ref[kb, p * LANES:(p + 1) * LANES, :], weights[p])
            out.append((r + jnp.sum(lks[p], axis=0, keepdims=True), acc))
        return tuple(out)

    init = tuple((jnp.zeros((1, 2 * QB), F32), jnp.zeros((LANES, 2 * QB), F32)) for _ in range(npair))
    last = (i * QB) // KB
    carry = step(last, init, True)
    carry = lax.fori_loop(0, last, lambda n, c: step(last - 1 - n, c, False), carry)
    for p in range(npair):
        o_ref[:, p * LANES:(p + 1) * LANES] = _pair_out(carry[p][1])


def _sb_prompt_t(pb, nb, t):
    nq = t // QB
    return pl.pallas_call(
        _sb_t_kernel,
        grid=(nb, nq),
        in_specs=[pl.BlockSpec((QB, HW), lambda b, i: (b * nq + i, C_QC // HW)),
                  pl.BlockSpec((t, HW), lambda b, i: (b, C_KC // HW)),
                  pl.BlockSpec((t, HW), lambda b, i: (b, C_VC // HW))],
        out_specs=pl.BlockSpec((QB, HW), lambda b, i: (b * nq + i, 0)),
        out_shape=jax.ShapeDtypeStruct((nb * t, HW), BF16),
        scratch_shapes=[pltpu.VMEM((t // KB, HW, KB), BF16)],
        compiler_params=_params("arbitrary", "arbitrary"),
        name="sb_prompt",
    )(pb, pb, pb)


def _select_blocks_t(score, qrow, n_s, sc_ref):
    blk = _iota(score.shape, 0)
    blk_f = blk.astype(F32)
    valid = blk * SEL_BLOCK <= qrow
    forced = (blk == jnp.right_shift(qrow, 6)) | (blk == 0)
    sc = jnp.where(forced, FORCE_SCORE, jnp.where(valid, score, NEG_INF))
    sc = jnp.where(blk < n_s, sc, BELOW_ALL)
    rows = min(score.shape[0], -(-n_s // 8) * 8)
    sc, blk = sc[0:rows], _iota((rows, score.shape[1]), 0)
    sc_ref[0:rows, :] = sc
    rank = jnp.zeros(sc.shape, F32)
    for k in range(n_s):
        other = sc_ref[k:k + 1, :]
        rank = rank + jnp.where((other > sc) | ((other == sc) & (blk > k)), 1.0, 0.0)
    sel = jnp.where((rank < min(SEL_TOPK, n_s)) & (blk < n_s), 1.0, 0.0)
    if rows < score.shape[0]:
        sel = jnp.concatenate([sel, jnp.zeros((score.shape[0] - rows, score.shape[1]), F32)], axis=0)
    return sel


def _nsa_t_kernel(t5_ref, q_ref, ck_ref, cv_ref, sel_ref, win_ref, misc_ref, o_ref, svt_ref, wvt_ref, sc_ref, *, n_s):
    i = pl.program_id(1)
    q0 = i * QB
    nch = ck_ref.shape[2]
    nblk = svt_ref.shape[0]
    cols = N_HEADS * QB

    @pl.when(i == 0)
    def _():
        for kb in range(nblk):
            svt_ref[kb] = _transpose_bf16(sel_ref[kb * QB:(kb + 1) * QB, LANES:2 * LANES])
            wvt_ref[kb] = _transpose_bf16(win_ref[kb * QB:(kb + 1) * QB, LANES:2 * LANES])

    lane = _iota((QB, LANES), 1)
    qrow = q0 + _iota((1, QB), 1)

    def per_head(f):
        return jnp.concatenate([f(h) for h in range(N_HEADS)], axis=1)

    def per_group(x0, x1):
        return jnp.concatenate([x0] * NSA_HG + [x1] * NSA_HG, axis=1)

    def bias_of(rel):
        bk = _bucket(rel)
        return per_head(lambda h: _lut(jnp.broadcast_to(t5_ref[h:h + 1, :], (rel.shape[0], LANES)), bk))

    def q_head(head):
        g = head // NSA_HG
        x = q_ref[:, (head // 2) * LANES:(head // 2 + 1) * LANES].astype(F32) * SCALE
        if head % 2 != g:
            x = pltpu.roll(x, HEAD_DIM, axis=1)
        return jnp.where((lane >= HEAD_DIM) == (g == 1), x, 0.0).astype(BF16)

    q8 = jnp.concatenate([q_head(h) for h in range(N_HEADS)], axis=0)
    rel_d = _iota((QB, QB), 1) - _iota((QB, QB), 0)
    bias_d = bias_of(rel_d)
    bias_p = bias_of(rel_d + QB)
    bias_far = per_head(lambda h: jnp.broadcast_to(t5_ref[h:h + 1, N_BUCKETS - 1:N_BUCKETS], (1, QB)))

    nrow = _iota((nch, QB), 0)
    rel_c = qrow - (nrow * CMP_STRIDE + (CMP_BLOCK - 1))
    vis_c = jnp.where((rel_c >= 0) & (nrow < nch - 1), 1.0, 0.0)
    mask_c = per_group(vis_c, vis_c) > 0.5
    lc = jnp.where(mask_c, _dot_nt(ck_ref[0, 0], q8) + bias_of(rel_c), NEG_INF)
    ec = jnp.where(mask_c, jnp.exp(lc - jnp.max(lc, axis=0, keepdims=True)), 0.0)
    pc = ec / jnp.maximum(jnp.sum(ec, axis=0, keepdims=True), TINY)
    o_cmp = _dot(_transpose_bf16(cv_ref[0, 0]), pc.astype(BF16))
    psums = []
    for g in range(NSA_G):
        ps = pc[:, g * NSA_HG * QB:(g * NSA_HG + 1) * QB]
        for hg in range(1, NSA_HG):
            ps = ps + pc[:, (g * NSA_HG + hg) * QB:(g * NSA_HG + hg + 1) * QB]
        psums.append(ps)
    psum = jnp.concatenate(psums, axis=1)
    cs = _iota((LANES, nch), 1) * CMP_STRIDE
    ss = _iota((LANES, nch), 0) * SEL_BLOCK
    ov_t = (jnp.maximum(jnp.minimum(cs + CMP_BLOCK, ss + SEL_BLOCK) - jnp.maximum(cs, ss), 0).astype(F32)
            / CMP_BLOCK).astype(BF16)
    p_hi = psum.astype(BF16)
    p_lo = (psum - p_hi.astype(F32)).astype(BF16)
    selm = _select_blocks_t(_dot(ov_t, p_hi) + _dot(ov_t, p_lo), jnp.concatenate([qrow] * NSA_G, axis=1),
                            n_s, sc_ref).astype(BF16)

    def attend(steps):
        loaded = []
        for k_ref, vt_ref, k0, width, _, _, _, real in steps:
            k0 = pl.multiple_of(k0, QB)
            kk = k_ref[pl.ds(k0, width), 0:LANES]
            kb = jnp.right_shift(k0, 7)
            vt = vt_ref[kb] if width == QB else jnp.concatenate([vt_ref[kb], vt_ref[kb + 1]], axis=1)
            if real is not None:
                kk = jnp.where(real, kk, jnp.zeros_like(kk))
                vt = jnp.where(real, vt, jnp.zeros_like(vt))
            loaded.append((kk, vt))
        scores = [_dot_nt(kk, q8) for kk, _ in loaded]
        soft = []
        for (_, _, _, _, hide, bias, (m, l, _), _), s in zip(steps, scores):
            s = s + (bias + hide)
            m_new = jnp.maximum(m, jnp.max(s, axis=0, keepdims=True))
            pr = jnp.exp(s - m_new)
            alpha = jnp.exp(m - m_new)
            soft.append((m_new, alpha * l + jnp.sum(pr, axis=0, keepdims=True), alpha, pr.astype(BF16)))
        return [(m_new, l, alpha * step[6][2] + _dot(vt, pr))
                for step, (_, vt), (m_new, l, alpha, pr) in zip(steps, loaded, soft)]

    def init():
        return (jnp.full((1, cols), NEG_INF, F32), jnp.zeros((1, cols), F32), jnp.zeros((LANES, cols), F32))

    def win_step(dlt, carry):
        rel = qrow - ((i - dlt) * QB + _iota((QB, 1), 0))
        hide = jnp.where((rel >= 0) & (rel < WINDOW), 0.0, NEG_INF)
        bias = bias_d if dlt == 0 else (bias_p if dlt == 1 else bias_far)
        return (win_ref, wvt_ref, jnp.maximum(i - dlt, 0) * QB, QB, per_group(hide, hide), bias, carry,
                None if dlt == 0 else i - dlt >= 0)

    def sel_step(k0, width, bias, carry, causal=False, valid=None):
        blk = jnp.right_shift(k0 + _iota((width, LANES), 0), 6)
        expand = jnp.where(_iota((width, LANES), 1) == blk, 1.0, 0.0).astype(BF16)
        picked = _dot(expand, selm)
        if causal:
            picked = picked * jnp.concatenate([jnp.where((k0 + _iota((width, 1), 0)) <= qrow, 1.0, 0.0)] * NSA_G, axis=1)
        if valid is not None:
            picked = picked * jnp.where(valid, 1.0, 0.0)
        hide = jnp.where(picked > 0.5, 0.0, NEG_INF)
        return (sel_ref, svt_ref, k0, width, per_group(hide[:, 0:QB], hide[:, QB:2 * QB]), bias, carry, None)

    k_prev = jnp.maximum(q0 - QB, 0)
    k_odd = jnp.maximum(q0 - 2 * QB, 0)
    c_win, c_sel = attend([win_step(0, init()), sel_step(q0, QB, bias_d, init(), causal=True)])
    c_win, c_sel = attend([win_step(1, c_win), sel_step(k_prev, QB, bias_p, c_sel, valid=i >= 1)])
    c_win, c_sel = attend([win_step(2, c_win),
                           sel_step(k_odd, QB, bias_far, c_sel, valid=(i >= 2) & (jnp.bitwise_and(i, 1) == 0))])
    for dlt in range(3, WINDOW // QB + 1):
        (c_win,) = attend([win_step(dlt, c_win)])
    o_win = c_win[2] / jnp.maximum(c_win[1], TINY)
    c_sel = lax.fori_loop(0, jnp.right_shift(jnp.maximum(i - 1, 0), 1),
                          lambda kb, c: attend([sel_step(kb * KB, KB, bias_far, c)])[0], c_sel)
    o_sel = c_sel[2] / jnp.maximum(c_sel[1], TINY)

    sig_t = _sigmoid(misc_ref[:, 0:LANES]).T

    def gate(br):
        return per_head(lambda h: sig_t[8 + br * N_HEADS + h:9 + br * N_HEADS + h, :])

    o8 = gate(0) * o_cmp + gate(1) * o_sel + gate(2) * o_win
    for p in range(N_HEADS // 2):
        own = []
        for head in (2 * p, 2 * p + 1):
            g = head // NSA_HG
            own.append(o8[g * HEAD_DIM:(g + 1) * HEAD_DIM, head * QB:(head + 1) * QB])
        o_ref[:, p * LANES:(p + 1) * LANES] = jnp.concatenate(own, axis=0).T.astype(BF16)


def _nsa_prompt_t(pb, pf, comp, t5, nb, t):
    nq = t // QB
    nch = comp.shape[2]
    kernel = functools.partial(_nsa_t_kernel, n_s=-(-t // SEL_BLOCK))
    return pl.pallas_call(
        kernel,
        grid=(nb, nq),
        in_specs=[pl.BlockSpec((N_HEADS, LANES), lambda b, i: (0, 0)),
                  pl.BlockSpec((QB, HW), lambda b, i: (b * nq + i, C_QB // HW)),
                  pl.BlockSpec((1, 1, nch, LANES), lambda b, i: (0, b, 0, 0)),
                  pl.BlockSpec((1, 1, nch, LANES), lambda b, i: (1, b, 0, 0)),
                  pl.BlockSpec((t, 2 * LANES), lambda b, i: (b, C_SEL // (2 * LANES))),
                  pl.BlockSpec((t, 2 * LANES), lambda b, i: (b, C_WIN // (2 * LANES))),
                  pl.BlockSpec((QB, MISC_W), lambda b, i: (b * nq + i, C_MISC // MISC_W))],
        out_specs=pl.BlockSpec((QB, HW), lambda b, i: (b * nq + i, 0)),
        out_shape=jax.ShapeDtypeStruct((nb * t, HW), BF16),
        scratch_shapes=[pltpu.VMEM((nq, LANES, QB), BF16), pltpu.VMEM((nq, LANES, QB), BF16),
                        pltpu.VMEM((LANES, NSA_G * QB), F32)],
        compiler_params=_params("arbitrary", "arbitrary"),
        name="nsa_prompt",
    )(t5, pb, comp, comp, pb, pb, pf)


def _page_specs(npg, block, layer, tail):
    def spec(j):
        return pl.BlockSpec(block, lambda b, pt: (pt[b * npg + j], layer) + tail)
    return [spec(j) for j in range(npg)]


def _cum_sample_body(pages, m_ref, bf_ref, logf_ref):
    npg = len(pages)
    x = jnp.concatenate([pg[0, 0] for pg in pages], axis=0)
    lane = _iota(x.shape, 1)
    s = 1
    while s < PAGE:
        x = x + jnp.where(lane >= s, pltpu.roll(x, s, axis=1), 0.0)
        s *= 2
    off = jnp.zeros((N_HEADS, 1), F32)
    cum_pages = []
    for j in range(npg):
        blk = x[j * N_HEADS:(j + 1) * N_HEADS]
        cum_pages.append(blk + off)
        off = off + blk[:, PAGE - 1:PAGE]
    eye = _iota((N_HEADS, LANES), 0) == _iota((N_HEADS, LANES), 1)
    tot = jnp.sum(jnp.where(eye, off, 0.0), axis=0, keepdims=True)
    logf = _log_sigmoid(m_ref[:, 0:LANES] + bf_ref[...])
    logf_ref[...] = logf
    row = _iota(logf.shape, 0)
    c = logf
    s = 1
    while s < DEC_T:
        c = c + jnp.where(row >= s, pltpu.roll(c, s, axis=0), 0.0)
        s *= 2
    return cum_pages, c + tot


def _block_diag_q(q):
    qt = jnp.concatenate([q] * N_HEADS, axis=0)
    same = jnp.right_shift(_iota(qt.shape, 1), 6) == jnp.right_shift(_iota(qt.shape, 0), 3)
    return jnp.where(same, qt * SCALE, 0.0).astype(BF16)


def _rows_per_head(x):
    return jnp.concatenate([jnp.broadcast_to(x[h:h + 1], (DEC_T, x.shape[1])) for h in range(N_HEADS)], axis=0)


def _col_per_head(x):
    return jnp.concatenate([x[:, h:h + 1] for h in range(N_HEADS)], axis=0)


def _own_head_lanes(acc):
    lane_h = jnp.right_shift(_iota((DEC_T, HW), 1), 6)
    out = jnp.zeros((DEC_T, HW), F32)
    for h in range(N_HEADS):
        out = jnp.where(lane_h == h, acc[h * DEC_T:(h + 1) * DEC_T], out)
    return out


def _pad_rows(x, n):
    return jnp.concatenate([x, jnp.zeros((n - x.shape[0], x.shape[1]), x.dtype)], axis=0)


def _new_key_mask(strict):
    shape = (N_HEADS * DEC_T, PAGE)
    t_row = jnp.bitwise_and(_iota(shape, 0), DEC_T - 1)
    return (_iota(shape, 1) < t_row) if strict else (_iota(shape, 1) <= t_row)


def _fox_sample_body(pages, q_ref, k_ref, v_ref, cum_pages, cum_new, o_ref):
    npg = len(pages)
    qbd = _block_diag_q(q_ref[...])
    cq = _col_per_head(cum_new)
    cum_new_t = _pad_rows(cum_new, LANES).T[0:N_HEADS]
    raw = [_dot(qbd, pages[j][0, 0, 0].astype(BF16)) for j in range(npg)]
    s_pages = [raw[j] + cq - _rows_per_head(cum_pages[j]) for j in range(npg)]
    k_new = _pad_rows(k_ref[...], PAGE).astype(BF16)
    s_new = _dot_nt(qbd, k_new) + cq - _rows_per_head(cum_new_t)
    mask_new = _new_key_mask(strict=False)
    s_new = jnp.where(mask_new, s_new, NEG_INF)
    m = jnp.max(s_new, axis=-1, keepdims=True)
    for s in s_pages:
        m = jnp.maximum(m, jnp.max(s, axis=-1, keepdims=True))
    p_new = jnp.where(mask_new, jnp.exp(s_new - m), 0.0)
    l = jnp.sum(p_new, axis=-1, keepdims=True)
    acc = _dot(p_new.astype(BF16), _pad_rows(v_ref[...], PAGE).astype(BF16))
    for j in range(npg):
        pr = jnp.exp(s_pages[j] - m)
        l = l + jnp.sum(pr, axis=-1, keepdims=True)
        acc = acc + _dot_nt(pr.astype(BF16), pages[j][0, 0, 1].astype(BF16))
    o_ref[...] = _own_head_lanes(acc / jnp.maximum(l, TINY))


def _sb_sample_body(pages, q_ref, k_ref, v_ref, o_ref):
    npg = len(pages)
    qbd = _block_diag_q(q_ref[...])
    tri = _tri(PAGE)
    z = _dot_nt(qbd, _pad_rows(k_ref[...], PAGE).astype(BF16))
    mask = _new_key_mask(strict=True)
    sp = _softplus(z)
    lk = jnp.where(mask, -sp, 0.0)
    between = _split_dot(lk, tri)
    a = jnp.where(mask, jnp.exp(z - sp + between), 0.0)
    acc = _dot(a.astype(BF16), _pad_rows(v_ref[...], PAGE).astype(BF16))
    r = jnp.sum(lk, axis=-1, keepdims=True)
    rows = N_HEADS * DEC_T
    zs = [_dot(qbd, pages[j][0, 0, 0].astype(BF16)) for j in range(npg)]
    sps = [_softplus(z) for z in zs]
    within = _split_dot(jnp.concatenate([-sp for sp in sps], axis=0), tri)
    for j in reversed(range(npg)):
        a = jnp.exp(zs[j] - sps[j] + within[j * rows:(j + 1) * rows] + r)
        acc = acc + _dot_nt(a.astype(BF16), pages[j][0, 0, 1].astype(BF16))
        r = r - jnp.sum(sps[j], axis=-1, keepdims=True)
    o_ref[...] = _own_head_lanes(acc)


def _compress_sample_body(pages, w1_ref, pe_ref, w2_ref, o_ref, ch_ref, xp_ref):
    npg = len(pages)
    per_page = PAGE // CMP_STRIDE
    for j in range(npg):
        for c in range(2):
            xp_ref[2 * j + c] = pages[j][0, 0, 0, c].T
    for j in range(npg):
        for c in range(2):
            for r in range(CMP_STRIDE):
                ch_ref[c, j * per_page:(j + 1) * per_page, r * LANES:(r + 1) * LANES] = (
                    xp_ref[2 * j + c, pl.ds(r, per_page, stride=CMP_STRIDE), :])
    _compress_chunks(ch_ref, w1_ref, pe_ref, w2_ref, o_ref)


def _sample_fox_kernel(pt_ref, *refs, npg):
    logf_pages, fox_pages, cmp_pages = refs[0:npg], refs[npg:2 * npg], refs[2 * npg:3 * npg]
    (m_ref, bf_ref, q_ref, k_ref, v_ref, w1_ref, pe_ref, w2_ref,
     logf_ref, o_ref, comp_ref, ch_ref, xp_ref) = refs[3 * npg:]
    cum_pages, cum_new = _cum_sample_body(logf_pages, m_ref, bf_ref, logf_ref)
    _fox_sample_body(fox_pages, q_ref, k_ref, v_ref, cum_pages, cum_new, o_ref)
    _compress_sample_body(cmp_pages, w1_ref, pe_ref, w2_ref, comp_ref, ch_ref, xp_ref)


def _sample_fox(pt, logf_view, fox_view, nsa_view, pf, bf, w1big, pe_rows, w2p, layer, nbs, npg, row0):
    kernel = functools.partial(_sample_fox_kernel, npg=npg)
    nch = npg * PAGE // CMP_STRIDE
    rb = row0 // DEC_T
    grid_spec = pltpu.PrefetchScalarGridSpec(
        num_scalar_prefetch=1,
        grid=(nbs,),
        in_specs=_page_specs(npg, (1, 1, N_HEADS, PAGE), layer, (0, 0))
        + _page_specs(npg, (1, 1, 2, HW, PAGE), layer, (0, 0, 0))
        + _page_specs(npg, (1, 1, 1, 2, LANES, PAGE), layer, (0, 0, 0, 0))
        + [pl.BlockSpec((DEC_T, MISC_W), lambda b, pt: (rb + b, C_MISC // MISC_W)),
           pl.BlockSpec((1, LANES), lambda b, pt: (0, 0)),
           pl.BlockSpec((DEC_T, HW), lambda b, pt: (rb + b, C_QA // HW)),
           pl.BlockSpec((DEC_T, HW), lambda b, pt: (rb + b, C_KA // HW)),
           pl.BlockSpec((DEC_T, HW), lambda b, pt: (rb + b, C_VA // HW))]
        + _compress_weight_specs(layer, lambda idx: (lambda b, pt: idx)),
        out_specs=[pl.BlockSpec((DEC_T, LANES), lambda b, pt: (b, 0)),
                   pl.BlockSpec((DEC_T, HW), lambda b, pt: (b, 0)),
                   pl.BlockSpec((2, 1, nch, LANES), lambda b, pt: (0, b, 0, 0))],
        scratch_shapes=[pltpu.VMEM((2, nch, CHUNK_W), F32), pltpu.VMEM((2 * npg, PAGE, LANES), F32)],
    )
    return pl.pallas_call(
        kernel,
        grid_spec=grid_spec,
        out_shape=[jax.ShapeDtypeStruct((nbs * DEC_T, LANES), F32),
                   jax.ShapeDtypeStruct((nbs * DEC_T, HW), F32),
                   jax.ShapeDtypeStruct((2, nbs, nch, LANES), BF16)],
        compiler_params=_params("arbitrary"),
        name="sample_fox_compress",
    )(pt, *([logf_view] * npg), *([fox_view] * npg), *([nsa_view] * npg), pf, bf, pf, pf, pf, w1big, pe_rows, w2p)


def _nsa_sample_body(pages, q_ref, seln_ref, winn_ref, misc_ref, ck_ref, cv_ref, wst_ref, t5_ref, exp_ref, o_ref, n_s):
    npg = len(pages)
    past = npg * PAGE
    rows = N_HEADS * DEC_T
    nch = ck_ref.shape[2]
    wb = wst_ref.shape[4]
    lane8 = _iota((DEC_T, LANES), 1)
    t5c = t5_ref[...]

    def lut(rel):
        return _lut(t5c, _bucket(rel))

    def t_of(shape):
        return jnp.bitwise_and(_iota(shape, 0), DEC_T - 1)

    def g_rows(x):
        return jnp.concatenate([x] * NSA_HG, axis=0)

    q = q_ref[...]
    qrows = []
    for head in range(N_HEADS):
        g = head // NSA_HG
        x = q[:, (head // 2) * LANES:(head // 2 + 1) * LANES] * SCALE
        if head % 2 != g:
            x = pltpu.roll(x, HEAD_DIM, axis=1)
        qrows.append(jnp.where((lane8 >= HEAD_DIM) == (g == 1), x, 0.0))
    qbd = jnp.concatenate(qrows, axis=0).astype(BF16)
    bias_far = t5c[:, N_BUCKETS - 1:N_BUCKETS]

    raw_c = _dot_nt(qbd, ck_ref[0, 0])
    raw_pages = [_dot(qbd, pages[j][0, 0, 0, 0].astype(BF16)) for j in range(npg)]
    seln = _pad_rows(seln_ref[...], PAGE).astype(BF16)
    winn = _pad_rows(winn_ref[...], PAGE).astype(BF16)
    raw_new = _dot_nt(qbd, seln[:, 0:LANES])
    raw_w = _dot(qbd, wst_ref[0, 0, 0].astype(BF16))
    raw_wn = _dot_nt(qbd, winn[:, 0:LANES])

    qpos_c = past + t_of((rows, nch))
    rel_c = qpos_c - (_iota((rows, nch), 1) * CMP_STRIDE + (CMP_BLOCK - 1))
    mask_c = (rel_c >= 0) & (_iota((rows, nch), 1) < nch - 1)
    pc = _masked_softmax(raw_c + lut(rel_c), mask_c)
    psums = []
    for g in range(NSA_G):
        base = g * NSA_HG * DEC_T
        psum = pc[base:base + DEC_T]
        for hg in range(1, NSA_HG):
            psum = psum + pc[base + hg * DEC_T:base + (hg + 1) * DEC_T]
        psums.append(psum)

    rel_new = t_of((rows, PAGE)) - _iota((rows, PAGE), 1)
    in_new = _iota((rows, PAGE), 1) < DEC_T
    bias_new = lut(rel_new)
    rel_w = (wb + t_of((rows, wb))) - _iota((rows, wb), 1)
    mask_w = (rel_w >= 0) & (rel_w < WINDOW)
    s_w = jnp.where(mask_w, raw_w + lut(rel_w), NEG_INF)
    mask_wn = (rel_new >= 0) & (rel_new < WINDOW) & in_new
    s_wn = jnp.where(mask_wn, raw_wn + bias_new, NEG_INF)
    m = jnp.maximum(jnp.max(s_w, axis=-1, keepdims=True), jnp.max(s_wn, axis=-1, keepdims=True))
    p_w = jnp.where(mask_w, jnp.exp(s_w - m), 0.0)
    p_wn = jnp.where(mask_wn, jnp.exp(s_wn - m), 0.0)
    l_w = jnp.sum(p_w, axis=-1, keepdims=True) + jnp.sum(p_wn, axis=-1, keepdims=True)

    score = _split_dot(jnp.concatenate(psums, axis=0), _overlap(nch))
    o_cmp = _dot(pc.astype(BF16), cv_ref[0, 0])
    acc_w = _dot_nt(p_w.astype(BF16), wst_ref[0, 0, 1].astype(BF16)) + _dot(p_wn.astype(BF16), winn[:, LANES:2 * LANES])
    o_win = acc_w / jnp.maximum(l_w, TINY)

    qpos_g = past + jnp.bitwise_and(_iota((NSA_G * DEC_T, 1), 0), DEC_T - 1)
    picked = _select_blocks(score, qpos_g, n_s)
    picked = jnp.concatenate([g_rows(picked[g * DEC_T:(g + 1) * DEC_T]) for g in range(NSA_G)], axis=0)
    sel_past = _dot(picked.astype(BF16), exp_ref[...])
    last_blk = past // SEL_BLOCK
    sel_new = jnp.sum(jnp.where(_iota((rows, LANES), 1) == last_blk, picked, 0.0), axis=-1, keepdims=True) > 0.5
    s_pages, m_pages = [], []
    for j in range(npg):
        if past - (j + 1) * PAGE + 1 >= MAX_DISTANCE:
            s = raw_pages[j] + bias_far
        else:
            s = raw_pages[j] + lut(past + t_of((rows, PAGE)) - (j * PAGE + _iota((rows, PAGE), 1)))
        mk = sel_past[:, j * PAGE:(j + 1) * PAGE] > 0.5
        s_pages.append(jnp.where(mk, s, NEG_INF))
        m_pages.append(mk)
    mask_n = sel_new & (rel_new >= 0) & in_new
    s_new = jnp.where(mask_n, raw_new + bias_new, NEG_INF)
    m = jnp.max(s_new, axis=-1, keepdims=True)
    for s in s_pages:
        m = jnp.maximum(m, jnp.max(s, axis=-1, keepdims=True))
    p_new = jnp.where(mask_n, jnp.exp(s_new - m), 0.0)
    l = jnp.sum(p_new, axis=-1, keepdims=True)
    probs = []
    for j in range(npg):
        pr = jnp.where(m_pages[j], jnp.exp(s_pages[j] - m), 0.0)
        l = l + jnp.sum(pr, axis=-1, keepdims=True)
        probs.append(pr.astype(BF16))
    acc = _dot(p_new.astype(BF16), seln[:, LANES:2 * LANES])
    for j in range(npg):
        acc = acc + _dot_nt(probs[j], pages[j][0, 0, 0, 1].astype(BF16))
    o_sel = acc / jnp.maximum(l, TINY)

    sig = _sigmoid(misc_ref[:, 0:LANES])

    def gate(br):
        return jnp.concatenate(
            [sig[:, 8 + br * N_HEADS + h:9 + br * N_HEADS + h] for h in range(N_HEADS)], axis=0)

    o_all = gate(0) * o_cmp + gate(1) * o_sel + gate(2) * o_win
    pieces = []
    for head in range(N_HEADS):
        x = o_all[head * DEC_T:(head + 1) * DEC_T]
        if head % 2 != head // NSA_HG:
            x = pltpu.roll(x, HEAD_DIM, axis=1)
        pieces.append(x)
    for p in range(N_HEADS // 2):
        o_ref[:, p * LANES:(p + 1) * LANES] = jnp.where(lane8 < HEAD_DIM, pieces[2 * p], pieces[2 * p + 1])


def _sample_sb_nsa_kernel(pt_ref, *refs, npg, n_s):
    sb_pages, sel_pages = refs[0:npg], refs[npg:2 * npg]
    (qc_ref, kc_ref, vc_ref, q_ref, seln_ref, winn_ref, misc_ref, ck_ref, cv_ref, wst_ref, t5_ref, exp_ref,
     o_sb_ref, o_nsa_ref) = refs[2 * npg:]
    _sb_sample_body(sb_pages, qc_ref, kc_ref, vc_ref, o_sb_ref)
    _nsa_sample_body(sel_pages, q_ref, seln_ref, winn_ref, misc_ref, ck_ref, cv_ref, wst_ref, t5_ref, exp_ref,
                     o_nsa_ref, n_s)


def _sample_sb_nsa(pt, sb_view, nsa_view, pf, comp, win_view, t5col, expand, layer, nbs, npg, row0):
    nch = comp.shape[2]
    wb = win_view.shape[4]
    past = npg * PAGE
    kernel = functools.partial(_sample_sb_nsa_kernel, npg=npg, n_s=-(-(past + DEC_T) // SEL_BLOCK))
    rb = row0 // DEC_T
    grid_spec = pltpu.PrefetchScalarGridSpec(
        num_scalar_prefetch=1,
        grid=(nbs,),
        in_specs=_page_specs(npg, (1, 1, 2, HW, PAGE), layer, (0, 0, 0))
        + _page_specs(npg, (1, 1, 1, 2, LANES, PAGE), layer, (1, 0, 0, 0)) + [
            pl.BlockSpec((DEC_T, HW), lambda b, pt: (rb + b, C_QC // HW)),
            pl.BlockSpec((DEC_T, HW), lambda b, pt: (rb + b, C_KC // HW)),
            pl.BlockSpec((DEC_T, HW), lambda b, pt: (rb + b, C_VC // HW)),
            pl.BlockSpec((DEC_T, HW), lambda b, pt: (rb + b, C_QB // HW)),
            pl.BlockSpec((DEC_T, 2 * LANES), lambda b, pt: (rb + b, C_SEL // (2 * LANES))),
            pl.BlockSpec((DEC_T, 2 * LANES), lambda b, pt: (rb + b, C_WIN // (2 * LANES))),
            pl.BlockSpec((DEC_T, MISC_W), lambda b, pt: (rb + b, C_MISC // MISC_W)),
            pl.BlockSpec((1, 1, nch, LANES), lambda b, pt: (0, b, 0, 0)),
            pl.BlockSpec((1, 1, nch, LANES), lambda b, pt: (1, b, 0, 0)),
            pl.BlockSpec((1, 1, 2, LANES, wb), lambda b, pt: (b, layer, 0, 0, 0)),
            pl.BlockSpec((N_HEADS * DEC_T, LANES), lambda b, pt: (0, 0)),
            pl.BlockSpec((LANES, past), lambda b, pt: (0, 0))],
        out_specs=[pl.BlockSpec((DEC_T, HW), lambda b, pt: (b, 0)), pl.BlockSpec((DEC_T, HW), lambda b, pt: (b, 0))],
    )
    return pl.pallas_call(
        kernel,
        grid_spec=grid_spec,
        out_shape=[jax.ShapeDtypeStruct((nbs * DEC_T, HW), F32), jax.ShapeDtypeStruct((nbs * DEC_T, HW), F32)],
        compiler_params=_params("arbitrary"),
        name="sample_sb_nsa",
    )(pt, *([sb_view] * npg), *([nsa_view] * npg), pf, pf, pf, pf, pf, pf, pf, comp, comp, win_view, t5col, expand)


def _win_state_kernel(*refs):
    win_ref, new_refs, o_ref = refs[0], refs[1:-1], refs[-1]
    wb = win_ref.shape[4]
    lane = _iota((LANES, LANES), 1)
    for l, new_ref in enumerate(new_refs):
        for kv in range(2):
            shifted = pltpu.roll(win_ref[0, l, kv], wb - DEC_T, axis=1)
            new_t = _pad_rows(new_ref[:, kv * LANES:(kv + 1) * LANES], LANES).T
            tail = jnp.where(lane >= LANES - DEC_T, pltpu.roll(new_t, LANES - DEC_T, axis=1), shifted[:, wb - LANES:wb])
            o_ref[0, l, kv] = jnp.concatenate([shifted[:, 0:wb - LANES], tail], axis=1)


def _win_state(win_view, pfs, nbs, row0):
    _, depth, _, _, wb = win_view.shape
    rb = row0 // DEC_T
    blk = pl.BlockSpec((1, depth, 2, LANES, wb), lambda b: (b, 0, 0, 0, 0))
    return pl.pallas_call(
        _win_state_kernel,
        grid=(nbs,),
        in_specs=[blk] + [pl.BlockSpec((DEC_T, 2 * LANES), lambda b: (rb + b, C_WIN // (2 * LANES))) for _ in pfs],
        out_specs=blk,
        out_shape=jax.ShapeDtypeStruct(win_view.shape, F32),
        compiler_params=_params("arbitrary"),
        name="win_state",
    )(win_view, *pfs)


_STATE_SEGS = ((C_KA, 0, 0), (C_VA, 0, HW), (C_KC, 1, 0), (C_VC, 1, HW), (C_CMP, 2, 0))
_STATE_ROWS = (2 * HW, 2 * HW, HW)


def _prompt_states_kernel(*refs):
    nseg = len(_STATE_SEGS)
    depth = (len(refs) - len(_STATE_ROWS)) // nseg
    ins, outs = refs[:nseg * depth], refs[nseg * depth:]
    for l in range(depth):
        for src, (_, dst, row0) in zip(ins[nseg * l:nseg * (l + 1)], _STATE_SEGS):
            for c in range(HW // LANES):
                outs[dst][0, l, row0 + c * LANES:row0 + (c + 1) * LANES, :] = src[:, c * LANES:(c + 1) * LANES].T


def _prompt_states(pfs, nb, t):
    depth = len(pfs)
    tq = _pick(t, 512, LANES)
    nq = t // tq
    in_specs = [pl.BlockSpec((tq, HW), functools.partial(lambda b, i, c0: (b * nq + i, c0 // HW), c0=c0))
                for _ in range(depth) for c0, _, _ in _STATE_SEGS]
    return pl.pallas_call(
        _prompt_states_kernel,
        grid=(nb, nq),
        in_specs=in_specs,
        out_specs=[pl.BlockSpec((1, depth, w, tq), lambda b, i: (b, 0, 0, i)) for w in _STATE_ROWS],
        out_shape=[jax.ShapeDtypeStruct((nb, depth, w, t), F32) for w in _STATE_ROWS],
        compiler_params=_params("arbitrary", "arbitrary"),
        name="prompt_states",
    )(*[pf for pf in pfs for _ in _STATE_SEGS])


def _reorder_w_in(w_in, d):
    sizes = (HW, 2 * HW, N_HEADS, HW, 6 * NSA_G * HEAD_DIM, 3 * N_HEADS, HW, 2 * HW, 3 * d)
    offs = np.concatenate([[0], np.cumsum(sizes)])
    seg = [w_in[:, :, offs[i]:offs[i + 1]] for i in range(len(sizes))]
    q_a, kv_a, f_a, q_b, kv_b, g_b, q_c, kv_c, g_m = seg
    pad = jnp.zeros(w_in.shape[:2] + (MISC_W - N_HEADS - 3 * N_HEADS,), w_in.dtype)
    return jnp.concatenate([s.astype(BF16) for s in (q_a, kv_a, q_b, q_c, kv_c, kv_b, f_a, g_b, pad, g_m)], axis=-1)


def kernel(x_prompt, x_sample, cache_fox_kv, cache_fox_logf, cache_nsa_kv, cache_sb_kv, state_nsa_win_kv, page_table, norm_mix_g, norm_ffn_g, norm_final_g, w_in, b_forget, t5_table, cmp_pe, cmp_w1, cmp_w2, w_out_a, w_out_b, w_out_c, w_out, router_group_w, router_group_b, router_expert_w, router_expert_b, expert_w_gate, expert_w_up, expert_w_down):
    nb, t, d = x_prompt.shape
    nbs, dec_t, _ = x_sample.shape
    depth = w_in.shape[0]
    npool = cache_fox_kv.shape[0]
    npg = page_table.shape[1]
    past = npg * PAGE
    wb = state_nsa_win_kv.shape[2]
    assert dec_t == DEC_T and t % KB == 0 and d % 128 == 0 and wb == WINDOW and past >= WINDOW
    n_p, n_s_rows = nb * t, nbs * DEC_T

    fox_view = jnp.transpose(cache_fox_kv, (0, 1, 3, 4, 5, 2)).reshape(npool, depth, 2, HW, PAGE)
    sb_view = jnp.transpose(cache_sb_kv, (0, 1, 3, 4, 5, 2)).reshape(npool, depth, 2, HW, PAGE)
    nsa_view = jnp.transpose(cache_nsa_kv, (0, 1, 3, 4, 5, 6, 2)).reshape(npool, depth, 2, 2, LANES, PAGE)
    logf_view = jnp.transpose(cache_fox_logf, (0, 1, 3, 2))
    win_view = jnp.transpose(state_nsa_win_kv, (0, 1, 3, 4, 5, 2)).reshape(nbs, depth, 2, LANES, wb)
    pt = page_table.reshape(-1).astype(jnp.int32)

    w_in_r = _reorder_w_in(w_in, d)
    bf_pad = jnp.pad(b_forget.astype(F32), ((0, 0), (0, LANES - N_HEADS))).reshape(depth, 1, LANES)
    t5 = jnp.pad(t5_table.astype(F32).T, ((0, 0), (0, LANES - N_BUCKETS)))
    t5col = jnp.repeat(t5, DEC_T, axis=0)
    expand = jnp.asarray(np.arange(LANES)[:, None] == (np.arange(past)[None, :] // SEL_BLOCK), BF16)
    w1r = cmp_w1.reshape(depth, 2, 2, CMP_STRIDE, HEAD_DIM, CMP_HIDDEN)
    w1big = jnp.einsum("zchldk,gG->zclgdhGk", w1r, jnp.eye(NSA_G, dtype=w1r.dtype)).reshape(
        depth, 2, CHUNK_W, 2 * NSA_G * CMP_HIDDEN).astype(BF16)
    pe_rows = jnp.broadcast_to(cmp_pe.reshape(depth, 2, 2, CMP_STRIDE, 1, HEAD_DIM),
                               (depth, 2, 2, CMP_STRIDE, NSA_G, HEAD_DIM)).reshape(depth, 2, 2, CHUNK_W)
    pe_rows = jnp.pad(pe_rows, ((0, 0), (0, 0), (0, 6), (0, 0))).astype(F32)
    w2p = jnp.stack([jnp.pad(cmp_w2, ((0, 0), (0, 0), (0, 0), (g * HEAD_DIM, LANES - (g + 1) * HEAD_DIM)))
                     for g in range(NSA_G)], axis=2).astype(BF16)
    w_router = jnp.pad(jnp.concatenate([router_group_w, router_expert_w], axis=-1),
                       ((0, 0), (0, 0), (0, LANES - N_GROUPS - N_EXPERTS))).astype(F32)
    b_router = jnp.pad(jnp.concatenate([router_group_b, router_expert_b], axis=-1),
                       ((0, 0), (0, LANES - N_GROUPS - N_EXPERTS))).astype(F32).reshape(depth, 1, LANES)
    wa, wb_, wc, wo = (w.astype(BF16) for w in (w_out_a, w_out_b, w_out_c, w_out))
    wg, wu, wd = (w.astype(BF16) for w in (expert_w_gate, expert_w_up, expert_w_down))

    h = jnp.concatenate([x_prompt.reshape(n_p, d), x_sample.reshape(n_s_rows, d)], axis=0)
    st_p = [[] for _ in range(5)]
    st_s = [[] for _ in range(5)]
    pfs = []
    for l in range(depth):
        pf, pb = _proj(h, norm_mix_g[l], w_in_r, l)

        logf_p, cum_p = _cum_prompt(pf, bf_pad[l], nb, t)
        cum_t = jnp.transpose(cum_p[:, :N_HEADS].reshape(nb, t, N_HEADS), (0, 2, 1)).reshape(nb, N_HEADS, t // QB, QB)
        o_a_p = _fox_prompt_t(pb, cum_p, cum_t, nb, t)
        o_c_p = _sb_prompt_t(pb, nb, t)
        comp_p = _compress_prompt(pf, w1big, pe_rows, w2p, l, nb, t)
        o_b_p = _nsa_prompt_t(pb, pf, comp_p, t5, nb, t)

        logf_s, o_a_s, comp_s = _sample_fox(pt, logf_view, fox_view, nsa_view, pf, bf_pad[l], w1big, pe_rows, w2p,
                                            l, nbs, npg, n_p)
        o_c_s, o_b_s = _sample_sb_nsa(pt, sb_view, nsa_view, pf, comp_s, win_view, t5col, expand, l, nbs, npg, n_p)

        o_a = jnp.concatenate([o_a_p, o_a_s.astype(BF16)], axis=0)
        o_b = jnp.concatenate([o_b_p, o_b_s.astype(BF16)], axis=0)
        o_c = jnp.concatenate([o_c_p, o_c_s.astype(BF16)], axis=0)
        mixed = _merge(o_a, o_b, o_c, pf, wa, wb_, wc, d, l)
        h = _mm_res(mixed, wo, h, l)
        xn, comb, mcol, mrow, cnt = _router(h, norm_ffn_g[l], w_router[l], b_router[l])
        counts = cnt[:, 0, :N_GROUPS].astype(jnp.int32).reshape(-1)
        h = _moe(xn, comb, mcol, mrow, counts, h, wg, wu, wd, l)

        def rows(c0, width, shape, lo, hi):
            return pf[lo:hi, c0:c0 + width].reshape(shape)

        win_new_p = rows(C_WIN, 2 * LANES, (nb, t, 2, NSA_G, HEAD_DIM), 0, n_p)
        pfs.append(pf)
        st_p[1].append(logf_p[:, :N_HEADS].reshape(nb, t, N_HEADS))
        st_p[4].append(win_new_p[:, t - min(WINDOW, t):])
        st_s[0].append(rows(C_KA, 2 * HW, (nbs, DEC_T, 2, N_HEADS, HEAD_DIM), n_p, n_p + n_s_rows))
        st_s[1].append(logf_s[:, :N_HEADS].reshape(nbs, DEC_T, N_HEADS))
        st_s[2].append(rows(C_CMP, 4 * LANES, (nbs, DEC_T, 2, 2, NSA_G, HEAD_DIM), n_p, n_p + n_s_rows))
        st_s[3].append(rows(C_KC, 2 * HW, (nbs, DEC_T, 2, N_HEADS, HEAD_DIM), n_p, n_p + n_s_rows))

    y_prompt = _final_norm(h, norm_final_g, 0, n_p).reshape(nb, t, d)
    y_sample = _final_norm(h, norm_final_g, n_p, n_s_rows).reshape(nbs, DEC_T, d)
    ss = [jnp.stack(s, axis=1) for s in st_s[:4]]
    fox_t, sb_t, nsa_t = _prompt_states(pfs, nb, t)
    fox_p = jnp.transpose(fox_t.reshape(nb, depth, 2, N_HEADS, HEAD_DIM, t), (0, 1, 5, 2, 3, 4))
    sb_p = jnp.transpose(sb_t.reshape(nb, depth, 2, N_HEADS, HEAD_DIM, t), (0, 1, 5, 2, 3, 4))
    nsa_p = jnp.transpose(nsa_t.reshape(nb, depth, 2, 2, NSA_G, HEAD_DIM, t), (0, 1, 6, 2, 3, 4, 5))
    logf_p_all = jnp.stack(st_p[1], axis=1)
    win_p = jnp.stack(st_p[4], axis=1)
    win_s = _win_state(win_view, pfs, nbs, n_p).reshape(nbs, depth, 2, NSA_G, HEAD_DIM, wb)
    win_s = jnp.transpose(win_s, (0, 1, 5, 2, 3, 4))
    return (y_prompt, y_sample, fox_p, ss[0], logf_p_all, ss[1], nsa_p, ss[2], sb_p, ss[3], win_p, win_s)
```
